```python
import math
import jax
import jax.numpy as jnp
from jax import lax
import numpy as np

D_MODEL = 1024
BATCH = 16
SEQ = 2048
DEPTH = 2

GROUP_WIDTH = D_MODEL // 2
FOX_HEAD_DIM = 64
FOX_HEADS = GROUP_WIDTH // FOX_HEAD_DIM
FOX_BLOCK = 128
FOX_F_BIAS_INIT = 3.0
GDN_HEAD_DIM = 128
GDN_HEADS = GROUP_WIDTH // GDN_HEAD_DIM
GDN_CHUNK = 64
SHORT_CONV = 4
HGRN_EXPAND = 128
HGRN_HEAD_DIM = 128
HGRN_HEADS = GROUP_WIDTH // HGRN_HEAD_DIM
HGRN_CHUNK = 64
M2_HEAD_DIM = 64
M2_HEADS = GROUP_WIDTH // M2_HEAD_DIM
M2_GROUPS = 2
M2_STATE = 128
M2_CHUNK = 128
D_FF = 2816
FFN_CONV = 3
NORM_EPS = 1e-6

HGRN_KW = HGRN_HEADS * HGRN_EXPAND
M2_BC = M2_GROUPS * M2_STATE
EVEN_SPLITS = [GROUP_WIDTH, GROUP_WIDTH, GROUP_WIDTH, FOX_HEADS, 3 * GROUP_WIDTH, GDN_HEADS, GDN_HEADS]
EVEN_IN = sum(EVEN_SPLITS) + GROUP_WIDTH
ODD_SPLITS = [HGRN_KW, HGRN_KW, GROUP_WIDTH, GROUP_WIDTH, GROUP_WIDTH, GROUP_WIDTH + 2 * M2_BC]
ODD_IN = sum(ODD_SPLITS) + M2_HEADS

kernel_name = 'fox_gdn_hgrn2_ssd_hybrid'


def rmsnorm(x, gain):
    xf = x.astype(jnp.float32)
    y = xf * lax.rsqrt(jnp.mean(xf * xf, axis=-1, keepdims=True) + NORM_EPS)
    return (y * gain.astype(jnp.float32)).astype(x.dtype)


def l2norm(x):
    return x * lax.rsqrt(jnp.sum(x * x, axis=-1, keepdims=True) + NORM_EPS)


def causal_dwconv(x, w, b=None):
    K, C = w.shape
    y = lax.conv_general_dilated(x, w[:, None, :].astype(x.dtype), window_strides=(1,),
                                 padding=[(K - 1, 0)], dimension_numbers=('NWC', 'WIO', 'NWC'),
                                 feature_group_count=C)
    if b is not None:
        y = y + b.astype(y.dtype)
    return y


def _split(t, sizes):
    return jnp.split(t, [int(i) for i in np.cumsum(sizes)], axis=-1)


def _to_chunks(t, c):
    b, s = t.shape[:2]
    t = t.reshape((b, s // c, c) + t.shape[2:])
    return jnp.moveaxis(t, 3, 1)


def _from_chunks(t):
    t = jnp.moveaxis(t, 1, 3)
    return t.reshape((t.shape[0], t.shape[1] * t.shape[2]) + t.shape[3:])


def fox_attention(q, k, v, log_f):
    S_ = q.shape[1]
    scale = q.shape[-1] ** -0.5
    c = jnp.cumsum(log_f, axis=1).transpose(0, 2, 1)
    q, k, v = (t.transpose(0, 2, 1, 3) for t in (q, k, v))
    pos = jnp.arange(S_)
    outs = []
    for blk in range(S_ // FOX_BLOCK):
        lo, hi = blk * FOX_BLOCK, (blk + 1) * FOX_BLOCK
        s = jnp.einsum('bhqd,bhkd->bhqk', q[:, :, lo:hi], k[:, :, :hi]) * scale
        s = s + c[:, :, lo:hi, None] - c[:, :, None, :hi]
        s = jnp.where(pos[lo:hi, None] >= pos[None, :hi], s, -jnp.inf)
        p = jax.nn.softmax(s, axis=-1)
        outs.append(jnp.einsum('bhqk,bhkd->bhqd', p, v[:, :, :hi]))
    return jnp.concatenate(outs, axis=2).transpose(0, 2, 1, 3)


def gated_delta_rule(q, k, v, g, beta):
    B_, _, H, K = q.shape
    V = v.shape[-1]
    C = GDN_CHUNK
    q, k, v = _to_chunks(q * K ** -0.5, C), _to_chunks(k, C), _to_chunks(v, C)
    g, beta = _to_chunks(g, C), _to_chunks(beta, C)
    G = jnp.cumsum(g, axis=-1)
    tri = jnp.tril(jnp.ones((C, C), bool))
    gamma = jnp.exp(jnp.where(tri, G[..., :, None] - G[..., None, :], -jnp.inf))
    kb = k * beta[..., None]
    m = jnp.einsum('bhntk,bhnsk->bhnts', kb, k) * gamma
    m = jnp.where(jnp.tril(jnp.ones((C, C), bool), -1), m, 0.0) + jnp.eye(C, dtype=m.dtype)
    rhs = jnp.concatenate([v * beta[..., None], kb * jnp.exp(G)[..., None]], axis=-1)
    sol = lax.linalg.triangular_solve(m, rhs, left_side=True, lower=True)
    u0, w = sol[..., :V], sol[..., V:]
    a_qk = jnp.einsum('bhntk,bhnsk->bhnts', q, k) * gamma
    q_dec = q * jnp.exp(G)[..., None]
    k_dec = k * jnp.exp(G[..., -1:] - G)[..., None]
    g_last = jnp.exp(G[..., -1])
    xs = tuple(jnp.moveaxis(t, 2, 0) for t in (u0, w, a_qk, q_dec, k_dec, g_last))

    def step(S, inp):
        u0_n, w_n, a_n, q_n, k_n, gl_n = inp
        u = u0_n - jnp.einsum('bhck,bhkv->bhcv', w_n, S)
        o = jnp.einsum('bhck,bhkv->bhcv', q_n, S) + jnp.einsum('bhts,bhsv->bhtv', a_n, u)
        S = gl_n[..., None, None] * S + jnp.einsum('bhck,bhcv->bhkv', k_n, u)
        return S, o

    _, o = lax.scan(step, jnp.zeros((B_, H, K, V), jnp.float32), xs)
    return _from_chunks(jnp.moveaxis(o, 0, 2))


def hgrn2_recurrence(q, k, v, log_f):
    B_, _, H, K = q.shape
    V = v.shape[-1]
    C = HGRN_CHUNK
    q, k, v, log_f = (_to_chunks(t, C) for t in (q, k, v, log_f))
    G = jnp.cumsum(log_f, axis=3)
    tri = jnp.tril(jnp.ones((C, C), bool))[..., None]
    xs = tuple(jnp.moveaxis(t, 2, 0) for t in (q, k, v, G))

    def step(S, inp):
        q_n, k_n, v_n, G_n = inp
        dec = jnp.exp(jnp.where(tri, G_n[:, :, :, None, :] - G_n[:, :, None, :, :], -jnp.inf))
        a = jnp.einsum('bhtk,bhsk,bhtsk->bhts', q_n, k_n, dec)
        o = jnp.einsum('bhtk,bhkv->bhtv', q_n * jnp.exp(G_n), S) + jnp.einsum('bhts,bhsv->bhtv', a, v_n)
        G_last = G_n[:, :, -1]
        S = jnp.exp(G_last)[..., None] * S + jnp.einsum(
            'bhsk,bhsv->bhkv', k_n * jnp.exp(G_last[:, :, None] - G_n), v_n)
        return S, o

    _, o = lax.scan(step, jnp.zeros((B_, H, K, V), jnp.float32), xs)
    return _from_chunks(jnp.moveaxis(o, 0, 2))


def ssd_scan(x, dt, A, Bm, Cm):
    B_, _, H, P = x.shape
    N = Bm.shape[-1]
    C = M2_CHUNK
    X = _to_chunks(x * dt[..., None], C)
    Bc, Cc = _to_chunks(Bm, C), _to_chunks(Cm, C)
    A_cs = jnp.cumsum(_to_chunks(dt * A, C), axis=-1)
    tri = jnp.tril(jnp.ones((C, C), bool))
    L = jnp.exp(jnp.where(tri, A_cs[..., :, None] - A_cs[..., None, :], -jnp.inf))
    cb = jnp.einsum('bhnld,bhnsd->bhnls', Cc, Bc)
    y_diag = jnp.einsum('bhnls,bhnsp->bhnlp', cb * L, X)
    states = jnp.einsum('bhnsd,bhns,bhnsp->bhnpd', Bc, jnp.exp(A_cs[..., -1:] - A_cs), X)

    def step(h, inp):
        st, dec = inp
        return dec[..., None, None] * h + st, h

    _, prev = lax.scan(step, jnp.zeros((B_, H, P, N), jnp.float32),
                       (jnp.moveaxis(states, 2, 0), jnp.moveaxis(jnp.exp(A_cs[..., -1]), 2, 0)))
    prev = jnp.moveaxis(prev, 0, 2)
    y_off = jnp.einsum('bhnld,bhnpd,bhnl->bhnlp', Cc, prev, jnp.exp(A_cs))
    return _from_chunks(y_diag + y_off)


def fox_gdn_mixer(h, w_in, fox_f_bias, gdn_conv_w, gdn_A_log, gdn_dt_bias, gdn_norm_gain, w_out):
    B_, S_, _ = h.shape
    f32 = jnp.float32
    hd = lambda t, n: t.reshape(B_, S_, n, -1).astype(f32)
    fq, fk, fv, ff, g_qkv, g_a, g_b, g_gate = _split(h @ w_in, EVEN_SPLITS)
    log_f = jax.nn.log_sigmoid((ff + fox_f_bias).astype(f32))
    o_fox = fox_attention(hd(fq, FOX_HEADS), hd(fk, FOX_HEADS), hd(fv, FOX_HEADS), log_f)
    gq, gk, gv = jnp.split(jax.nn.silu(causal_dwconv(g_qkv, gdn_conv_w)), 3, axis=-1)
    beta = jax.nn.sigmoid(g_b.astype(f32))
    g = -jnp.exp(gdn_A_log.astype(f32)) * jax.nn.softplus(g_a.astype(f32) + gdn_dt_bias.astype(f32))
    o = gated_delta_rule(l2norm(hd(gq, GDN_HEADS)), l2norm(hd(gk, GDN_HEADS)), hd(gv, GDN_HEADS), g, beta)
    o_gdn = rmsnorm(o, gdn_norm_gain) * jax.nn.silu(hd(g_gate, GDN_HEADS))
    mixed = jnp.concatenate([o_fox.reshape(B_, S_, -1), o_gdn.reshape(B_, S_, -1)], axis=-1)
    return mixed.astype(h.dtype) @ w_out


def hgrn2_mamba2_mixer(h, w_in, hgrn_lb_logits, hgrn_norm_gain, m2_conv_w, m2_conv_b,
                       m2_A_log, m2_dt_bias, m2_D, m2_norm_gain, w_out, layer):
    B_, S_, _ = h.shape
    f32 = jnp.float32
    hd = lambda t, n: t.reshape(B_, S_, n, -1).astype(f32)
    hq, hf, hi, hg, mz, mxbc, mdt = _split(h @ w_in, ODD_SPLITS)
    lb_all = jax.nn.softmax(hgrn_lb_logits.astype(f32), axis=0)
    lb_all = jnp.cumsum(lb_all, axis=0) - lb_all[0]
    lb = lb_all[layer].reshape(HGRN_HEADS, HGRN_EXPAND)
    zf = hd(hf, HGRN_HEADS)
    f = lb + (1.0 - lb) * jax.nn.sigmoid(zf)
    k = (1.0 - lb) * jax.nn.sigmoid(-zf)
    o = hgrn2_recurrence(jax.nn.silu(hd(hq, HGRN_HEADS)), k, hd(hi, HGRN_HEADS), jnp.log(f))
    o_hgrn = rmsnorm(o, hgrn_norm_gain) * jax.nn.silu(hd(hg, HGRN_HEADS))
    xbc = jax.nn.silu(causal_dwconv(mxbc, m2_conv_w, m2_conv_b)).astype(f32)
    xs, Bs, Cs = jnp.split(xbc, [GROUP_WIDTH, GROUP_WIDTH + M2_BC], axis=-1)
    rep = M2_HEADS // M2_GROUPS
    Bs = jnp.repeat(Bs.reshape(B_, S_, M2_GROUPS, M2_STATE), rep, axis=2)
    Cs = jnp.repeat(Cs.reshape(B_, S_, M2_GROUPS, M2_STATE), rep, axis=2)
    xs = xs.reshape(B_, S_, M2_HEADS, M2_HEAD_DIM)
    dt = jax.nn.softplus(mdt.astype(f32) + m2_dt_bias.astype(f32))
    A = -jnp.exp(m2_A_log.astype(f32))
    y = ssd_scan(xs, dt, A, Bs, Cs) + m2_D.astype(f32)[:, None] * xs
    y = y.reshape(B_, S_, GROUP_WIDTH) * jax.nn.silu(mz.astype(f32))
    y = rmsnorm(y.reshape(B_, S_, M2_GROUPS, -1), m2_norm_gain.reshape(M2_GROUPS, -1))
    mixed = jnp.concatenate([o_hgrn.reshape(B_, S_, -1), y.reshape(B_, S_, -1)], axis=-1)
    return mixed.astype(h.dtype) @ w_out


def conv_ffn(h, w_up, conv_w, conv_b, w_down):
    u = causal_dwconv(h @ w_up, conv_w, conv_b)
    gate, up = jnp.split(u, 2, axis=-1)
    return (jax.nn.silu(gate) * up) @ w_down


def _dt_bias(key, n):
    dt = jnp.exp(jax.random.uniform(key, (n,), jnp.float32, math.log(1e-3), math.log(1e-1)))
    return dt + jnp.log(-jnp.expm1(-dt))


def setup_inputs(seed: int = 0) -> dict:
    key = jax.random.key(seed)
    ks = iter(jax.random.split(key, 32))
    nrm = lambda shape, scale: jax.random.normal(next(ks), shape, jnp.float32) * scale
    W = GROUP_WIDTH
    return {
        'x': nrm((BATCH, SEQ, D_MODEL), 1.0),
        'norm_gains': 1.0 + nrm((DEPTH, 4, D_MODEL), 0.02),
        'w_out': nrm((DEPTH, D_MODEL, D_MODEL), D_MODEL ** -0.5),
        'ffn_w_up': nrm((DEPTH, D_MODEL, 2 * D_FF), D_MODEL ** -0.5),
        'ffn_conv_w': nrm((DEPTH, FFN_CONV, 2 * D_FF), FFN_CONV ** -0.5),
        'ffn_conv_b': nrm((DEPTH, 2 * D_FF), 0.02),
        'ffn_w_down': nrm((DEPTH, D_FF, D_MODEL), D_FF ** -0.5),
        'even_w_in': nrm((D_MODEL, EVEN_IN), D_MODEL ** -0.5),
        'fox_f_bias': FOX_F_BIAS_INIT + nrm((FOX_HEADS,), 0.1),
        'gdn_conv_w': nrm((SHORT_CONV, 3 * W), SHORT_CONV ** -0.5),
        'gdn_A_log': jnp.log(jax.random.uniform(next(ks), (GDN_HEADS,), jnp.float32, 1.0, 16.0)),
        'gdn_dt_bias': _dt_bias(next(ks), GDN_HEADS),
        'gdn_norm_gain': 1.0 + nrm((GDN_HEAD_DIM,), 0.02),
        'odd_w_in': nrm((D_MODEL, ODD_IN), D_MODEL ** -0.5),
        'hgrn_lb_logits': nrm((DEPTH, HGRN_KW), 0.1),
        'hgrn_norm_gain': 1.0 + nrm((HGRN_HEAD_DIM,), 0.02),
        'm2_conv_w': nrm((SHORT_CONV, W + 2 * M2_BC), SHORT_CONV ** -0.5),
        'm2_conv_b': nrm((W + 2 * M2_BC,), 0.02),
        'm2_A_log': jnp.log(jax.random.uniform(next(ks), (M2_HEADS,), jnp.float32, 1.0, 16.0)),
        'm2_dt_bias': _dt_bias(next(ks), M2_HEADS),
        'm2_D': 1.0 + nrm((M2_HEADS,), 0.1),
        'm2_norm_gain': 1.0 + nrm((W,), 0.02),
    }


def reference(x, norm_gains, w_out, ffn_w_up, ffn_conv_w, ffn_conv_b, ffn_w_down,
              even_w_in, fox_f_bias, gdn_conv_w, gdn_A_log, gdn_dt_bias, gdn_norm_gain,
              odd_w_in, hgrn_lb_logits, hgrn_norm_gain, m2_conv_w, m2_conv_b,
              m2_A_log, m2_dt_bias, m2_D, m2_norm_gain):
    for layer in range(DEPTH):
        g = norm_gains[layer]
        hn = rmsnorm(x, g[0])
        if layer % 2 == 0:
            mix = fox_gdn_mixer(hn, even_w_in, fox_f_bias, gdn_conv_w, gdn_A_log, gdn_dt_bias,
                                gdn_norm_gain, w_out[layer])
        else:
            mix = hgrn2_mamba2_mixer(hn, odd_w_in, hgrn_lb_logits, hgrn_norm_gain, m2_conv_w,
                                     m2_conv_b, m2_A_log, m2_dt_bias, m2_D, m2_norm_gain,
                                     w_out[layer], layer)
        x = x + rmsnorm(mix, g[1])
        hn = rmsnorm(x, g[2])
        x = x + rmsnorm(conv_ffn(hn, ffn_w_up[layer], ffn_conv_w[layer], ffn_conv_b[layer],
                                 ffn_w_down[layer]), g[3])
    return x
```

```python
import functools

import jax
import jax.numpy as jnp
from jax import lax
from jax.experimental import pallas as pl
from jax.experimental.pallas import tpu as pltpu

F32 = jnp.float32
BF16 = jnp.bfloat16

D_MODEL = 1024
GROUP_WIDTH = D_MODEL // 2
FOX_HEAD_DIM = 64
FOX_HEADS = GROUP_WIDTH // FOX_HEAD_DIM
GDN_HEAD_DIM = 128
GDN_HEADS = GROUP_WIDTH // GDN_HEAD_DIM
GDN_CHUNK = 64
SHORT_CONV = 4
HGRN_HEAD_DIM = 128
HGRN_HEADS = GROUP_WIDTH // HGRN_HEAD_DIM
HGRN_CHUNK = 64
HGRN_SUB = 16
M2_HEAD_DIM = 64
M2_HEADS = GROUP_WIDTH // M2_HEAD_DIM
M2_GROUPS = 2
M2_STATE = 128
M2_CHUNK = 128
D_FF = 2816
FFN_CONV = 3
NORM_EPS = 1e-6

LANES = 128
SUBLANES = 8
SMALL_W = LANES
VMEM_LIMIT = 56 * 1024 * 1024

HI = lax.Precision.HIGHEST
NEG_INF = float("-inf")


def _dot(a, b):
    return jnp.dot(a, b, preferred_element_type=F32)


def _dot_hi(a, b):
    return jnp.dot(a, b, preferred_element_type=F32, precision=HI)


def _dot_nt(a, b, precision=None):
    return lax.dot_general(a, b, (((1,), (1,)), ((), ())), preferred_element_type=F32,
                           precision=precision)


def _dot_tn(a, b):
    return lax.dot_general(a, b, (((0,), (0,)), ((), ())), preferred_element_type=F32)


def _rms(x, gain):
    return x * lax.rsqrt(jnp.mean(x * x, axis=-1, keepdims=True) + NORM_EPS) * gain


def _sigmoid(x):
    return 1.0 / (1.0 + jnp.exp(-x))


def _silu(x):
    return x * _sigmoid(x)


def _softplus(x):
    return jnp.maximum(x, 0.0) + jnp.log1p(jnp.exp(-jnp.abs(x)))


def _log_sigmoid(x):
    return jnp.minimum(x, 0.0) - jnp.log1p(jnp.exp(-jnp.abs(x)))


def _tril(n, strict=False):
    r = lax.broadcasted_iota(jnp.int32, (n, n), 0)
    c = lax.broadcasted_iota(jnp.int32, (n, n), 1)
    return (r > c) if strict else (r >= c)


def _lane_lo(shape):
    return lax.broadcasted_iota(jnp.int32, shape, len(shape) - 1) < (LANES // 2)


def _rows_as_lanes(cols, first, n):
    sel = (lax.broadcasted_iota(jnp.int32, (n, LANES), 1)
           == lax.broadcasted_iota(jnp.int32, (n, LANES), 0) + first).astype(F32)
    return _dot_nt(sel, cols, precision=HI)


def _causal_conv(xbuf, x, w_ref, width, rows):
    xbuf[SUBLANES:SUBLANES + rows, :] = x
    y = None
    for k in range(width):
        start = SUBLANES - (width - 1) + k
        term = xbuf[start:start + rows, :] * w_ref[k:k + 1, :]
        y = term if y is None else y + term
    xbuf[0:SUBLANES, :] = xbuf[rows:rows + SUBLANES, :]
    return y


def _params(n_grid):
    return pltpu.CompilerParams(dimension_semantics=("arbitrary",) * n_grid,
                                vmem_limit_bytes=VMEM_LIMIT)


def _resident(shape):
    nd = len(shape)
    return pl.BlockSpec(shape, lambda *_: (0,) * nd)


def _inproj_kernel(x_ref, g_ref, w_ref, *out_refs, n_a, n_b, cn):
    hn = _rms(x_ref[...], g_ref[...]).astype(BF16)
    outs = list(out_refs)
    if n_a:
        oa = outs.pop(0)
        for c0 in range(0, n_a, cn):
            c1 = min(c0 + cn, n_a)
            oa[:, c0:c1] = _dot(hn, w_ref[:, c0:c1]).astype(BF16)
    ob = outs.pop(0)
    for c0 in range(0, n_b, cn):
        c1 = min(c0 + cn, n_b)
        ob[:, c0:c1] = _dot(hn, w_ref[:, n_a + c0:n_a + c1])


def _inproj(x2, gain, w, n_a, n_b, tm):
    T = x2.shape[0]
    out_shape, out_specs = [], []
    if n_a:
        out_shape.append(jax.ShapeDtypeStruct((T, n_a), BF16))
        out_specs.append(pl.BlockSpec((tm, n_a), lambda i: (i, 0)))
    out_shape.append(jax.ShapeDtypeStruct((T, n_b), F32))
    out_specs.append(pl.BlockSpec((tm, n_b), lambda i: (i, 0)))
    return pl.pallas_call(
        functools.partial(_inproj_kernel, n_a=n_a, n_b=n_b, cn=512),
        grid=(T // tm,),
        in_specs=[pl.BlockSpec((tm, D_MODEL), lambda i: (i, 0)),
                  _resident((1, D_MODEL)),
                  _resident((D_MODEL, n_a + n_b))],
        out_specs=out_specs,
        out_shape=out_shape,
        compiler_params=_params(1),
        name="inproj",
    )(x2, gain, w)


def _outproj_kernel(m_ref, w_ref, x_ref, g_ref, o_ref):
    y = _dot(m_ref[...], w_ref[...])
    o_ref[...] = x_ref[...] + _rms(y, g_ref[...])


def _outproj(mixed, w, x2, gain, tm):
    T = x2.shape[0]
    return pl.pallas_call(
        _outproj_kernel,
        grid=(T // tm,),
        in_specs=[pl.BlockSpec((tm, D_MODEL), lambda i: (i, 0)),
                  _resident((D_MODEL, D_MODEL)),
                  pl.BlockSpec((tm, D_MODEL), lambda i: (i, 0)),
                  _resident((1, D_MODEL))],
        out_specs=pl.BlockSpec((tm, D_MODEL), lambda i: (i, 0)),
        out_shape=jax.ShapeDtypeStruct((T, D_MODEL), F32),
        compiler_params=_params(1),
        name="outproj",
    )(mixed, w, x2, gain)


def _ffn_kernel(x_ref, gpre_ref, wup_ref, cw_ref, cb_ref, wdn_ref, gpost_ref, o_ref,
                ubuf, carry, act, *, tm, fc):
    @pl.when(pl.program_id(1) == 0)
    def _():
        carry[...] = jnp.zeros_like(carry)

    x = x_ref[...]
    hn = _rms(x, gpre_ref[...]).astype(BF16)
    for c in range(D_FF // fc):
        halves = []
        for half in range(2):
            col = half * D_FF + c * fc
            u = _dot(hn, wup_ref[:, col:col + fc])
            ubuf[half, 0:SUBLANES, :] = carry[:, col:col + fc]
            ubuf[half, SUBLANES:SUBLANES + tm, :] = u
            carry[:, col:col + fc] = u[tm - SUBLANES:tm, :]
            y = cb_ref[:, col:col + fc]
            for k in range(FFN_CONV):
                start = SUBLANES - (FFN_CONV - 1) + k
                y = y + ubuf[half, start:start + tm, :] * cw_ref[k:k + 1, col:col + fc]
            halves.append(y)
        act[:, c * fc:(c + 1) * fc] = (_silu(halves[0]) * halves[1]).astype(BF16)
    y = _dot(act[...], wdn_ref[...])
    o_ref[...] = x + _rms(y, gpost_ref[...])


def _ffn(x2, B, S, gpre, w_up, conv_w, conv_b, w_down, gpost, tm, fc=256):
    nt = S // tm
    row = lambda b, i: (b * nt + i, 0)
    return pl.pallas_call(
        functools.partial(_ffn_kernel, tm=tm, fc=fc),
        grid=(B, nt),
        in_specs=[pl.BlockSpec((tm, D_MODEL), row),
                  _resident((1, D_MODEL)),
                  _resident((D_MODEL, 2 * D_FF)),
                  _resident((FFN_CONV, 2 * D_FF)),
                  _resident((1, 2 * D_FF)),
                  _resident((D_FF, D_MODEL)),
                  _resident((1, D_MODEL))],
        out_specs=pl.BlockSpec((tm, D_MODEL), row),
        out_shape=jax.ShapeDtypeStruct(x2.shape, F32),
        scratch_shapes=[pltpu.VMEM((2, tm + SUBLANES, fc), F32),
                        pltpu.VMEM((SUBLANES, 2 * D_FF), F32),
                        pltpu.VMEM((tm, D_FF), BF16)],
        compiler_params=_params(2),
        name="convffn",
    )(x2, gpre, w_up, conv_w, conv_b, w_down, gpost)


def _foxgate_kernel(s_ref, b_ref, ccol_ref, crow_ref, *, S, blk):
    tri = _tril(blk).astype(F32)
    carry = jnp.zeros((1, SMALL_W), F32)
    for j in range(S // blk):
        z = s_ref[j * blk:(j + 1) * blk, :] + b_ref[...]
        cs = _dot_hi(tri, _log_sigmoid(z)) + carry
        carry = cs[blk - 1:blk, :]
        ccol_ref[j * blk:(j + 1) * blk, :] = cs
        crow_ref[0, :, j * blk:(j + 1) * blk] = _rows_as_lanes(cs, 0, FOX_HEADS)


def _foxgate(rest, B, S, small_blk, bias_row):
    T = B * S
    blk = 256
    return pl.pallas_call(
        functools.partial(_foxgate_kernel, S=S, blk=blk),
        grid=(B,),
        in_specs=[pl.BlockSpec((S, SMALL_W), lambda b: (b, small_blk)),
                  _resident((1, SMALL_W))],
        out_specs=[pl.BlockSpec((S, SMALL_W), lambda b: (b, 0)),
                   pl.BlockSpec((1, FOX_HEADS, S), lambda b: (b, 0, 0))],
        out_shape=[jax.ShapeDtypeStruct((T, SMALL_W), F32),
                   jax.ShapeDtypeStruct((B, FOX_HEADS, S), F32)],
        compiler_params=_params(1),
        name="foxgate",
    )(rest, bias_row)


def _fox_kernel(q_ref, k_ref, v_ref, ccol_ref, crow_ref, o_ref, *, tq):
    p = pl.program_id(1)
    i = pl.program_id(2)
    lo = _lane_lo((1, LANES))
    q = q_ref[...] * jnp.asarray(FOX_HEAD_DIM ** -0.5, BF16)
    zero = jnp.zeros_like(q)
    qs = (jnp.where(lo, q, zero), jnp.where(lo, zero, q))
    lane = lax.broadcasted_iota(jnp.int32, (1, SMALL_W), 1)
    sub = lax.broadcasted_iota(jnp.int32, (FOX_HEADS, 1), 0)
    cc = ccol_ref[...]
    ct = [jnp.sum(jnp.where(lane == 2 * p + e, cc, 0.0), axis=1, keepdims=True) for e in range(2)]
    causal = _tril(tq)

    def block(j, carry, masked):
        m, l, acc = carry
        off = pl.multiple_of(j * tq, tq)
        k = k_ref[pl.ds(off, tq), :]
        v = v_ref[pl.ds(off, tq), :]
        cr = crow_ref[0, :, pl.ds(off, tq)]
        m_new, l_new, alpha, pv = [], [], [], []
        for e in range(2):
            cs = jnp.sum(jnp.where(sub == 2 * p + e, cr, 0.0), axis=0, keepdims=True)
            s = _dot_nt(qs[e], k) + (ct[e] - cs)
            if masked:
                s = jnp.where(causal, s, NEG_INF)
            mn = jnp.maximum(m[e], jnp.max(s, axis=1, keepdims=True))
            pe = jnp.exp(s - mn)
            a = jnp.exp(m[e] - mn)
            m_new.append(mn)
            l_new.append(a * l[e] + jnp.sum(pe, axis=1, keepdims=True))
            alpha.append(a)
            pv.append(_dot(pe.astype(BF16), v))
        acc = jnp.where(lo, alpha[0], alpha[1]) * acc + jnp.where(lo, pv[0], pv[1])
        return tuple(m_new), tuple(l_new), acc

    col = lambda val: jnp.full((tq, 1), val, F32)
    init = ((col(NEG_INF), col(NEG_INF)), (col(0.0), col(0.0)), jnp.zeros((tq, LANES), F32))
    carry = lax.fori_loop(0, i, lambda j, c: block(j, c, False), init)
    _, l, acc = block(i, carry, True)
    o_ref[...] = (acc / jnp.where(lo, l[0], l[1])).astype(BF16)


def _fox(qkv, ccol, crow, B, S, tq):
    T = B * S
    nq = S // tq
    npair = FOX_HEADS // 2
    return pl.pallas_call(
        functools.partial(_fox_kernel, tq=tq),
        grid=(B, npair, nq),
        in_specs=[pl.BlockSpec((tq, LANES), lambda b, p, i: (b * nq + i, p)),
                  pl.BlockSpec((S, LANES), lambda b, p, i: (b, npair + p)),
                  pl.BlockSpec((S, LANES), lambda b, p, i: (b, 2 * npair + p)),
                  pl.BlockSpec((tq, SMALL_W), lambda b, p, i: (b * nq + i, 0)),
                  pl.BlockSpec((1, FOX_HEADS, S), lambda b, p, i: (b, 0, 0))],
        out_specs=pl.BlockSpec((tq, LANES), lambda b, p, i: (b * nq + i, p)),
        out_shape=jax.ShapeDtypeStruct((T, GROUP_WIDTH), BF16),
        compiler_params=_params(3),
        name="fox",
    )(qkv, qkv, qkv, ccol, crow)


def _unit_lower_inverse(m_strict, n):
    eye = (lax.broadcasted_iota(jnp.int32, (n, n), 0)
           == lax.broadcasted_iota(jnp.int32, (n, n), 1)).astype(F32)
    pw = -m_strict
    inv = eye + pw
    steps = 1
    while steps * 2 < n:
        pw = _dot_hi(pw, pw)
        inv = inv + _dot_hi(inv, pw)
        steps *= 2
    return inv


def _gdn_kernel(qkv_ref, gate_ref, small_ref, cw_ref, alog_ref, dtb_ref, gain_ref, o_ref,
                xbuf, state, *, a_lane, b_lane):
    C = GDN_CHUNK
    W = GROUP_WIDTH
    D = GDN_HEAD_DIM

    @pl.when(pl.program_id(1) == 0)
    def _():
        xbuf[0:SUBLANES, :] = jnp.zeros((SUBLANES, 3 * W), F32)
        state[...] = jnp.zeros_like(state)

    act = _silu(_causal_conv(xbuf, qkv_ref[...], cw_ref, SHORT_CONV, C))
    small = small_ref[...]
    g_all = -jnp.exp(alog_ref[...]) * _softplus(small + dtb_ref[...])
    beta_all = _sigmoid(small)
    tri = _tril(C)
    G_all = _dot_hi(tri.astype(F32), g_all)
    G_rows = _rows_as_lanes(G_all, a_lane, SUBLANES)
    scale = D ** -0.5
    for h in range(GDN_HEADS):
        sl = slice(h * D, (h + 1) * D)
        q = act[:, h * D:(h + 1) * D]
        k = act[:, W + h * D:W + (h + 1) * D]
        v = act[:, 2 * W + h * D:2 * W + (h + 1) * D]
        q = q * lax.rsqrt(jnp.sum(q * q, axis=-1, keepdims=True) + NORM_EPS) * scale
        k = k * lax.rsqrt(jnp.sum(k * k, axis=-1, keepdims=True) + NORM_EPS)
        Gc = G_all[:, a_lane + h:a_lane + h + 1]
        Gr = G_rows[h:h + 1, :]
        beta = beta_all[:, b_lane + h:b_lane + h + 1]
        gamma = jnp.exp(jnp.where(tri, Gc - Gr, NEG_INF))
        kb = k * beta
        kb16, k16 = kb.astype(BF16), k.astype(BF16)
        m = jnp.where(_tril(C, strict=True), _dot_nt(kb16, k16) * gamma, 0.0)
        eG = jnp.exp(Gc)
        rhs = jnp.concatenate([v * beta, kb * eG], axis=1)
        sol = _dot_hi(_unit_lower_inverse(m, C), rhs)
        u0, w = sol[:, :D], sol[:, D:]
        a_qk = _dot_nt(q.astype(BF16), k16) * gamma
        G_last = Gc[C - 1:C, :]
        St = state[h]
        St16 = St.astype(BF16)
        u = u0 - _dot(w.astype(BF16), St16)
        u16 = u.astype(BF16)
        o = _dot((q * eG).astype(BF16), St16) + _dot(a_qk.astype(BF16), u16)
        state[h] = jnp.exp(G_last) * St + _dot_tn((k * jnp.exp(G_last - Gc)).astype(BF16), u16)
        o = _rms(o, gain_ref[...]) * _silu(gate_ref[:, sl])
        o_ref[:, sl] = o.astype(BF16)


def _gdn(rest, B, S, conv_w, alog_row, dtb_row, gain_row, a_lane, b_lane):
    T = B * S
    C = GDN_CHUNK
    W = GROUP_WIDTH
    nt = S // C
    row = lambda b, i: b * nt + i
    return pl.pallas_call(
        functools.partial(_gdn_kernel, a_lane=a_lane, b_lane=b_lane),
        grid=(B, nt),
        in_specs=[pl.BlockSpec((C, 3 * W), lambda b, i: (row(b, i), 0)),
                  pl.BlockSpec((C, W), lambda b, i: (row(b, i), 3)),
                  pl.BlockSpec((C, SMALL_W), lambda b, i: (row(b, i), 4 * W // SMALL_W)),
                  _resident((SHORT_CONV, 3 * W)),
                  _resident((1, SMALL_W)),
                  _resident((1, SMALL_W)),
                  _resident((1, GDN_HEAD_DIM))],
        out_specs=pl.BlockSpec((C, W), lambda b, i: (row(b, i), 0)),
        out_shape=jax.ShapeDtypeStruct((T, W), BF16),
        scratch_shapes=[pltpu.VMEM((C + SUBLANES, 3 * W), F32),
                        pltpu.VMEM((GDN_HEADS, GDN_HEAD_DIM, GDN_HEAD_DIM), F32)],
        compiler_params=_params(2),
        name="gdn",
    )(rest, rest, rest, conv_w, alog_row, dtb_row, gain_row)


def _hgrn_kernel(q_ref, f_ref, i_ref, g_ref, lbl_ref, gain_ref, o_ref, state):
    C = HGRN_CHUNK
    D = HGRN_HEAD_DIM
    SB = HGRN_SUB

    @pl.when(pl.program_id(1) == 0)
    def _():
        state[...] = jnp.zeros_like(state)

    logits = lbl_ref[...]
    e = jnp.exp(logits - jnp.max(logits, axis=0, keepdims=True))
    prob = e / jnp.sum(e, axis=0, keepdims=True)
    lb = (prob[0:1, :] + prob[1:2, :]) - prob[0:1, :]

    zf = f_ref[...]
    f = lb + (1.0 - lb) * _sigmoid(zf)
    kk = (1.0 - lb) * _sigmoid(-zf)
    G_all = _dot_hi(_tril(C).astype(F32), jnp.log(f))
    q_all = _silu(q_ref[...])
    col_id = lax.broadcasted_iota(jnp.int32, (SB, C), 1)
    row_id = lax.broadcasted_iota(jnp.int32, (SB, 1), 0)
    for h in range(HGRN_HEADS):
        sl = slice(h * D, (h + 1) * D)
        q, k, v, G = q_all[:, sl], kk[:, sl], i_ref[:, sl], G_all[:, sl]
        St = state[h]
        v16 = v.astype(BF16)
        blocks = []
        for b in range(C // SB):
            r0 = b * SB
            qb, Gb = q[r0:r0 + SB, :], G[r0:r0 + SB, :]
            G0 = G[r0:r0 + 1, :]
            if b:
                qd = qb * jnp.exp(Gb - G0)
                kd = k * jnp.exp(jnp.minimum(G0 - G, 0.0))
                blk = jnp.where(col_id < r0, _dot_nt(qd.astype(BF16), kd.astype(BF16)), 0.0)
            else:
                blk = jnp.zeros((SB, C), F32)
            for s in range(SB):
                dec = jnp.exp(jnp.minimum(Gb - G[r0 + s:r0 + s + 1, :], 0.0))
                cs = jnp.sum(qb * (k[r0 + s:r0 + s + 1, :] * dec), axis=1, keepdims=True)
                blk = jnp.where((col_id == r0 + s) & (row_id >= s), cs, blk)
            blocks.append(blk)
        a = jnp.concatenate(blocks, axis=0)
        o = _dot_nt((q * jnp.exp(G)).astype(BF16), St.astype(BF16)) + _dot(a.astype(BF16), v16)
        G_last = G[C - 1:C, :]
        state[h] = jnp.exp(G_last) * St + _dot_tn(v16, (k * jnp.exp(G_last - G)).astype(BF16))
        o = _rms(o, gain_ref[...]) * _silu(g_ref[:, sl])
        o_ref[:, sl] = o.astype(BF16)


def _hgrn(proj, B, S, lb_logits, gain_row):
    T = B * S
    C = HGRN_CHUNK
    W = GROUP_WIDTH
    nt = S // C
    spec = lambda j: pl.BlockSpec((C, W), lambda b, i: (b * nt + i, j))
    return pl.pallas_call(
        _hgrn_kernel,
        grid=(B, nt),
        in_specs=[spec(0), spec(1), spec(2), spec(3),
                  _resident(lb_logits.shape),
                  _resident((1, HGRN_HEAD_DIM))],
        out_specs=spec(0),
        out_shape=jax.ShapeDtypeStruct((T, W), BF16),
        scratch_shapes=[pltpu.VMEM((HGRN_HEADS, HGRN_HEAD_DIM, HGRN_HEAD_DIM), F32)],
        compiler_params=_params(2),
        name="hgrn2",
    )(proj, proj, proj, proj, lb_logits, gain_row)


def _ssd_kernel(xbc_ref, z_ref, small_ref, cw_ref, cb_ref, alog_ref, dtb_ref, dvec_ref, gain_ref,
                o_ref, xbuf, state):
    L = M2_CHUNK
    W = GROUP_WIDTH
    N = M2_STATE
    CW = W + 2 * M2_GROUPS * N
    pairs_per_group = M2_HEADS // M2_GROUPS // 2

    @pl.when(pl.program_id(1) == 0)
    def _():
        xbuf[0:SUBLANES, :] = jnp.zeros((SUBLANES, CW), F32)
        state[...] = jnp.zeros_like(state)

    xbc = _silu(_causal_conv(xbuf, xbc_ref[...], cw_ref, SHORT_CONV, L) + cb_ref[...])
    dt = _softplus(small_ref[...] + dtb_ref[...])
    tri = _tril(L)
    A_cs = _dot_hi(tri.astype(F32), dt * (-jnp.exp(alog_ref[...])))
    A_rows = _rows_as_lanes(A_cs, 0, M2_HEADS)
    lo = _lane_lo((1, LANES))
    for g in range(M2_GROUPS):
        Bg = xbc[:, W + g * N:W + (g + 1) * N]
        Cg = xbc[:, W + M2_GROUPS * N + g * N:W + M2_GROUPS * N + (g + 1) * N]
        cb = _dot_nt(Cg.astype(BF16), Bg.astype(BF16))
        ys = []
        for pp in range(pairs_per_group):
            p = g * pairs_per_group + pp
            xs = xbc[:, p * LANES:(p + 1) * LANES]
            dtl = jnp.where(lo, dt[:, 2 * p:2 * p + 1], dt[:, 2 * p + 1:2 * p + 2])
            X16 = (xs * dtl).astype(BF16)
            St = state[p]
            St16 = St.astype(BF16)
            y_h, st_h = [], []
            for e in range(2):
                h = 2 * p + e
                ac = A_cs[:, h:h + 1]
                ar = A_rows[h:h + 1, :]
                decay = jnp.exp(jnp.where(tri, ac - ar, NEG_INF))
                y = _dot((cb * decay).astype(BF16), X16) + _dot((Cg * jnp.exp(ac)).astype(BF16), St16)
                last = ac[L - 1:L, :]
                st = jnp.exp(last) * St + _dot_tn((Bg * jnp.exp(last - ac)).astype(BF16), X16)
                y_h.append(y)
                st_h.append(st)
            state[p] = jnp.where(lo, st_h[0], st_h[1])
            ys.append(jnp.where(lo, y_h[0], y_h[1]) + dvec_ref[:, p * LANES:(p + 1) * LANES] * xs)
        gs = slice(g * (W // M2_GROUPS), (g + 1) * (W // M2_GROUPS))
        y = jnp.concatenate(ys, axis=1) * _silu(z_ref[:, gs])
        o_ref[:, gs] = _rms(y, gain_ref[:, gs]).astype(BF16)


def _ssd(proj, B, S, conv_w, conv_b, alog_row, dtb_row, dvec, gain_row):
    T = B * S
    L = M2_CHUNK
    W = GROUP_WIDTH
    CW = W + 2 * M2_GROUPS * M2_STATE
    nt = S // L
    row = lambda b, i: b * nt + i
    return pl.pallas_call(
        _ssd_kernel,
        grid=(B, nt),
        in_specs=[pl.BlockSpec((L, CW), lambda b, i: (row(b, i), 4 * W // CW)),
                  pl.BlockSpec((L, W), lambda b, i: (row(b, i), (4 * W + CW) // W)),
                  pl.BlockSpec((L, SMALL_W), lambda b, i: (row(b, i), (5 * W + CW) // SMALL_W)),
                  _resident((SHORT_CONV, CW)),
                  _resident((1, CW)),
                  _resident((1, SMALL_W)),
                  _resident((1, SMALL_W)),
                  _resident((1, W)),
                  _resident((1, W))],
        out_specs=pl.BlockSpec((L, W), lambda b, i: (row(b, i), 0)),
        out_shape=jax.ShapeDtypeStruct((T, W), BF16),
        scratch_shapes=[pltpu.VMEM((L + SUBLANES, CW), F32),
                        pltpu.VMEM((M2_HEADS // 2, M2_STATE, LANES), F32)],
        compiler_params=_params(2),
        name="ssd",
    )(proj, proj, proj, conv_w, conv_b, alog_row, dtb_row, dvec, gain_row)


def _pad_lanes(v, first, width=SMALL_W):
    v = v.astype(F32)
    return jnp.pad(v, (first, width - first - v.shape[0])).reshape(1, width)


def kernel(x, norm_gains, w_out, ffn_w_up, ffn_conv_w, ffn_conv_b, ffn_w_down,
           even_w_in, fox_f_bias, gdn_conv_w, gdn_A_log, gdn_dt_bias, gdn_norm_gain,
           odd_w_in, hgrn_lb_logits, hgrn_norm_gain, m2_conv_w, m2_conv_b,
           m2_A_log, m2_dt_bias, m2_D, m2_norm_gain):
    B, S, D = x.shape
    assert D == D_MODEL and S % 512 == 0
    T = B * S
    W = GROUP_WIDTH
    tm = 512
    row = lambda v: v.astype(F32).reshape(1, -1)
    x2 = x.reshape(T, D).astype(F32)

    o_ff = 3 * W
    o_qkv = o_ff + FOX_HEADS
    o_a = o_qkv + 3 * W
    o_b = o_a + GDN_HEADS
    o_gate = o_b + GDN_HEADS
    a_lane, b_lane = FOX_HEADS, FOX_HEADS + GDN_HEADS
    pad = jnp.zeros((D, SMALL_W - FOX_HEADS - 2 * GDN_HEADS), even_w_in.dtype)
    w_even = jnp.concatenate([even_w_in[:, :o_ff], even_w_in[:, o_qkv:o_a], even_w_in[:, o_gate:],
                              even_w_in[:, o_ff:o_qkv], even_w_in[:, o_a:o_gate], pad],
                             axis=1).astype(BF16)
    g = norm_gains[0]
    qkv, rest = _inproj(x2, row(g[0]), w_even, 3 * W, 4 * W + SMALL_W, tm)
    ccol, crow = _foxgate(rest, B, S, 4 * W // SMALL_W, _pad_lanes(fox_f_bias, 0))
    o_fox = _fox(qkv, ccol, crow, B, S, 256)
    o_gdn = _gdn(rest, B, S, gdn_conv_w.astype(F32), _pad_lanes(gdn_A_log, a_lane),
                 _pad_lanes(gdn_dt_bias, a_lane), row(gdn_norm_gain), a_lane, b_lane)
    mixed = jnp.concatenate([o_fox, o_gdn], axis=1)
    x2 = _outproj(mixed, w_out[0].astype(BF16), x2, row(g[1]), tm)
    x2 = _ffn(x2, B, S, row(g[2]), ffn_w_up[0].astype(BF16), ffn_conv_w[0].astype(F32),
              row(ffn_conv_b[0]), ffn_w_down[0].astype(BF16), row(g[3]), tm)

    assert hgrn_lb_logits.shape == (2, W)
    CW = W + 2 * M2_GROUPS * M2_STATE
    pad = jnp.zeros((D, SMALL_W - M2_HEADS), odd_w_in.dtype)
    w_odd = jnp.concatenate([odd_w_in[:, :4 * W], odd_w_in[:, 5 * W:5 * W + CW],
                             odd_w_in[:, 4 * W:5 * W], odd_w_in[:, 5 * W + CW:], pad],
                            axis=1).astype(BF16)
    g = norm_gains[1]
    (proj,) = _inproj(x2, row(g[0]), w_odd, 0, w_odd.shape[1], tm)
    o_hgrn = _hgrn(proj, B, S, hgrn_lb_logits.astype(F32), row(hgrn_norm_gain))
    o_ssd = _ssd(proj, B, S, m2_conv_w.astype(F32), row(m2_conv_b), _pad_lanes(m2_A_log, 0),
                 _pad_lanes(m2_dt_bias, 0), row(jnp.repeat(m2_D, M2_HEAD_DIM)), row(m2_norm_gain))
    mixed = jnp.concatenate([o_hgrn, o_ssd], axis=1)
    x2 = _outproj(mixed, w_out[1].astype(BF16), x2, row(g[1]), tm)
    x2 = _ffn(x2, B, S, row(g[2]), ffn_w_up[1].astype(BF16), ffn_conv_w[1].astype(F32),
              row(ffn_conv_b[1]), ffn_w_down[1].astype(BF16), row(g[3]), tm)
    return x2.reshape(B, S, D).astype(x.dtype)
```

```python
import functools

import jax
import jax.numpy as jnp
from jax import lax
from jax.experimental import pallas as pl
from jax.experimental.pallas import tpu as pltpu

F32 = jnp.float32
BF16 = jnp.bfloat16

D_MODEL = 1024
GROUP_WIDTH = D_MODEL // 2
FOX_HEAD_DIM = 64
FOX_HEADS = GROUP_WIDTH // FOX_HEAD_DIM
FOX_AUG = 3
GDN_HEAD_DIM = 128
GDN_HEADS = GROUP_WIDTH // GDN_HEAD_DIM
GDN_CHUNK = 64
SHORT_CONV = 4
HGRN_HEAD_DIM = 128
HGRN_HEADS = GROUP_WIDTH // HGRN_HEAD_DIM
HGRN_CHUNK = 64
HGRN_SUB = 16
M2_HEAD_DIM = 64
M2_HEADS = GROUP_WIDTH // M2_HEAD_DIM
M2_GROUPS = 2
M2_STATE = 128
M2_CHUNK = 128
D_FF = 2816
FFN_CONV = 3
NORM_EPS = 1e-6

LANES = 128
SUBLANES = 8
SMALL_W = LANES
VMEM_LIMIT = 56 * 1024 * 1024

HI = lax.Precision.HIGHEST
NEG_INF = float("-inf")


def _dot(a, b):
    return jnp.dot(a, b, preferred_element_type=F32)


def _dot_hi(a, b):
    return jnp.dot(a, b, preferred_element_type=F32, precision=HI)


def _dot_nt(a, b, precision=None):
    return lax.dot_general(a, b, (((1,), (1,)), ((), ())), preferred_element_type=F32,
                           precision=precision)


def _dot_tn(a, b):
    return lax.dot_general(a, b, (((0,), (0,)), ((), ())), preferred_element_type=F32)


def _rms(x, gain):
    return x * lax.rsqrt(jnp.mean(x * x, axis=-1, keepdims=True) + NORM_EPS) * gain


def _sigmoid(x):
    return 1.0 / (1.0 + jnp.exp(-x))


def _silu(x):
    return x * _sigmoid(x)


def _softplus(x):
    return jnp.maximum(x, 0.0) + jnp.log1p(jnp.exp(-jnp.abs(x)))


def _log_sigmoid(x):
    return jnp.minimum(x, 0.0) - jnp.log1p(jnp.exp(-jnp.abs(x)))


def _tril(n, strict=False):
    r = lax.broadcasted_iota(jnp.int32, (n, n), 0)
    c = lax.broadcasted_iota(jnp.int32, (n, n), 1)
    return (r > c) if strict else (r >= c)


def _lane_lo(shape):
    return lax.broadcasted_iota(jnp.int32, shape, len(shape) - 1) < (LANES // 2)


def _rows_as_lanes(cols, first, n):
    sel = (lax.broadcasted_iota(jnp.int32, (n, LANES), 1)
           == lax.broadcasted_iota(jnp.int32, (n, LANES), 0) + first).astype(F32)
    return _dot_nt(sel, cols, precision=HI)


def _causal_conv(xbuf, x, w_ref, width, rows):
    xbuf[SUBLANES:SUBLANES + rows, :] = x
    y = None
    for k in range(width):
        start = SUBLANES - (width - 1) + k
        term = xbuf[start:start + rows, :] * w_ref[k:k + 1, :]
        y = term if y is None else y + term
    xbuf[0:SUBLANES, :] = xbuf[rows:rows + SUBLANES, :]
    return y


def _params(n_grid):
    return pltpu.CompilerParams(dimension_semantics=("arbitrary",) * n_grid,
                                vmem_limit_bytes=VMEM_LIMIT)


def _resident(shape):
    nd = len(shape)
    return pl.BlockSpec(shape, lambda *_: (0,) * nd)


def _inproj_kernel(x_ref, g_ref, w_ref, *out_refs, n_a, n_b, cn):
    hn = _rms(x_ref[...], g_ref[...]).astype(BF16)
    outs = list(out_refs)
    if n_a:
        oa = outs.pop(0)
        for c0 in range(0, n_a, cn):
            c1 = min(c0 + cn, n_a)
            oa[:, c0:c1] = _dot(hn, w_ref[:, c0:c1]).astype(BF16)
    ob = outs.pop(0)
    for c0 in range(0, n_b, cn):
        c1 = min(c0 + cn, n_b)
        ob[:, c0:c1] = _dot(hn, w_ref[:, n_a + c0:n_a + c1])


def _inproj(x2, gain, w, n_a, n_b, tm):
    T = x2.shape[0]
    out_shape, out_specs = [], []
    if n_a:
        out_shape.append(jax.ShapeDtypeStruct((T, n_a), BF16))
        out_specs.append(pl.BlockSpec((tm, n_a), lambda i: (i, 0)))
    out_shape.append(jax.ShapeDtypeStruct((T, n_b), F32))
    out_specs.append(pl.BlockSpec((tm, n_b), lambda i: (i, 0)))
    return pl.pallas_call(
        functools.partial(_inproj_kernel, n_a=n_a, n_b=n_b, cn=512),
        grid=(T // tm,),
        in_specs=[pl.BlockSpec((tm, D_MODEL), lambda i: (i, 0)),
                  _resident((1, D_MODEL)),
                  _resident((D_MODEL, n_a + n_b))],
        out_specs=out_specs,
        out_shape=out_shape,
        compiler_params=_params(1),
        name="inproj",
    )(x2, gain, w)


def _outproj_kernel(m_ref, w_ref, x_ref, g_ref, o_ref):
    y = _dot(m_ref[...], w_ref[...])
    o_ref[...] = x_ref[...] + _rms(y, g_ref[...])


def _outproj(mixed, w, x2, gain, tm):
    T = x2.shape[0]
    return pl.pallas_call(
        _outproj_kernel,
        grid=(T // tm,),
        in_specs=[pl.BlockSpec((tm, D_MODEL), lambda i: (i, 0)),
                  _resident((D_MODEL, D_MODEL)),
                  pl.BlockSpec((tm, D_MODEL), lambda i: (i, 0)),
                  _resident((1, D_MODEL))],
        out_specs=pl.BlockSpec((tm, D_MODEL), lambda i: (i, 0)),
        out_shape=jax.ShapeDtypeStruct((T, D_MODEL), F32),
        compiler_params=_params(1),
        name="outproj",
    )(mixed, w, x2, gain)


def _ffn_kernel(x_ref, gpre_ref, wup_ref, cw_ref, cb_ref, wdn_ref, gpost_ref, o_ref,
                ubuf, carry, act, *, tm, fc):
    @pl.when(pl.program_id(1) == 0)
    def _():
        carry[...] = jnp.zeros_like(carry)

    x = x_ref[...]
    hn = _rms(x, gpre_ref[...]).astype(BF16)
    for c in range(D_FF // fc):
        halves = []
        for half in range(2):
            col = half * D_FF + c * fc
            u = _dot(hn, wup_ref[:, col:col + fc])
            ubuf[half, 0:SUBLANES, :] = carry[:, col:col + fc]
            ubuf[half, SUBLANES:SUBLANES + tm, :] = u
            carry[:, col:col + fc] = u[tm - SUBLANES:tm, :]
            y = cb_ref[:, col:col + fc]
            for k in range(FFN_CONV):
                start = SUBLANES - (FFN_CONV - 1) + k
                y = y + ubuf[half, start:start + tm, :] * cw_ref[k:k + 1, col:col + fc]
            halves.append(y)
        act[:, c * fc:(c + 1) * fc] = (_silu(halves[0]) * halves[1]).astype(BF16)
    y = _dot(act[...], wdn_ref[...])
    o_ref[...] = x + _rms(y, gpost_ref[...])


def _ffn(x2, B, S, gpre, w_up, conv_w, conv_b, w_down, gpost, tm, fc=256):
    nt = S // tm
    row = lambda b, i: (b * nt + i, 0)
    return pl.pallas_call(
        functools.partial(_ffn_kernel, tm=tm, fc=fc),
        grid=(B, nt),
        in_specs=[pl.BlockSpec((tm, D_MODEL), row),
                  _resident((1, D_MODEL)),
                  _resident((D_MODEL, 2 * D_FF)),
                  _resident((FFN_CONV, 2 * D_FF)),
                  _resident((1, 2 * D_FF)),
                  _resident((D_FF, D_MODEL)),
                  _resident((1, D_MODEL))],
        out_specs=pl.BlockSpec((tm, D_MODEL), row),
        out_shape=jax.ShapeDtypeStruct(x2.shape, F32),
        scratch_shapes=[pltpu.VMEM((2, tm + SUBLANES, fc), F32),
                        pltpu.VMEM((SUBLANES, 2 * D_FF), F32),
                        pltpu.VMEM((tm, D_FF), BF16)],
        compiler_params=_params(2),
        name="convffn",
    )(x2, gpre, w_up, conv_w, conv_b, w_down, gpost)


def _foxgate_kernel(s_ref, b_ref, qk_ref, qt_ref, ka_ref, vt_ref, carry, *, blk):
    W = GROUP_WIDTH

    @pl.when(pl.program_id(1) == 0)
    def _():
        carry[...] = jnp.zeros_like(carry)

    z = s_ref[...] + b_ref[...]
    cs = _dot_hi(_tril(blk).astype(F32), _log_sigmoid(z)) + carry[...]
    carry[...] = cs[blk - 1:blk, :]
    lane = lax.broadcasted_iota(jnp.int32, (1, LANES), 1)
    lo = lane < FOX_HEAD_DIM
    ones = jnp.where(lane < FOX_HEAD_DIM + FOX_AUG, 1.0, 0.0)
    scale = FOX_HEAD_DIM ** -0.5
    for p in range(FOX_HEADS // 2):
        q = qk_ref[:, p * LANES:(p + 1) * LANES].astype(F32) * scale
        k = qk_ref[:, W + p * LANES:W + (p + 1) * LANES].astype(F32)
        for e in range(2):
            h = 2 * p + e
            qe = pltpu.roll(q, FOX_HEAD_DIM, 1) if e else q
            ke = pltpu.roll(k, FOX_HEAD_DIM, 1) if e else k
            rem = -cs[:, h:h + 1]
            aug = jnp.zeros((blk, LANES), F32)
            for piece in range(FOX_AUG):
                part = rem.astype(BF16).astype(F32)
                aug = jnp.where(lane == FOX_HEAD_DIM + piece, part, aug)
                rem = rem - part
            qt_ref[0, 0, h * LANES:(h + 1) * LANES, :] = jnp.where(lo, qe, ones).T.astype(BF16)
            ka_ref[:, h * LANES:(h + 1) * LANES] = jnp.where(lo, ke, aug).astype(BF16)
        v = qk_ref[:, 2 * W + p * LANES:2 * W + (p + 1) * LANES].astype(F32)
        vt_ref[0, 0, p * LANES:(p + 1) * LANES, :] = v.T.astype(BF16)


def _foxgate(qkv, rest, B, S, small_blk, bias_row, blk):
    T = B * S
    W = GROUP_WIDTH
    nb = S // blk
    wide = FOX_HEADS * LANES
    row = lambda b, j: b * nb + j
    return pl.pallas_call(
        functools.partial(_foxgate_kernel, blk=blk),
        grid=(B, nb),
        in_specs=[pl.BlockSpec((blk, SMALL_W), lambda b, j: (row(b, j), small_blk)),
                  _resident((1, SMALL_W)),
                  pl.BlockSpec((blk, 3 * W), lambda b, j: (row(b, j), 0))],
        out_specs=[pl.BlockSpec((1, 1, wide, blk), lambda b, j: (b, j, 0, 0)),
                   pl.BlockSpec((blk, wide), lambda b, j: (row(b, j), 0)),
                   pl.BlockSpec((1, 1, W, blk), lambda b, j: (b, j, 0, 0))],
        out_shape=[jax.ShapeDtypeStruct((B, nb, wide, blk), BF16),
                   jax.ShapeDtypeStruct((T, wide), BF16),
                   jax.ShapeDtypeStruct((B, nb, W, blk), BF16)],
        scratch_shapes=[pltpu.VMEM((1, SMALL_W), F32)],
        compiler_params=_params(2),
        name="foxgate",
    )(rest, bias_row, qkv)


def _fox_kernel(qt_ref, k_ref, vt_ref, o_ref, m_ref, l_ref, acc_ref, *, blk, kvb):
    i = pl.program_id(1)
    kv_id = lax.broadcasted_iota(jnp.int32, (kvb, blk), 0)
    q_id = lax.broadcasted_iota(jnp.int32, (kvb, blk), 1)
    first_head = lax.broadcasted_iota(jnp.int32, (LANES, 1), 0) < FOX_HEAD_DIM
    m_ref[...] = jnp.full(m_ref.shape, NEG_INF, F32)
    l_ref[...] = jnp.zeros_like(l_ref)
    acc_ref[...] = jnp.zeros_like(acc_ref)

    def sub_block(j, sub, masked):
        off = pl.multiple_of(j * blk + sub * kvb, kvb)
        for p in range(FOX_HEADS // 2):
            vt = vt_ref[0, j, p * LANES:(p + 1) * LANES, sub * kvb:(sub + 1) * kvb]
            alpha, pv = [], []
            for e in range(2):
                h = 2 * p + e
                hs = slice(h * LANES, (h + 1) * LANES)
                s = _dot(k_ref[pl.ds(off, kvb), hs], qt_ref[0, 0, hs, :])
                if masked:
                    s = jnp.where(kv_id + sub * kvb <= q_id, s, NEG_INF)
                m_old = m_ref[h:h + 1, :]
                m_new = jnp.maximum(m_old, jnp.max(s, axis=0, keepdims=True))
                pe = jnp.exp(s - m_new)
                a = jnp.exp(m_old - m_new)
                m_ref[h:h + 1, :] = m_new
                l_ref[h:h + 1, :] = a * l_ref[h:h + 1, :] + jnp.sum(pe, axis=0, keepdims=True)
                alpha.append(a)
                pv.append(_dot(vt, pe.astype(BF16)))
            acc_ref[p] = (jnp.where(first_head, alpha[0], alpha[1]) * acc_ref[p]
                          + jnp.where(first_head, pv[0], pv[1]))

    def block(j, masked):
        for sub in range(blk // kvb):
            sub_block(j, sub, masked)

    def body(j, carry):
        block(j, False)
        return carry

    lax.fori_loop(0, i, body, 0)
    block(i, True)
    for p in range(FOX_HEADS // 2):
        l = jnp.where(first_head, l_ref[2 * p:2 * p + 1, :], l_ref[2 * p + 1:2 * p + 2, :])
        o_ref[:, p * LANES:(p + 1) * LANES] = (acc_ref[p] / l).T.astype(BF16)


def _fox(qt, ka, vt, B, S, blk):
    T = B * S
    nq = S // blk
    W = GROUP_WIDTH
    wide = FOX_HEADS * LANES
    return pl.pallas_call(
        functools.partial(_fox_kernel, blk=blk, kvb=128),
        grid=(B, nq),
        in_specs=[pl.BlockSpec((1, 1, wide, blk), lambda b, i: (b, i, 0, 0)),
                  pl.BlockSpec((S, wide), lambda b, i: (b, 0)),
                  pl.BlockSpec((1, nq, W, blk), lambda b, i: (b, 0, 0, 0))],
        out_specs=pl.BlockSpec((blk, W), lambda b, i: (b * nq + i, 0)),
        out_shape=jax.ShapeDtypeStruct((T, W), BF16),
        scratch_shapes=[pltpu.VMEM((FOX_HEADS, blk), F32),
                        pltpu.VMEM((FOX_HEADS, blk), F32),
                        pltpu.VMEM((FOX_HEADS // 2, LANES, blk), F32)],
        compiler_params=_params(2),
        name="fox",
    )(qt, ka, vt)


def _unit_lower_inverse(m_strict, n, nilpotent):
    eye = (lax.broadcasted_iota(jnp.int32, (n, n), 0)
           == lax.broadcasted_iota(jnp.int32, (n, n), 1)).astype(F32)
    pw = -m_strict
    inv = eye + pw
    steps = 1
    while steps * 2 < nilpotent:
        pw16 = pw.astype(BF16)
        pw = _dot(pw16, pw16)
        inv = inv + _dot(inv.astype(BF16), pw.astype(BF16))
        steps *= 2
    return inv


def _gdn_kernel(qkv_ref, gate_ref, small_ref, cw_ref, alog_ref, dtb_ref, gain_ref, o_ref,
                xbuf, state, sol_ref, qd_ref, kd_ref, aqk_ref, ubuf, *, a_lane, b_lane, rows):
    C = GDN_CHUNK
    W = GROUP_WIDTH
    D = GDN_HEAD_DIM
    n_chunks = rows // C

    @pl.when(pl.program_id(1) == 0)
    def _():
        xbuf[0:SUBLANES, :] = jnp.zeros((SUBLANES, 3 * W), F32)
        state[...] = jnp.zeros_like(state)

    ubuf[...] = jnp.zeros_like(ubuf)

    act = _silu(_causal_conv(xbuf, qkv_ref[...], cw_ref, SHORT_CONV, rows))
    small = small_ref[...]
    g_all = -jnp.exp(alog_ref[...]) * _softplus(small + dtb_ref[...])
    beta_all = _sigmoid(small)
    r_id = lax.broadcasted_iota(jnp.int32, (rows, rows), 0)
    c_id = lax.broadcasted_iota(jnp.int32, (rows, rows), 1)
    chunk_start = r_id - (r_id & (C - 1))
    tri_bd = (c_id >= chunk_start) & (c_id <= r_id)
    diag = r_id == c_id
    G_all = _dot_hi(tri_bd.astype(F32), g_all)
    G_rows = _rows_as_lanes(G_all, a_lane, SUBLANES)
    G_tot = jnp.concatenate(
        [jnp.broadcast_to(G_all[(c + 1) * C - 1:(c + 1) * C, :], (C, SMALL_W))
         for c in range(n_chunks)], axis=0)
    scale = D ** -0.5
    for h in range(GDN_HEADS):
        q = act[:, h * D:(h + 1) * D]
        k = act[:, W + h * D:W + (h + 1) * D]
        v = act[:, 2 * W + h * D:2 * W + (h + 1) * D]
        q = q * lax.rsqrt(jnp.sum(q * q, axis=-1, keepdims=True) + NORM_EPS) * scale
        k = k * lax.rsqrt(jnp.sum(k * k, axis=-1, keepdims=True) + NORM_EPS)
        Gc = G_all[:, a_lane + h:a_lane + h + 1]
        Gr = G_rows[h:h + 1, :]
        Gt = G_tot[:, a_lane + h:a_lane + h + 1]
        beta = beta_all[:, b_lane + h:b_lane + h + 1]
        gamma = jnp.exp(jnp.where(tri_bd, Gc - Gr, NEG_INF))
        kb = k * beta
        kb16, k16 = kb.astype(BF16), k.astype(BF16)
        m = jnp.where(diag, 0.0, _dot_nt(kb16, k16) * gamma)
        eG = jnp.exp(Gc)
        rhs = jnp.concatenate([v * beta, kb * eG], axis=1).astype(BF16)
        sol_ref[h] = _dot(_unit_lower_inverse(m, rows, C).astype(BF16), rhs)
        aqk_ref[h] = (_dot_nt(q.astype(BF16), k16) * gamma).astype(BF16)
        qd_ref[h] = (q * eG).astype(BF16)
        kd_ref[h] = (k * jnp.exp(Gt - Gc)).astype(BF16)

    for c in range(n_chunks):
        rs = slice(c * C, (c + 1) * C)
        for h in range(GDN_HEADS):
            sl = slice(h * D, (h + 1) * D)
            St = state[h]
            St16 = St.astype(BF16)
            u = sol_ref[h, rs, 0:D] - _dot(sol_ref[h, rs, D:2 * D].astype(BF16), St16)
            u16 = u.astype(BF16)
            ubuf[h, rs, :] = u16
            o = _dot(qd_ref[h, rs, :], St16) + _dot(aqk_ref[h, rs, :], ubuf[h])
            decay = jnp.exp(G_tot[c * C:c * C + 1, a_lane + h:a_lane + h + 1])
            state[h] = decay * St + _dot_tn(kd_ref[h, rs, :], u16)
            o = _rms(o, gain_ref[...]) * _silu(gate_ref[rs, sl])
            o_ref[rs, sl] = o.astype(BF16)


def _gdn(rest, B, S, conv_w, alog_row, dtb_row, gain_row, a_lane, b_lane, rows=256):
    T = B * S
    W = GROUP_WIDTH
    H, D = GDN_HEADS, GDN_HEAD_DIM
    nt = S // rows
    row = lambda b, i: b * nt + i
    return pl.pallas_call(
        functools.partial(_gdn_kernel, a_lane=a_lane, b_lane=b_lane, rows=rows),
        grid=(B, nt),
        in_specs=[pl.BlockSpec((rows, 3 * W), lambda b, i: (row(b, i), 0)),
                  pl.BlockSpec((rows, W), lambda b, i: (row(b, i), 3)),
                  pl.BlockSpec((rows, SMALL_W), lambda b, i: (row(b, i), 4 * W // SMALL_W)),
                  _resident((SHORT_CONV, 3 * W)),
                  _resident((1, SMALL_W)),
                  _resident((1, SMALL_W)),
                  _resident((1, D))],
        out_specs=pl.BlockSpec((rows, W), lambda b, i: (row(b, i), 0)),
        out_shape=jax.ShapeDtypeStruct((T, W), BF16),
        scratch_shapes=[pltpu.VMEM((rows + SUBLANES, 3 * W), F32),
                        pltpu.VMEM((H, D, D), F32),
                        pltpu.VMEM((H, rows, 2 * D), F32),
                        pltpu.VMEM((H, rows, D), BF16),
                        pltpu.VMEM((H, rows, D), BF16),
                        pltpu.VMEM((H, rows, rows), BF16),
                        pltpu.VMEM((H, rows, D), BF16)],
        compiler_params=_params(2),
        name="gdn",
    )(rest, rest, rest, conv_w, alog_row, dtb_row, gain_row)


def _hgrn_kernel(q_ref, f_ref, i_ref, g_ref, lbl_ref, gain_ref, o_ref, state):
    C = HGRN_CHUNK
    D = HGRN_HEAD_DIM
    SB = HGRN_SUB

    @pl.when(pl.program_id(1) == 0)
    def _():
        state[...] = jnp.zeros_like(state)

    logits = lbl_ref[...]
    e = jnp.exp(logits - jnp.max(logits, axis=0, keepdims=True))
    prob = e / jnp.sum(e, axis=0, keepdims=True)
    lb = (prob[0:1, :] + prob[1:2, :]) - prob[0:1, :]

    zf = f_ref[...]
    f = lb + (1.0 - lb) * _sigmoid(zf)
    kk = (1.0 - lb) * _sigmoid(-zf)
    G_all = _dot_hi(_tril(C).astype(F32), jnp.log(f))
    q_all = _silu(q_ref[...])
    col_id = lax.broadcasted_iota(jnp.int32, (SB, C), 1)
    row_id = lax.broadcasted_iota(jnp.int32, (SB, 1), 0)
    for h in range(HGRN_HEADS):
        sl = slice(h * D, (h + 1) * D)
        q, k, v, G = q_all[:, sl], kk[:, sl], i_ref[:, sl], G_all[:, sl]
        St = state[h]
        v16 = v.astype(BF16)
        blocks = []
        for b in range(C // SB):
            r0 = b * SB
            qb, Gb = q[r0:r0 + SB, :], G[r0:r0 + SB, :]
            G0 = G[r0:r0 + 1, :]
            if b:
                qd = qb * jnp.exp(Gb - G0)
                kd = k * jnp.exp(jnp.minimum(G0 - G, 0.0))
                blk = jnp.where(col_id < r0, _dot_nt(qd.astype(BF16), kd.astype(BF16)), 0.0)
            else:
                blk = jnp.zeros((SB, C), F32)
            for s in range(SB):
                dec = jnp.exp(jnp.minimum(Gb - G[r0 + s:r0 + s + 1, :], 0.0))
                cs = jnp.sum(qb * (k[r0 + s:r0 + s + 1, :] * dec), axis=1, keepdims=True)
                blk = jnp.where((col_id == r0 + s) & (row_id >= s), cs, blk)
            blocks.append(blk)
        a = jnp.concatenate(blocks, axis=0)
        o = _dot_nt((q * jnp.exp(G)).astype(BF16), St.astype(BF16)) + _dot(a.astype(BF16), v16)
        G_last = G[C - 1:C, :]
        state[h] = jnp.exp(G_last) * St + _dot_tn(v16, (k * jnp.exp(G_last - G)).astype(BF16))
        o = _rms(o, gain_ref[...]) * _silu(g_ref[:, sl])
        o_ref[:, sl] = o.astype(BF16)


def _hgrn(proj, B, S, lb_logits, gain_row):
    T = B * S
    C = HGRN_CHUNK
    W = GROUP_WIDTH
    nt = S // C
    spec = lambda j: pl.BlockSpec((C, W), lambda b, i: (b * nt + i, j))
    return pl.pallas_call(
        _hgrn_kernel,
        grid=(B, nt),
        in_specs=[spec(0), spec(1), spec(2), spec(3),
                  _resident(lb_logits.shape),
                  _resident((1, HGRN_HEAD_DIM))],
        out_specs=spec(0),
        out_shape=jax.ShapeDtypeStruct((T, W), BF16),
        scratch_shapes=[pltpu.VMEM((HGRN_HEADS, HGRN_HEAD_DIM, HGRN_HEAD_DIM), F32)],
        compiler_params=_params(2),
        name="hgrn2",
    )(proj, proj, proj, proj, lb_logits, gain_row)


def _ssd_kernel(xbc_ref, z_ref, small_ref, cw_ref, cb_ref, alog_ref, dtb_ref, dvec_ref, gain_ref,
                o_ref, xbuf, state):
    L = M2_CHUNK
    W = GROUP_WIDTH
    N = M2_STATE
    CW = W + 2 * M2_GROUPS * N
    pairs_per_group = M2_HEADS // M2_GROUPS // 2

    @pl.when(pl.program_id(1) == 0)
    def _():
        xbuf[0:SUBLANES, :] = jnp.zeros((SUBLANES, CW), F32)
        state[...] = jnp.zeros_like(state)

    xbc = _silu(_causal_conv(xbuf, xbc_ref[...], cw_ref, SHORT_CONV, L) + cb_ref[...])
    dt = _softplus(small_ref[...] + dtb_ref[...])
    tri = _tril(L)
    A_cs = _dot_hi(tri.astype(F32), dt * (-jnp.exp(alog_ref[...])))
    A_rows = _rows_as_lanes(A_cs, 0, M2_HEADS)
    lo = _lane_lo((1, LANES))
    for g in range(M2_GROUPS):
        Bg = xbc[:, W + g * N:W + (g + 1) * N]
        Cg = xbc[:, W + M2_GROUPS * N + g * N:W + M2_GROUPS * N + (g + 1) * N]
        cb = _dot_nt(Cg.astype(BF16), Bg.astype(BF16))
        ys = []
        for pp in range(pairs_per_group):
            p = g * pairs_per_group + pp
            xs = xbc[:, p * LANES:(p + 1) * LANES]
            dtl = jnp.where(lo, dt[:, 2 * p:2 * p + 1], dt[:, 2 * p + 1:2 * p + 2])
            X16 = (xs * dtl).astype(BF16)
            St = state[p]
            St16 = St.astype(BF16)
            y_h, st_h = [], []
            for e in range(2):
                h = 2 * p + e
                ac = A_cs[:, h:h + 1]
                ar = A_rows[h:h + 1, :]
                decay = jnp.exp(jnp.where(tri, ac - ar, NEG_INF))
                y = _dot((cb * decay).astype(BF16), X16) + _dot((Cg * jnp.exp(ac)).astype(BF16), St16)
                last = ac[L - 1:L, :]
                st = jnp.exp(last) * St + _dot_tn((Bg * jnp.exp(last - ac)).astype(BF16), X16)
                y_h.append(y)
                st_h.append(st)
            state[p] = jnp.where(lo, st_h[0], st_h[1])
            ys.append(jnp.where(lo, y_h[0], y_h[1]) + dvec_ref[:, p * LANES:(p + 1) * LANES] * xs)
        gs = slice(g * (W // M2_GROUPS), (g + 1) * (W // M2_GROUPS))
        y = jnp.concatenate(ys, axis=1) * _silu(z_ref[:, gs])
        o_ref[:, gs] = _rms(y, gain_ref[:, gs]).astype(BF16)


def _ssd(proj, B, S, conv_w, conv_b, alog_row, dtb_row, dvec, gain_row):
    T = B * S
    L = M2_CHUNK
    W = GROUP_WIDTH
    CW = W + 2 * M2_GROUPS * M2_STATE
    nt = S // L
    row = lambda b, i: b * nt + i
    return pl.pallas_call(
        _ssd_kernel,
        grid=(B, nt),
        in_specs=[pl.BlockSpec((L, CW), lambda b, i: (row(b, i), 4 * W // CW)),
                  pl.BlockSpec((L, W), lambda b, i: (row(b, i), (4 * W + CW) // W)),
                  pl.BlockSpec((L, SMALL_W), lambda b, i: (row(b, i), (5 * W + CW) // SMALL_W)),
                  _resident((SHORT_CONV, CW)),
                  _resident((1, CW)),
                  _resident((1, SMALL_W)),
                  _resident((1, SMALL_W)),
                  _resident((1, W)),
                  _resident((1, W))],
        out_specs=pl.BlockSpec((L, W), lambda b, i: (row(b, i), 0)),
        out_shape=jax.ShapeDtypeStruct((T, W), BF16),
        scratch_shapes=[pltpu.VMEM((L + SUBLANES, CW), F32),
                        pltpu.VMEM((M2_HEADS // 2, M2_STATE, LANES), F32)],
        compiler_params=_params(2),
        name="ssd",
    )(proj, proj, proj, conv_w, conv_b, alog_row, dtb_row, dvec, gain_row)


def _pad_lanes(v, first, width=SMALL_W):
    v = v.astype(F32)
    return jnp.pad(v, (first, width - first - v.shape[0])).reshape(1, width)


def kernel(x, norm_gains, w_out, ffn_w_up, ffn_conv_w, ffn_conv_b, ffn_w_down,
           even_w_in, fox_f_bias, gdn_conv_w, gdn_A_log, gdn_dt_bias, gdn_norm_gain,
           odd_w_in, hgrn_lb_logits, hgrn_norm_gain, m2_conv_w, m2_conv_b,
           m2_A_log, m2_dt_bias, m2_D, m2_norm_gain):
    B, S, D = x.shape
    assert D == D_MODEL and S % 512 == 0
    T = B * S
    W = GROUP_WIDTH
    tm = 512
    row = lambda v: v.astype(F32).reshape(1, -1)
    x2 = x.reshape(T, D).astype(F32)

    o_ff = 3 * W
    o_qkv = o_ff + FOX_HEADS
    o_a = o_qkv + 3 * W
    o_b = o_a + GDN_HEADS
    o_gate = o_b + GDN_HEADS
    a_lane, b_lane = FOX_HEADS, FOX_HEADS + GDN_HEADS
    pad = jnp.zeros((D, SMALL_W - FOX_HEADS - 2 * GDN_HEADS), even_w_in.dtype)
    w_even = jnp.concatenate([even_w_in[:, :o_ff], even_w_in[:, o_qkv:o_a], even_w_in[:, o_gate:],
                              even_w_in[:, o_ff:o_qkv], even_w_in[:, o_a:o_gate], pad],
                             axis=1).astype(BF16)
    g = norm_gains[0]
    qkv, rest = _inproj(x2, row(g[0]), w_even, 3 * W, 4 * W + SMALL_W, tm)
    fox_blk = 256
    qt, ka, vt = _foxgate(qkv, rest, B, S, 4 * W // SMALL_W, _pad_lanes(fox_f_bias, 0), fox_blk)
    o_fox = _fox(qt, ka, vt, B, S, fox_blk)
    o_gdn = _gdn(rest, B, S, gdn_conv_w.astype(F32), _pad_lanes(gdn_A_log, a_lane),
                 _pad_lanes(gdn_dt_bias, a_lane), row(gdn_norm_gain), a_lane, b_lane)
    mixed = jnp.concatenate([o_fox, o_gdn], axis=1)
    x2 = _outproj(mixed, w_out[0].astype(BF16), x2, row(g[1]), tm)
    x2 = _ffn(x2, B, S, row(g[2]), ffn_w_up[0].astype(BF16), ffn_conv_w[0].astype(F32),
              row(ffn_conv_b[0]), ffn_w_down[0].astype(BF16), row(g[3]), tm)

    assert hgrn_lb_logits.shape == (2, W)
    CW = W + 2 * M2_GROUPS * M2_STATE
    pad = jnp.zeros((D, SMALL_W - M2_HEADS), odd_w_in.dtype)
    w_odd = jnp.concatenate([odd_w_in[:, :4 * W], odd_w_in[:, 5 * W:5 * W + CW],
                             odd_w_in[:, 4 * W:5 * W], odd_w_in[:, 5 * W + CW:], pad],
                            axis=1).astype(BF16)
    g = norm_gains[1]
    (proj,) = _inproj(x2, row(g[0]), w_odd, 0, w_odd.shape[1], tm)
    o_hgrn = _hgrn(proj, B, S, hgrn_lb_logits.astype(F32), row(hgrn_norm_gain))
    o_ssd = _ssd(proj, B, S, m2_conv_w.astype(F32), row(m2_conv_b), _pad_lanes(m2_A_log, 0),
                 _pad_lanes(m2_dt_bias, 0), row(jnp.repeat(m2_D, M2_HEAD_DIM)), row(m2_norm_gain))
    mixed = jnp.concatenate([o_hgrn, o_ssd], axis=1)
    x2 = _outproj(mixed, w_out[1].astype(BF16), x2, row(g[1]), tm)
    x2 = _ffn(x2, B, S, row(g[2]), ffn_w_up[1].astype(BF16), ffn_conv_w[1].astype(F32),
              row(ffn_conv_b[1]), ffn_w_down[1].astype(BF16), row(g[3]), tm)
    return x2.reshape(B, S, D).astype(x.dtype)
```

```python
import functools

import jax
import jax.numpy as jnp
from jax import lax
from jax.experimental import pallas as pl
from jax.experimental.pallas import tpu as pltpu

F32 = jnp.float32
BF16 = jnp.bfloat16

D_MODEL = 1024
GROUP_WIDTH = D_MODEL // 2
FOX_HEAD_DIM = 64
FOX_HEADS = GROUP_WIDTH // FOX_HEAD_DIM
FOX_AUG = 3
GDN_HEAD_DIM = 128
GDN_HEADS = GROUP_WIDTH // GDN_HEAD_DIM
GDN_CHUNK = 64
SHORT_CONV = 4
HGRN_HEAD_DIM = 128
HGRN_HEADS = GROUP_WIDTH // HGRN_HEAD_DIM
HGRN_CHUNK = 64
M2_HEAD_DIM = 64
M2_HEADS = GROUP_WIDTH // M2_HEAD_DIM
M2_GROUPS = 2
M2_STATE = 128
M2_CHUNK = 128
D_FF = 2816
FFN_CONV = 3
NORM_EPS = 1e-6

LANES = 128
SUBLANES = 8
SMALL_W = LANES
VMEM_LIMIT = 56 * 1024 * 1024

NEG_INF = float("-inf")


def _dot(a, b):
    return jnp.dot(a, b, preferred_element_type=F32)


def _dot_nt(a, b):
    return lax.dot_general(a, b, (((1,), (1,)), ((), ())), preferred_element_type=F32)


def _dot_tn(a, b):
    return lax.dot_general(a, b, (((0,), (0,)), ((), ())), preferred_element_type=F32)


def _bf16_pieces(x):
    hi = x.astype(BF16)
    rest = x - hi.astype(F32)
    mid = rest.astype(BF16)
    return hi, mid, (rest - mid.astype(F32)).astype(BF16)


def _exact_dot(sel16, x):
    hi, mid, lo = _bf16_pieces(x)
    return _dot(sel16, hi) + _dot(sel16, mid) + _dot(sel16, lo)


def _rms(x, gain):
    return x * lax.rsqrt(jnp.mean(x * x, axis=-1, keepdims=True) + NORM_EPS) * gain


def _sigmoid(x):
    return 1.0 / (1.0 + jnp.exp(-x))


def _silu(x):
    h = 0.5 * x
    return h * jnp.tanh(h) + h


def _softplus(x):
    return jnp.maximum(x, 0.0) + jnp.log1p(jnp.exp(-jnp.abs(x)))


def _log_sigmoid(x):
    return jnp.minimum(x, 0.0) - jnp.log1p(jnp.exp(-jnp.abs(x)))


def _tril(n, strict=False):
    r = lax.broadcasted_iota(jnp.int32, (n, n), 0)
    c = lax.broadcasted_iota(jnp.int32, (n, n), 1)
    return (r > c) if strict else (r >= c)


def _lane_lo(shape):
    return lax.broadcasted_iota(jnp.int32, shape, len(shape) - 1) < (LANES // 2)


def _rows_as_lanes(cols, first, n):
    sel = (lax.broadcasted_iota(jnp.int32, (n, LANES), 1)
           == lax.broadcasted_iota(jnp.int32, (n, LANES), 0) + first).astype(BF16)
    hi, mid, lo = _bf16_pieces(cols)
    return _dot_nt(sel, hi) + _dot_nt(sel, mid) + _dot_nt(sel, lo)


def _causal_conv(xbuf, x, w_ref, width, rows):
    xbuf[SUBLANES:SUBLANES + rows, :] = x
    y = None
    for k in range(width):
        start = SUBLANES - (width - 1) + k
        term = xbuf[start:start + rows, :] * w_ref[k:k + 1, :]
        y = term if y is None else y + term
    xbuf[0:SUBLANES, :] = xbuf[rows:rows + SUBLANES, :]
    return y


def _params(n_grid):
    return pltpu.CompilerParams(dimension_semantics=("arbitrary",) * n_grid,
                                vmem_limit_bytes=VMEM_LIMIT)


def _resident(shape):
    nd = len(shape)
    return pl.BlockSpec(shape, lambda *_: (0,) * nd)


def _inproj_kernel(x_ref, g_ref, w_ref, *out_refs, n_a, n_b, cn):
    hn = _rms(x_ref[...], g_ref[...]).astype(BF16)
    outs = list(out_refs)
    if n_a:
        oa = outs.pop(0)
        for c0 in range(0, n_a, cn):
            c1 = min(c0 + cn, n_a)
            oa[:, c0:c1] = _dot(hn, w_ref[:, c0:c1]).astype(BF16)
    ob = outs.pop(0)
    for c0 in range(0, n_b, cn):
        c1 = min(c0 + cn, n_b)
        ob[:, c0:c1] = _dot(hn, w_ref[:, n_a + c0:n_a + c1])


def _inproj(x2, gain, w, n_a, n_b, tm):
    T = x2.shape[0]
    out_shape, out_specs = [], []
    if n_a:
        out_shape.append(jax.ShapeDtypeStruct((T, n_a), BF16))
        out_specs.append(pl.BlockSpec((tm, n_a), lambda i: (i, 0)))
    out_shape.append(jax.ShapeDtypeStruct((T, n_b), F32))
    out_specs.append(pl.BlockSpec((tm, n_b), lambda i: (i, 0)))
    return pl.pallas_call(
        functools.partial(_inproj_kernel, n_a=n_a, n_b=n_b, cn=512),
        grid=(T // tm,),
        in_specs=[pl.BlockSpec((tm, D_MODEL), lambda i: (i, 0)),
                  _resident((1, D_MODEL)),
                  _resident((D_MODEL, n_a + n_b))],
        out_specs=out_specs,
        out_shape=out_shape,
        compiler_params=_params(1),
        name="inproj",
    )(x2, gain, w)


def _mix_ffn_kernel(ma_ref, mb_ref, wo_ref, x_ref, gmix_ref, gpre_ref, wup_ref, cw_ref, cb_ref,
                    wdn_ref, gpost_ref, o_ref, ubuf, carry, act, *, tm, fc):
    W = GROUP_WIDTH

    @pl.when(pl.program_id(1) == 0)
    def _():
        carry[...] = jnp.zeros_like(carry)

    mix = _dot(ma_ref[...], wo_ref[0:W, :]) + _dot(mb_ref[...], wo_ref[W:2 * W, :])
    x = x_ref[...] + _rms(mix, gmix_ref[...])
    hn = _rms(x, gpre_ref[...]).astype(BF16)
    for c in range(D_FF // fc):
        halves = []
        for half in range(2):
            col = half * D_FF + c * fc
            u = _dot(hn, wup_ref[:, col:col + fc])
            ubuf[half, 0:SUBLANES, :] = carry[:, col:col + fc]
            ubuf[half, SUBLANES:SUBLANES + tm, :] = u
            carry[:, col:col + fc] = u[tm - SUBLANES:tm, :]
            y = cb_ref[:, col:col + fc]
            for k in range(FFN_CONV):
                start = SUBLANES - (FFN_CONV - 1) + k
                y = y + ubuf[half, start:start + tm, :] * cw_ref[k:k + 1, col:col + fc]
            halves.append(y)
        act[:, c * fc:(c + 1) * fc] = (_silu(halves[0]) * halves[1]).astype(BF16)
    y = _dot(act[...], wdn_ref[...])
    o_ref[...] = x + _rms(y, gpost_ref[...])


def _mix_ffn(mix_a, mix_b, w_out, x2, B, S, gmix, gpre, w_up, conv_w, conv_b, w_down, gpost,
             tm, fc=256):
    W = GROUP_WIDTH
    nt = S // tm
    row = lambda b, i: (b * nt + i, 0)
    return pl.pallas_call(
        functools.partial(_mix_ffn_kernel, tm=tm, fc=fc),
        grid=(B, nt),
        in_specs=[pl.BlockSpec((tm, W), row),
                  pl.BlockSpec((tm, W), row),
                  _resident((D_MODEL, D_MODEL)),
                  pl.BlockSpec((tm, D_MODEL), row),
                  _resident((1, D_MODEL)),
                  _resident((1, D_MODEL)),
                  _resident((D_MODEL, 2 * D_FF)),
                  _resident((FFN_CONV, 2 * D_FF)),
                  _resident((1, 2 * D_FF)),
                  _resident((D_FF, D_MODEL)),
                  _resident((1, D_MODEL))],
        out_specs=pl.BlockSpec((tm, D_MODEL), row),
        out_shape=jax.ShapeDtypeStruct(x2.shape, F32),
        scratch_shapes=[pltpu.VMEM((2, tm + SUBLANES, fc), F32),
                        pltpu.VMEM((SUBLANES, 2 * D_FF), F32),
                        pltpu.VMEM((tm, D_FF), BF16)],
        compiler_params=_params(2),
        name="mixffn",
    )(mix_a, mix_b, w_out, x2, gmix, gpre, w_up, conv_w, conv_b, w_down, gpost)


def _foxgate_kernel(s_ref, b_ref, qk_ref, qt_ref, ka_ref, vt_ref, carry, *, blk):
    W = GROUP_WIDTH

    @pl.when(pl.program_id(1) == 0)
    def _():
        carry[...] = jnp.zeros_like(carry)

    z = s_ref[...] + b_ref[...]
    cs = _exact_dot(_tril(blk).astype(BF16), _log_sigmoid(z)) + carry[...]
    carry[...] = cs[blk - 1:blk, :]
    lane = lax.broadcasted_iota(jnp.int32, (1, LANES), 1)
    lo = lane < FOX_HEAD_DIM
    ones = jnp.where(lane < FOX_HEAD_DIM + FOX_AUG, 1.0, 0.0)
    scale = FOX_HEAD_DIM ** -0.5
    for p in range(FOX_HEADS // 2):
        q = qk_ref[:, p * LANES:(p + 1) * LANES].astype(F32) * scale
        k = qk_ref[:, W + p * LANES:W + (p + 1) * LANES].astype(F32)
        for e in range(2):
            h = 2 * p + e
            qe = pltpu.roll(q, FOX_HEAD_DIM, 1) if e else q
            ke = pltpu.roll(k, FOX_HEAD_DIM, 1) if e else k
            rem = -cs[:, h:h + 1]
            aug = jnp.zeros((blk, LANES), F32)
            for piece in range(FOX_AUG):
                part = rem.astype(BF16).astype(F32)
                aug = jnp.where(lane == FOX_HEAD_DIM + piece, part, aug)
                rem = rem - part
            qt_ref[0, 0, h * LANES:(h + 1) * LANES, :] = jnp.where(lo, qe, ones).T.astype(BF16)
            ka_ref[:, h * LANES:(h + 1) * LANES] = jnp.where(lo, ke, aug).astype(BF16)
        v = qk_ref[:, 2 * W + p * LANES:2 * W + (p + 1) * LANES].astype(F32)
        vt_ref[0, 0, p * LANES:(p + 1) * LANES, :] = v.T.astype(BF16)


def _foxgate(qkv, rest, B, S, small_blk, bias_row, blk):
    T = B * S
    W = GROUP_WIDTH
    nb = S // blk
    wide = FOX_HEADS * LANES
    row = lambda b, j: b * nb + j
    return pl.pallas_call(
        functools.partial(_foxgate_kernel, blk=blk),
        grid=(B, nb),
        in_specs=[pl.BlockSpec((blk, SMALL_W), lambda b, j: (row(b, j), small_blk)),
                  _resident((1, SMALL_W)),
                  pl.BlockSpec((blk, 3 * W), lambda b, j: (row(b, j), 0))],
        out_specs=[pl.BlockSpec((1, 1, wide, blk), lambda b, j: (b, j, 0, 0)),
                   pl.BlockSpec((blk, wide), lambda b, j: (row(b, j), 0)),
                   pl.BlockSpec((1, 1, W, blk), lambda b, j: (b, j, 0, 0))],
        out_shape=[jax.ShapeDtypeStruct((B, nb, wide, blk), BF16),
                   jax.ShapeDtypeStruct((T, wide), BF16),
                   jax.ShapeDtypeStruct((B, nb, W, blk), BF16)],
        scratch_shapes=[pltpu.VMEM((1, SMALL_W), F32)],
        compiler_params=_params(2),
        name="foxgate",
    )(rest, bias_row, qkv)


def _fox_kernel(qt_ref, k_ref, vt_ref, o_ref, m_ref, l_ref, acc_ref, *, blk, kvb):
    i = pl.program_id(1)
    kv_id = lax.broadcasted_iota(jnp.int32, (kvb, blk), 0)
    q_id = lax.broadcasted_iota(jnp.int32, (kvb, blk), 1)
    first_head = lax.broadcasted_iota(jnp.int32, (LANES, 1), 0) < FOX_HEAD_DIM
    m_ref[...] = jnp.full(m_ref.shape, NEG_INF, F32)
    l_ref[...] = jnp.zeros_like(l_ref)
    acc_ref[...] = jnp.zeros_like(acc_ref)

    def sub_block(j, sub, masked):
        off = pl.multiple_of(j * blk + sub * kvb, kvb)
        for p in range(FOX_HEADS // 2):
            vt = vt_ref[0, j, p * LANES:(p + 1) * LANES, sub * kvb:(sub + 1) * kvb]
            alpha, pv = [], []
            for e in range(2):
                h = 2 * p + e
                hs = slice(h * LANES, (h + 1) * LANES)
                s = _dot(k_ref[pl.ds(off, kvb), hs], qt_ref[0, 0, hs, :])
                if masked:
                    s = jnp.where(kv_id + sub * kvb <= q_id, s, NEG_INF)
                m_old = m_ref[h:h + 1, :]
                m_new = jnp.maximum(m_old, jnp.max(s, axis=0, keepdims=True))
                pe = jnp.exp(s - m_new)
                a = jnp.exp(m_old - m_new)
                m_ref[h:h + 1, :] = m_new
                l_ref[h:h + 1, :] = a * l_ref[h:h + 1, :] + jnp.sum(pe, axis=0, keepdims=True)
                alpha.append(a)
                pv.append(_dot(vt, pe.astype(BF16)))
            acc_ref[p] = (jnp.where(first_head, alpha[0], alpha[1]) * acc_ref[p]
                          + jnp.where(first_head, pv[0], pv[1]))

    def block(j, masked):
        for sub in range(blk // kvb):
            sub_block(j, sub, masked)

    def body(j, carry):
        block(j, False)
        return carry

    lax.fori_loop(0, i, body, 0)
    block(i, True)
    for p in range(FOX_HEADS // 2):
        l = jnp.where(first_head, l_ref[2 * p:2 * p + 1, :], l_ref[2 * p + 1:2 * p + 2, :])
        o_ref[:, p * LANES:(p + 1) * LANES] = (acc_ref[p] / l).T.astype(BF16)


def _fox(qt, ka, vt, B, S, blk):
    T = B * S
    nq = S // blk
    W = GROUP_WIDTH
    wide = FOX_HEADS * LANES
    return pl.pallas_call(
        functools.partial(_fox_kernel, blk=blk, kvb=128),
        grid=(B, nq),
        in_specs=[pl.BlockSpec((1, 1, wide, blk), lambda b, i: (b, i, 0, 0)),
                  pl.BlockSpec((S, wide), lambda b, i: (b, 0)),
                  pl.BlockSpec((1, nq, W, blk), lambda b, i: (b, 0, 0, 0))],
        out_specs=pl.BlockSpec((blk, W), lambda b, i: (b * nq + i, 0)),
        out_shape=jax.ShapeDtypeStruct((T, W), BF16),
        scratch_shapes=[pltpu.VMEM((FOX_HEADS, blk), F32),
                        pltpu.VMEM((FOX_HEADS, blk), F32),
                        pltpu.VMEM((FOX_HEADS // 2, LANES, blk), F32)],
        compiler_params=_params(2),
        name="fox",
    )(qt, ka, vt)


def _unit_lower_inverses(ms, n, nilpotent):
    eye = (lax.broadcasted_iota(jnp.int32, (n, n), 0)
           == lax.broadcasted_iota(jnp.int32, (n, n), 1)).astype(F32)
    pws = [-m for m in ms]
    invs = [eye + pw for pw in pws]
    steps = 1
    while steps * 2 < nilpotent:
        pws = [_dot(pw.astype(BF16), pw.astype(BF16)) for pw in pws]
        invs = [inv + _dot(inv.astype(BF16), pw.astype(BF16)) for inv, pw in zip(invs, pws)]
        steps *= 2
    return invs


def _gdn_kernel(qkv_ref, gate_ref, small_ref, cw_ref, alog_ref, dtb_ref, gain_ref, o_ref,
                xbuf, state, sol_ref, qd_ref, kd_ref, aqk_ref, ubuf, *, a_lane, b_lane, rows):
    C = GDN_CHUNK
    P = 2 * C
    W = GROUP_WIDTH
    D = GDN_HEAD_DIM

    @pl.when(pl.program_id(1) == 0)
    def _():
        xbuf[0:SUBLANES, :] = jnp.zeros((SUBLANES, 3 * W), F32)
        state[...] = jnp.zeros_like(state)

    act = _silu(_causal_conv(xbuf, qkv_ref[...], cw_ref, SHORT_CONV, rows))
    small = small_ref[...]
    g_all = -jnp.exp(alog_ref[...]) * _softplus(small + dtb_ref[...])
    beta_all = _sigmoid(small)
    r_id = lax.broadcasted_iota(jnp.int32, (P, P), 0)
    c_id = lax.broadcasted_iota(jnp.int32, (P, P), 1)
    tri_bd = (c_id >= r_id - (r_id & (C - 1))) & (c_id <= r_id)
    diag = r_id == c_id
    tri16 = tri_bd.astype(BF16)
    scale = D ** -0.5
    g_tot = []
    for pr in range(rows // P):
        ps = slice(pr * P, (pr + 1) * P)
        G_all = _exact_dot(tri16, g_all[ps, :])
        G_rows = _rows_as_lanes(G_all, a_lane, SUBLANES)
        G_tot = jnp.concatenate(
            [jnp.broadcast_to(G_all[(c + 1) * C - 1:(c + 1) * C, :], (C, SMALL_W)) for c in range(2)],
            axis=0)
        g_tot.append(G_tot)
        ms, rhss = [], []
        for h in range(GDN_HEADS):
            q = act[ps, h * D:(h + 1) * D]
            k = act[ps, W + h * D:W + (h + 1) * D]
            v = act[ps, 2 * W + h * D:2 * W + (h + 1) * D]
            q = q * lax.rsqrt(jnp.sum(q * q, axis=-1, keepdims=True) + NORM_EPS) * scale
            k = k * lax.rsqrt(jnp.sum(k * k, axis=-1, keepdims=True) + NORM_EPS)
            Gc = G_all[:, a_lane + h:a_lane + h + 1]
            Gr = G_rows[h:h + 1, :]
            Gt = G_tot[:, a_lane + h:a_lane + h + 1]
            beta = beta_all[ps, b_lane + h:b_lane + h + 1]
            gamma = jnp.exp(jnp.where(tri_bd, Gc - Gr, NEG_INF))
            kb = k * beta
            kb16, k16 = kb.astype(BF16), k.astype(BF16)
            ms.append(jnp.where(diag, 0.0, _dot_nt(kb16, k16) * gamma))
            eG = jnp.exp(Gc)
            rhss.append(jnp.concatenate([v * beta, kb * eG], axis=1).astype(BF16))
            aqk_ref[h, ps, :] = (_dot_nt(q.astype(BF16), k16) * gamma).astype(BF16)
            qd_ref[h, ps, :] = (q * eG).astype(BF16)
            kd_ref[h, ps, :] = (k * jnp.exp(Gt - Gc)).astype(BF16)
        for h, (inv, rhs) in enumerate(zip(_unit_lower_inverses(ms, P, C), rhss)):
            sol_ref[h, ps, :] = _dot(inv.astype(BF16), rhs)

    for c in range(rows // C):
        rs = slice(c * C, (c + 1) * C)
        first = c % 2 == 0
        for h in range(GDN_HEADS):
            sl = slice(h * D, (h + 1) * D)
            St = state[h]
            St16 = St.astype(BF16)
            u = sol_ref[h, rs, 0:D] - _dot(sol_ref[h, rs, D:2 * D].astype(BF16), St16)
            u16 = u.astype(BF16)
            o = _dot(qd_ref[h, rs, :], St16)
            if first:
                ubuf[h] = jnp.concatenate([u16, jnp.zeros_like(u16)], axis=0)
                o = o + _dot(aqk_ref[h, rs, 0:C], u16)
            else:
                ubuf[h, C:P, :] = u16
                o = o + _dot(aqk_ref[h, rs, :], ubuf[h])
            decay = jnp.exp(g_tot[c // 2][(c % 2) * C:(c % 2) * C + 1, a_lane + h:a_lane + h + 1])
            state[h] = decay * St + _dot_tn(kd_ref[h, rs, :], u16)
            o = _rms(o, gain_ref[...]) * _silu(gate_ref[rs, sl])
            o_ref[rs, sl] = o.astype(BF16)


def _gdn(rest, B, S, conv_w, alog_row, dtb_row, gain_row, a_lane, b_lane, rows=256):
    T = B * S
    W = GROUP_WIDTH
    H, D = GDN_HEADS, GDN_HEAD_DIM
    P = 2 * GDN_CHUNK
    nt = S // rows
    row = lambda b, i: b * nt + i
    return pl.pallas_call(
        functools.partial(_gdn_kernel, a_lane=a_lane, b_lane=b_lane, rows=rows),
        grid=(B, nt),
        in_specs=[pl.BlockSpec((rows, 3 * W), lambda b, i: (row(b, i), 0)),
                  pl.BlockSpec((rows, W), lambda b, i: (row(b, i), 3)),
                  pl.BlockSpec((rows, SMALL_W), lambda b, i: (row(b, i), 4 * W // SMALL_W)),
                  _resident((SHORT_CONV, 3 * W)),
                  _resident((1, SMALL_W)),
                  _resident((1, SMALL_W)),
                  _resident((1, D))],
        out_specs=pl.BlockSpec((rows, W), lambda b, i: (row(b, i), 0)),
        out_shape=jax.ShapeDtypeStruct((T, W), BF16),
        scratch_shapes=[pltpu.VMEM((rows + SUBLANES, 3 * W), F32),
                        pltpu.VMEM((H, D, D), F32),
                        pltpu.VMEM((H, rows, 2 * D), F32),
                        pltpu.VMEM((H, rows, D), BF16),
                        pltpu.VMEM((H, rows, D), BF16),
                        pltpu.VMEM((H, rows, P), BF16),
                        pltpu.VMEM((H, P, D), BF16)],
        compiler_params=_params(2),
        name="gdn",
    )(rest, rest, rest, conv_w, alog_row, dtb_row, gain_row)


def _block_ref_rows(x, half):
    R, L = x.shape
    if half >= SUBLANES:
        xb = x.reshape(R // (2 * half), 2 * half, L)
        return jnp.broadcast_to(xb[:, half:half + 1, :], xb.shape).reshape(R, L)
    xb = x.reshape(R // SUBLANES, SUBLANES, L)
    sub = lax.broadcasted_iota(jnp.int32, (1, SUBLANES, 1), 1)
    out = None
    for start in range(SUBLANES - 2 * half, -1, -2 * half):
        row = jnp.broadcast_to(xb[:, start + half:start + half + 1, :], xb.shape)
        out = row if out is None else jnp.where(sub < start + 2 * half, row, out)
    return out.reshape(R, L)


def _hgrn_kernel(q_ref, f_ref, i_ref, g_ref, lbl_ref, gain_ref, o_ref, state, *, rows):
    C = HGRN_CHUNK
    P = 2 * C
    D = HGRN_HEAD_DIM
    halves = [1 << b for b in range(C.bit_length() - 1)]

    @pl.when(pl.program_id(1) == 0)
    def _():
        state[...] = jnp.zeros_like(state)

    logits = lbl_ref[...]
    e = jnp.exp(logits - jnp.max(logits, axis=0, keepdims=True))
    prob = e / jnp.sum(e, axis=0, keepdims=True)
    lb = (prob[0:1, :] + prob[1:2, :]) - prob[0:1, :]

    r_id = lax.broadcasted_iota(jnp.int32, (P, P), 0)
    c_id = lax.broadcasted_iota(jnp.int32, (P, P), 1)
    tri16 = ((c_id >= r_id - (r_id & (C - 1))) & (c_id <= r_id)).astype(BF16)
    differ = r_id ^ c_id
    row_id = lax.broadcasted_iota(jnp.int32, (P, 1), 0)
    gain = gain_ref[...]
    for pr in range(rows // P):
        ps = slice(pr * P, (pr + 1) * P)
        zf = f_ref[ps, :]
        gate = _sigmoid(zf)
        f = lb + (1.0 - lb) * gate
        k_all = (1.0 - lb) * (1.0 - gate)
        q_all = _silu(q_ref[ps, :])
        G_all = _exact_dot(tri16, jnp.log(f))
        decays = [jnp.exp(-jnp.abs(G_all - _block_ref_rows(G_all, half))) for half in halves]
        for h in range(HGRN_HEADS):
            sl = slice(h * D, (h + 1) * D)
            q, k, G = q_all[:, sl], k_all[:, sl], G_all[:, sl]
            q16, k16 = q.astype(BF16), k.astype(BF16)
            v16 = i_ref[ps, sl].astype(BF16)
            a = jnp.zeros((P, P), F32)
            for half, dec in reversed(list(zip(halves, decays))):
                upper = (row_id & half) != 0
                z16 = (jnp.where(upper, q, k) * dec[:, sl]).astype(BF16)
                a = jnp.where(differ < 2 * half, _dot_nt(z16, z16), a)
            a = jnp.where(differ == 0, _dot_nt(q16, k16), a)
            a = jnp.where(r_id >= c_id, a, 0.0)
            o_intra = _dot(a.astype(BF16), v16)
            q_in = (q * jnp.exp(G)).astype(BF16)
            for c in range(2):
                cs = slice(c * C, (c + 1) * C)
                G_last = G[(c + 1) * C - 1:(c + 1) * C, :]
                St = state[h]
                o = _dot_nt(q_in[cs, :], St.astype(BF16)) + o_intra[cs, :]
                k_out = (k[cs, :] * jnp.exp(G_last - G[cs, :])).astype(BF16)
                state[h] = jnp.exp(G_last) * St + _dot_tn(v16[cs, :], k_out)
                rs = slice(pr * P + c * C, pr * P + (c + 1) * C)
                o_ref[rs, sl] = (_rms(o, gain) * _silu(g_ref[rs, sl])).astype(BF16)


def _hgrn(proj, B, S, lb_logits, gain_row, rows=256):
    T = B * S
    W = GROUP_WIDTH
    nt = S // rows
    spec = lambda j: pl.BlockSpec((rows, W), lambda b, i: (b * nt + i, j))
    return pl.pallas_call(
        functools.partial(_hgrn_kernel, rows=rows),
        grid=(B, nt),
        in_specs=[spec(0), spec(1), spec(2), spec(3),
                  _resident(lb_logits.shape),
                  _resident((1, HGRN_HEAD_DIM))],
        out_specs=spec(0),
        out_shape=jax.ShapeDtypeStruct((T, W), BF16),
        scratch_shapes=[pltpu.VMEM((HGRN_HEADS, HGRN_HEAD_DIM, HGRN_HEAD_DIM), F32)],
        compiler_params=_params(2),
        name="hgrn2",
    )(proj, proj, proj, proj, lb_logits, gain_row)


def _ssd_kernel(xbc_ref, z_ref, small_ref, cw_ref, cb_ref, alog_ref, dtb_ref, dvec_ref, gain_ref,
                o_ref, xbuf, state, *, rows):
    L = M2_CHUNK
    W = GROUP_WIDTH
    N = M2_STATE
    CW = W + 2 * M2_GROUPS * N
    pairs_per_group = M2_HEADS // M2_GROUPS // 2

    @pl.when(pl.program_id(1) == 0)
    def _():
        xbuf[0:SUBLANES, :] = jnp.zeros((SUBLANES, CW), F32)
        state[...] = jnp.zeros_like(state)

    xbc_all = _silu(_causal_conv(xbuf, xbc_ref[...], cw_ref, SHORT_CONV, rows) + cb_ref[...])
    dt_all = _softplus(small_ref[...] + dtb_ref[...])
    tri = _tril(L)
    tri16 = tri.astype(BF16)
    neg_a = -jnp.exp(alog_ref[...])
    lo = _lane_lo((1, LANES))
    for ck in range(rows // L):
        rs = slice(ck * L, (ck + 1) * L)
        xbc = xbc_all[rs, :]
        dt = dt_all[rs, :]
        A_cs = _exact_dot(tri16, dt * neg_a)
        A_rows = _rows_as_lanes(A_cs, 0, M2_HEADS)
        for g in range(M2_GROUPS):
            Bg = xbc[:, W + g * N:W + (g + 1) * N]
            Cg = xbc[:, W + M2_GROUPS * N + g * N:W + M2_GROUPS * N + (g + 1) * N]
            cb = _dot_nt(Cg.astype(BF16), Bg.astype(BF16))
            ys = []
            for pp in range(pairs_per_group):
                p = g * pairs_per_group + pp
                xs = xbc[:, p * LANES:(p + 1) * LANES]
                dtl = jnp.where(lo, dt[:, 2 * p:2 * p + 1], dt[:, 2 * p + 1:2 * p + 2])
                X16 = (xs * dtl).astype(BF16)
                St = state[p]
                rhs = jnp.concatenate([X16, St.astype(BF16)], axis=0)
                y_h, st_h = [], []
                for e in range(2):
                    h = 2 * p + e
                    ac = A_cs[:, h:h + 1]
                    ar = A_rows[h:h + 1, :]
                    decay = jnp.exp(jnp.where(tri, ac - ar, NEG_INF))
                    lhs = jnp.concatenate([(cb * decay).astype(BF16),
                                           (Cg * jnp.exp(ac)).astype(BF16)], axis=1)
                    y_h.append(_dot(lhs, rhs))
                    last = ac[L - 1:L, :]
                    st_h.append(jnp.exp(last) * St
                                + _dot_tn((Bg * jnp.exp(last - ac)).astype(BF16), X16))
                state[p] = jnp.where(lo, st_h[0], st_h[1])
                ys.append(jnp.where(lo, y_h[0], y_h[1]) + dvec_ref[:, p * LANES:(p + 1) * LANES] * xs)
            gs = slice(g * (W // M2_GROUPS), (g + 1) * (W // M2_GROUPS))
            y = jnp.concatenate(ys, axis=1) * _silu(z_ref[rs, gs])
            o_ref[rs, gs] = _rms(y, gain_ref[:, gs]).astype(BF16)


def _ssd(proj, B, S, conv_w, conv_b, alog_row, dtb_row, dvec, gain_row, rows=256):
    T = B * S
    W = GROUP_WIDTH
    CW = W + 2 * M2_GROUPS * M2_STATE
    nt = S // rows
    row = lambda b, i: b * nt + i
    return pl.pallas_call(
        functools.partial(_ssd_kernel, rows=rows),
        grid=(B, nt),
        in_specs=[pl.BlockSpec((rows, CW), lambda b, i: (row(b, i), 4 * W // CW)),
                  pl.BlockSpec((rows, W), lambda b, i: (row(b, i), (4 * W + CW) // W)),
                  pl.BlockSpec((rows, SMALL_W), lambda b, i: (row(b, i), (5 * W + CW) // SMALL_W)),
                  _resident((SHORT_CONV, CW)),
                  _resident((1, CW)),
                  _resident((1, SMALL_W)),
                  _resident((1, SMALL_W)),
                  _resident((1, W)),
                  _resident((1, W))],
        out_specs=pl.BlockSpec((rows, W), lambda b, i: (row(b, i), 0)),
        out_shape=jax.ShapeDtypeStruct((T, W), BF16),
        scratch_shapes=[pltpu.VMEM((rows + SUBLANES, CW), F32),
                        pltpu.VMEM((M2_HEADS // 2, M2_STATE, LANES), F32)],
        compiler_params=_params(2),
        name="ssd",
    )(proj, proj, proj, conv_w, conv_b, alog_row, dtb_row, dvec, gain_row)


def _pad_lanes(v, first, width=SMALL_W):
    v = v.astype(F32)
    return jnp.pad(v, (first, width - first - v.shape[0])).reshape(1, width)


def kernel(x, norm_gains, w_out, ffn_w_up, ffn_conv_w, ffn_conv_b, ffn_w_down,
           even_w_in, fox_f_bias, gdn_conv_w, gdn_A_log, gdn_dt_bias, gdn_norm_gain,
           odd_w_in, hgrn_lb_logits, hgrn_norm_gain, m2_conv_w, m2_conv_b,
           m2_A_log, m2_dt_bias, m2_D, m2_norm_gain):
    B, S, D = x.shape
    assert D == D_MODEL and S % 512 == 0
    T = B * S
    W = GROUP_WIDTH
    tm = 512
    row = lambda v: v.astype(F32).reshape(1, -1)
    x2 = x.reshape(T, D).astype(F32)

    o_ff = 3 * W
    o_qkv = o_ff + FOX_HEADS
    o_a = o_qkv + 3 * W
    o_b = o_a + GDN_HEADS
    o_gate = o_b + GDN_HEADS
    a_lane, b_lane = FOX_HEADS, FOX_HEADS + GDN_HEADS
    pad = jnp.zeros((D, SMALL_W - FOX_HEADS - 2 * GDN_HEADS), even_w_in.dtype)
    w_even = jnp.concatenate([even_w_in[:, :o_ff], even_w_in[:, o_qkv:o_a], even_w_in[:, o_gate:],
                              even_w_in[:, o_ff:o_qkv], even_w_in[:, o_a:o_gate], pad],
                             axis=1).astype(BF16)
    g = norm_gains[0]
    qkv, rest = _inproj(x2, row(g[0]), w_even, 3 * W, 4 * W + SMALL_W, tm)
    fox_blk = 256
    qt, ka, vt = _foxgate(qkv, rest, B, S, 4 * W // SMALL_W, _pad_lanes(fox_f_bias, 0), fox_blk)
    o_fox = _fox(qt, ka, vt, B, S, fox_blk)
    o_gdn = _gdn(rest, B, S, gdn_conv_w.astype(F32), _pad_lanes(gdn_A_log, a_lane),
                 _pad_lanes(gdn_dt_bias, a_lane), row(gdn_norm_gain), a_lane, b_lane)
    x2 = _mix_ffn(o_fox, o_gdn, w_out[0].astype(BF16), x2, B, S, row(g[1]), row(g[2]),
                  ffn_w_up[0].astype(BF16), ffn_conv_w[0].astype(F32), row(ffn_conv_b[0]),
                  ffn_w_down[0].astype(BF16), row(g[3]), tm)

    assert hgrn_lb_logits.shape == (2, W)
    CW = W + 2 * M2_GROUPS * M2_STATE
    pad = jnp.zeros((D, SMALL_W - M2_HEADS), odd_w_in.dtype)
    w_odd = jnp.concatenate([odd_w_in[:, :4 * W], odd_w_in[:, 5 * W:5 * W + CW],
                             odd_w_in[:, 4 * W:5 * W], odd_w_in[:, 5 * W + CW:], pad],
                            axis=1).astype(BF16)
    g = norm_gains[1]
    (proj,) = _inproj(x2, row(g[0]), w_odd, 0, w_odd.shape[1], tm)
    o_hgrn = _hgrn(proj, B, S, hgrn_lb_logits.astype(F32), row(hgrn_norm_gain))
    o_ssd = _ssd(proj, B, S, m2_conv_w.astype(F32), row(m2_conv_b), _pad_lanes(m2_A_log, 0),
                 _pad_lanes(m2_dt_bias, 0), row(jnp.repeat(m2_D, M2_HEAD_DIM)), row(m2_norm_gain))
    x2 = _mix_ffn(o_hgrn, o_ssd, w_out[1].astype(BF16), x2, B, S, row(g[1]), row(g[2]),
                  ffn_w_up[1].astype(BF16), ffn_conv_w[1].astype(F32), row(ffn_conv_b[1]),
                  ffn_w_down[1].astype(BF16), row(g[3]), tm)
    return x2.reshape(B, S, D).astype(x.dtype)
```

```python
import functools

import jax
import jax.numpy as jnp
from jax import lax
from jax.experimental import pallas as pl
from jax.experimental.pallas import tpu as pltpu

F32 = jnp.float32
BF16 = jnp.bfloat16

D_MODEL = 1024
GROUP_WIDTH = D_MODEL // 2
FOX_HEAD_DIM = 64
FOX_HEADS = GROUP_WIDTH // FOX_HEAD_DIM
FOX_AUG = 3
GDN_HEAD_DIM = 128
GDN_HEADS = GROUP_WIDTH // GDN_HEAD_DIM
GDN_CHUNK = 64
SHORT_CONV = 4
HGRN_HEAD_DIM = 128
HGRN_HEADS = GROUP_WIDTH // HGRN_HEAD_DIM
HGRN_CHUNK = 64
M2_HEAD_DIM = 64
M2_HEADS = GROUP_WIDTH // M2_HEAD_DIM
M2_GROUPS = 2
M2_STATE = 128
M2_CHUNK = 128
D_FF = 2816
FFN_CONV = 3
NORM_EPS = 1e-6

LANES = 128
SUBLANES = 8
SMALL_W = LANES
VMEM_LIMIT = 56 * 1024 * 1024

NEG_INF = float("-inf")


def _dot(a, b):
    return jnp.dot(a, b, preferred_element_type=F32)


def _dot_nt(a, b):
    return lax.dot_general(a, b, (((1,), (1,)), ((), ())), preferred_element_type=F32)


def _dot_tn(a, b):
    return lax.dot_general(a, b, (((0,), (0,)), ((), ())), preferred_element_type=F32)


def _bf16_pieces(x):
    hi = x.astype(BF16)
    rest = x - hi.astype(F32)
    mid = rest.astype(BF16)
    return hi, mid, (rest - mid.astype(F32)).astype(BF16)


def _exact_dot(sel16, x):
    hi, mid, lo = _bf16_pieces(x)
    return _dot(sel16, hi) + _dot(sel16, mid) + _dot(sel16, lo)


def _rms(x, gain):
    return x * lax.rsqrt(jnp.mean(x * x, axis=-1, keepdims=True) + NORM_EPS) * gain


def _sigmoid(x):
    return 1.0 / (1.0 + jnp.exp(-x))


def _silu(x):
    h = 0.5 * x
    return h * jnp.tanh(h) + h


def _softplus(x):
    return jnp.maximum(x, 0.0) + jnp.log1p(jnp.exp(-jnp.abs(x)))


def _log_sigmoid(x):
    return jnp.minimum(x, 0.0) - jnp.log1p(jnp.exp(-jnp.abs(x)))


def _tril(n, strict=False):
    r = lax.broadcasted_iota(jnp.int32, (n, n), 0)
    c = lax.broadcasted_iota(jnp.int32, (n, n), 1)
    return (r > c) if strict else (r >= c)


def _lane_lo(shape):
    return lax.broadcasted_iota(jnp.int32, shape, len(shape) - 1) < (LANES // 2)


def _rows_as_lanes(cols, first, n):
    sel = (lax.broadcasted_iota(jnp.int32, (n, LANES), 1)
           == lax.broadcasted_iota(jnp.int32, (n, LANES), 0) + first).astype(BF16)
    hi, mid, lo = _bf16_pieces(cols)
    return _dot_nt(sel, hi) + _dot_nt(sel, mid) + _dot_nt(sel, lo)


def _params(n_grid):
    return pltpu.CompilerParams(dimension_semantics=("arbitrary",) * n_grid,
                                vmem_limit_bytes=VMEM_LIMIT)


def _resident(shape):
    nd = len(shape)
    return pl.BlockSpec(shape, lambda *_: (0,) * nd)


def _conv_tail_reset(tail, tiles_per_seq):
    @pl.when(pl.program_id(0) % tiles_per_seq == 0)
    def _():
        tail[...] = jnp.zeros_like(tail)


def _conv_chunk(p, tail, w_ref, cols, width):
    rows = p.shape[0]
    ext = jnp.concatenate([tail[:, cols], p], axis=0)
    tail[:, cols] = p[rows - SUBLANES:rows, :]
    y = None
    for k in range(width):
        start = SUBLANES - (width - 1) + k
        term = ext[start:start + rows, :] * w_ref[k:k + 1, cols]
        y = term if y is None else y + term
    return y


def _staggered(stages):
    pending = None
    for produce, consume in stages:
        value = produce()
        if pending is not None:
            pending[1](pending[0])
        pending = (value, consume)
    pending[1](pending[0])


def _inproj_even_kernel(x_ref, g_ref, w_ref, cw_ref, qkv_ref, act_ref, gate_ref, small_ref, tail,
                        *, tm, tiles_per_seq):
    W = GROUP_WIDTH
    D = GDN_HEAD_DIM
    _conv_tail_reset(tail, tiles_per_seq)
    hn = _rms(x_ref[...], g_ref[...]).astype(BF16)
    proj = lambda c0, n: (lambda: _dot(hn, w_ref[:, c0:c0 + n]))

    def fox_out(c):
        def consume(p):
            qkv_ref[:, c * W:(c + 1) * W] = p.astype(BF16)
        return consume

    def gdn_out(c):
        def consume(p):
            a = _silu(_conv_chunk(p, tail, cw_ref, slice(c * W, (c + 1) * W), SHORT_CONV))
            for h in range(GDN_HEADS):
                ah = a[:, h * D:(h + 1) * D]
                if c < 2:
                    ah = ah * lax.rsqrt(jnp.sum(ah * ah, axis=-1, keepdims=True) + NORM_EPS)
                if c == 0:
                    ah = ah * D ** -0.5
                act_ref[:, c * W + h * D:c * W + (h + 1) * D] = ah
        return consume

    def gate_out(p):
        gate_ref[...] = _silu(p)

    def small_out(p):
        small_ref[...] = p

    stages = []
    for c in range(3):
        stages += [(proj((3 + c) * W, W), gdn_out(c)), (proj(c * W, W), fox_out(c))]
    stages += [(proj(6 * W, W), gate_out), (proj(7 * W, SMALL_W), small_out)]
    _staggered(stages)


def _inproj_even(x2, gain, w, conv_w, tm, tiles_per_seq):
    T = x2.shape[0]
    W = GROUP_WIDTH
    widths = (3 * W, 3 * W, W, SMALL_W)
    dtypes = (BF16, F32, F32, F32)
    return pl.pallas_call(
        functools.partial(_inproj_even_kernel, tm=tm, tiles_per_seq=tiles_per_seq),
        grid=(T // tm,),
        in_specs=[pl.BlockSpec((tm, D_MODEL), lambda i: (i, 0)),
                  _resident((1, D_MODEL)),
                  _resident(w.shape),
                  _resident(conv_w.shape)],
        out_specs=[pl.BlockSpec((tm, n), lambda i: (i, 0)) for n in widths],
        out_shape=[jax.ShapeDtypeStruct((T, n), dt) for n, dt in zip(widths, dtypes)],
        scratch_shapes=[pltpu.VMEM((SUBLANES, 3 * W), F32)],
        compiler_params=_params(1),
        name="inproj_even",
    )(x2, gain, w, conv_w)


def _inproj_odd_kernel(x_ref, g_ref, w_ref, lbl_ref, cw_ref, cb_ref, o_ref, tail,
                       *, tm, tiles_per_seq):
    W = GROUP_WIDTH
    CW = W + 2 * M2_GROUPS * M2_STATE
    _conv_tail_reset(tail, tiles_per_seq)
    hn = _rms(x_ref[...], g_ref[...]).astype(BF16)
    proj = lambda c0, n: (lambda: _dot(hn, w_ref[:, c0:c0 + n]))

    logits = lbl_ref[...]
    e = jnp.exp(logits - jnp.max(logits, axis=0, keepdims=True))
    prob = e / jnp.sum(e, axis=0, keepdims=True)
    lb = (prob[0:1, :] + prob[1:2, :]) - prob[0:1, :]

    def conv_out(c):
        def consume(p):
            cols = slice(c * W, (c + 1) * W)
            o_ref[:, cols] = _silu(_conv_chunk(p, tail, cw_ref, cols, SHORT_CONV) + cb_ref[:, cols])
        return consume

    def mapped_out(col, fn):
        def consume(p):
            o_ref[:, col:col + p.shape[1]] = fn(p)
        return consume

    def forget_out(col):
        def consume(p):
            gate = _sigmoid(p)
            o_ref[:, col:col + W] = jnp.log(lb + (1.0 - lb) * gate)
            o_ref[:, col + W:col + 2 * W] = (1.0 - lb) * (1.0 - gate)
        return consume

    keep = lambda p: p
    stages = [(proj(c * W, W), conv_out(c)) for c in range(CW // W)]
    stages += [(proj(CW, W), mapped_out(CW, _silu)),
               (proj(CW + W, W), forget_out(CW + W)),
               (proj(CW + 2 * W, W), mapped_out(CW + 3 * W, keep)),
               (proj(CW + 3 * W, W), mapped_out(CW + 4 * W, _silu)),
               (proj(CW + 4 * W, W), mapped_out(CW + 5 * W, _silu)),
               (proj(CW + 5 * W, SMALL_W), mapped_out(CW + 6 * W, keep))]
    _staggered(stages)


def _inproj_odd(x2, gain, w, lb_logits, conv_w, conv_b, tm, tiles_per_seq):
    T = x2.shape[0]
    W = GROUP_WIDTH
    CW = W + 2 * M2_GROUPS * M2_STATE
    n_out = CW + 6 * W + SMALL_W
    return pl.pallas_call(
        functools.partial(_inproj_odd_kernel, tm=tm, tiles_per_seq=tiles_per_seq),
        grid=(T // tm,),
        in_specs=[pl.BlockSpec((tm, D_MODEL), lambda i: (i, 0)),
                  _resident((1, D_MODEL)),
                  _resident(w.shape),
                  _resident(lb_logits.shape),
                  _resident(conv_w.shape),
                  _resident(conv_b.shape)],
        out_specs=pl.BlockSpec((tm, n_out), lambda i: (i, 0)),
        out_shape=jax.ShapeDtypeStruct((T, n_out), F32),
        scratch_shapes=[pltpu.VMEM((SUBLANES, CW), F32)],
        compiler_params=_params(1),
        name="inproj_odd",
    )(x2, gain, w, lb_logits, conv_w, conv_b)


def _mix_ffn_kernel(ma_ref, mb_ref, wo_ref, x_ref, gmix_ref, gpre_ref, wup_ref, cw_ref, cb_ref,
                    wdn_ref, gpost_ref, o_ref, ubuf, carry, act, *, tm, fc):
    W = GROUP_WIDTH

    @pl.when(pl.program_id(1) == 0)
    def _():
        carry[...] = jnp.zeros_like(carry)

    mix = _dot(ma_ref[...], wo_ref[0:W, :]) + _dot(mb_ref[...], wo_ref[W:2 * W, :])
    x = x_ref[...] + _rms(mix, gmix_ref[...])
    hn = _rms(x, gpre_ref[...]).astype(BF16)
    for c in range(D_FF // fc):
        halves = []
        for half in range(2):
            col = half * D_FF + c * fc
            u = _dot(hn, wup_ref[:, col:col + fc])
            ubuf[half, 0:SUBLANES, :] = carry[:, col:col + fc]
            ubuf[half, SUBLANES:SUBLANES + tm, :] = u
            carry[:, col:col + fc] = u[tm - SUBLANES:tm, :]
            y = cb_ref[:, col:col + fc]
            for k in range(FFN_CONV):
                start = SUBLANES - (FFN_CONV - 1) + k
                y = y + ubuf[half, start:start + tm, :] * cw_ref[k:k + 1, col:col + fc]
            halves.append(y)
        act[:, c * fc:(c + 1) * fc] = (_silu(halves[0]) * halves[1]).astype(BF16)
    y = _dot(act[...], wdn_ref[...])
    o_ref[...] = x + _rms(y, gpost_ref[...])


def _mix_ffn(mix_a, mix_b, w_out, x2, B, S, gmix, gpre, w_up, conv_w, conv_b, w_down, gpost,
             tm, fc=256):
    W = GROUP_WIDTH
    nt = S // tm
    row = lambda b, i: (b * nt + i, 0)
    return pl.pallas_call(
        functools.partial(_mix_ffn_kernel, tm=tm, fc=fc),
        grid=(B, nt),
        in_specs=[pl.BlockSpec((tm, W), row),
                  pl.BlockSpec((tm, W), row),
                  _resident((D_MODEL, D_MODEL)),
                  pl.BlockSpec((tm, D_MODEL), row),
                  _resident((1, D_MODEL)),
                  _resident((1, D_MODEL)),
                  _resident((D_MODEL, 2 * D_FF)),
                  _resident((FFN_CONV, 2 * D_FF)),
                  _resident((1, 2 * D_FF)),
                  _resident((D_FF, D_MODEL)),
                  _resident((1, D_MODEL))],
        out_specs=pl.BlockSpec((tm, D_MODEL), row),
        out_shape=jax.ShapeDtypeStruct(x2.shape, F32),
        scratch_shapes=[pltpu.VMEM((2, tm + SUBLANES, fc), F32),
                        pltpu.VMEM((SUBLANES, 2 * D_FF), F32),
                        pltpu.VMEM((tm, D_FF), BF16)],
        compiler_params=_params(2),
        name="mixffn",
    )(mix_a, mix_b, w_out, x2, gmix, gpre, w_up, conv_w, conv_b, w_down, gpost)


def _foxgate_kernel(s_ref, b_ref, qk_ref, qt_ref, ka_ref, vt_ref, carry, *, blk):
    W = GROUP_WIDTH

    @pl.when(pl.program_id(1) == 0)
    def _():
        carry[...] = jnp.zeros_like(carry)

    z = s_ref[...] + b_ref[...]
    cs = _exact_dot(_tril(blk).astype(BF16), _log_sigmoid(z)) + carry[...]
    carry[...] = cs[blk - 1:blk, :]
    lane = lax.broadcasted_iota(jnp.int32, (1, LANES), 1)
    lo = lane < FOX_HEAD_DIM
    ones = jnp.where(lane < FOX_HEAD_DIM + FOX_AUG, 1.0, 0.0)
    scale = FOX_HEAD_DIM ** -0.5
    for p in range(FOX_HEADS // 2):
        q = qk_ref[:, p * LANES:(p + 1) * LANES].astype(F32) * scale
        k = qk_ref[:, W + p * LANES:W + (p + 1) * LANES].astype(F32)
        for e in range(2):
            h = 2 * p + e
            qe = pltpu.roll(q, FOX_HEAD_DIM, 1) if e else q
            ke = pltpu.roll(k, FOX_HEAD_DIM, 1) if e else k
            rem = -cs[:, h:h + 1]
            aug = jnp.zeros((blk, LANES), F32)
            for piece in range(FOX_AUG):
                part = rem.astype(BF16).astype(F32)
                aug = jnp.where(lane == FOX_HEAD_DIM + piece, part, aug)
                rem = rem - part
            qt_ref[0, 0, h * LANES:(h + 1) * LANES, :] = jnp.where(lo, qe, ones).T.astype(BF16)
            ka_ref[:, h * LANES:(h + 1) * LANES] = jnp.where(lo, ke, aug).astype(BF16)
        v = qk_ref[:, 2 * W + p * LANES:2 * W + (p + 1) * LANES].astype(F32)
        vt_ref[0, 0, p * LANES:(p + 1) * LANES, :] = v.T.astype(BF16)


def _foxgate(qkv, small, B, S, small_blk, bias_row, blk):
    T = B * S
    W = GROUP_WIDTH
    nb = S // blk
    wide = FOX_HEADS * LANES
    row = lambda b, j: b * nb + j
    return pl.pallas_call(
        functools.partial(_foxgate_kernel, blk=blk),
        grid=(B, nb),
        in_specs=[pl.BlockSpec((blk, SMALL_W), lambda b, j: (row(b, j), small_blk)),
                  _resident((1, SMALL_W)),
                  pl.BlockSpec((blk, 3 * W), lambda b, j: (row(b, j), 0))],
        out_specs=[pl.BlockSpec((1, 1, wide, blk), lambda b, j: (b, j, 0, 0)),
                   pl.BlockSpec((blk, wide), lambda b, j: (row(b, j), 0)),
                   pl.BlockSpec((1, 1, W, blk), lambda b, j: (b, j, 0, 0))],
        out_shape=[jax.ShapeDtypeStruct((B, nb, wide, blk), BF16),
                   jax.ShapeDtypeStruct((T, wide), BF16),
                   jax.ShapeDtypeStruct((B, nb, W, blk), BF16)],
        scratch_shapes=[pltpu.VMEM((1, SMALL_W), F32)],
        compiler_params=_params(2),
        name="foxgate",
    )(small, bias_row, qkv)


def _fox_kernel(qt_ref, k_ref, vt_ref, o_ref, m_ref, l_ref, acc_ref, *, blk, kvb):
    i = pl.program_id(1)
    kv_id = lax.broadcasted_iota(jnp.int32, (kvb, blk), 0)
    q_id = lax.broadcasted_iota(jnp.int32, (kvb, blk), 1)
    first_head = lax.broadcasted_iota(jnp.int32, (LANES, 1), 0) < FOX_HEAD_DIM
    m_ref[...] = jnp.full(m_ref.shape, NEG_INF, F32)
    l_ref[...] = jnp.zeros_like(l_ref)
    acc_ref[...] = jnp.zeros_like(acc_ref)

    def sub_block(j, sub, masked):
        off = pl.multiple_of(j * blk + sub * kvb, kvb)
        for p in range(FOX_HEADS // 2):
            vt = vt_ref[0, j, p * LANES:(p + 1) * LANES, sub * kvb:(sub + 1) * kvb]
            alpha, pv = [], []
            for e in range(2):
                h = 2 * p + e
                hs = slice(h * LANES, (h + 1) * LANES)
                s = _dot(k_ref[pl.ds(off, kvb), hs], qt_ref[0, 0, hs, :])
                if masked:
                    s = jnp.where(kv_id + sub * kvb <= q_id, s, NEG_INF)
                m_old = m_ref[h:h + 1, :]
                m_new = jnp.maximum(m_old, jnp.max(s, axis=0, keepdims=True))
                pe = jnp.exp(s - m_new)
                a = jnp.exp(m_old - m_new)
                m_ref[h:h + 1, :] = m_new
                l_ref[h:h + 1, :] = a * l_ref[h:h + 1, :] + jnp.sum(pe, axis=0, keepdims=True)
                alpha.append(a)
                pv.append(_dot(vt, pe.astype(BF16)))
            acc_ref[p] = (jnp.where(first_head, alpha[0], alpha[1]) * acc_ref[p]
                          + jnp.where(first_head, pv[0], pv[1]))

    def block(j, masked):
        for sub in range(blk // kvb):
            sub_block(j, sub, masked)

    def body(j, carry):
        block(j, False)
        return carry

    lax.fori_loop(0, i, body, 0)
    block(i, True)
    for p in range(FOX_HEADS // 2):
        l = jnp.where(first_head, l_ref[2 * p:2 * p + 1, :], l_ref[2 * p + 1:2 * p + 2, :])
        o_ref[:, p * LANES:(p + 1) * LANES] = (acc_ref[p] / l).T.astype(BF16)


def _fox(qt, ka, vt, B, S, blk):
    T = B * S
    nq = S // blk
    W = GROUP_WIDTH
    wide = FOX_HEADS * LANES
    return pl.pallas_call(
        functools.partial(_fox_kernel, blk=blk, kvb=128),
        grid=(B, nq),
        in_specs=[pl.BlockSpec((1, 1, wide, blk), lambda b, i: (b, i, 0, 0)),
                  pl.BlockSpec((S, wide), lambda b, i: (b, 0)),
                  pl.BlockSpec((1, nq, W, blk), lambda b, i: (b, 0, 0, 0))],
        out_specs=pl.BlockSpec((blk, W), lambda b, i: (b * nq + i, 0)),
        out_shape=jax.ShapeDtypeStruct((T, W), BF16),
        scratch_shapes=[pltpu.VMEM((FOX_HEADS, blk), F32),
                        pltpu.VMEM((FOX_HEADS, blk), F32),
                        pltpu.VMEM((FOX_HEADS // 2, LANES, blk), F32)],
        compiler_params=_params(2),
        name="fox",
    )(qt, ka, vt)


def _unit_lower_solves(ms, rhss, n, nilpotent):
    eye = (lax.broadcasted_iota(jnp.int32, (n, n), 0)
           == lax.broadcasted_iota(jnp.int32, (n, n), 1)).astype(F32)
    xs = [-m for m in ms]
    sols = rhss
    power = 1
    while power < nilpotent:
        x16 = [x.astype(BF16) for x in xs]
        x2 = [_dot(x, x) for x in x16]
        x2_16 = [x.astype(BF16) for x in x2]
        factors = [eye + x + y + _dot(xb, yb) for x, y, xb, yb in zip(xs, x2, x16, x2_16)]
        sols = [_dot(f.astype(BF16), r.astype(BF16)) for f, r in zip(factors, sols)]
        power *= 4
        if power < nilpotent:
            xs = [_dot(y, y) for y in x2_16]
    assert power == nilpotent
    return sols


def _gdn_kernel(act_ref, gate_ref, small_ref, alog_ref, dtb_ref, gain_ref, o_ref,
                state, sol_ref, qd_ref, kd_ref, aqk_ref, ks_ref, ubuf, *, a_lane, b_lane, rows):
    C = GDN_CHUNK
    P = 2 * C
    W = GROUP_WIDTH
    D = GDN_HEAD_DIM
    n_chunks = rows // C

    @pl.when(pl.program_id(1) == 0)
    def _():
        state[...] = jnp.zeros_like(state)

    small = small_ref[...]
    g_all = -jnp.exp(alog_ref[...]) * _softplus(small + dtb_ref[...])
    beta_all = _sigmoid(small)
    r_id = lax.broadcasted_iota(jnp.int32, (P, P), 0)
    c_id = lax.broadcasted_iota(jnp.int32, (P, P), 1)
    tri_bd = (c_id >= r_id - (r_id & (C - 1))) & (c_id <= r_id)
    diag = r_id == c_id
    tri16 = tri_bd.astype(BF16)
    g_tot, ms, rhss, where = [], [], [], []
    for pr in range(rows // P):
        ps = slice(pr * P, (pr + 1) * P)
        G_all = _exact_dot(tri16, g_all[ps, :])
        G_rows = _rows_as_lanes(G_all, a_lane, SUBLANES)
        G_tot = jnp.concatenate(
            [jnp.broadcast_to(G_all[(c + 1) * C - 1:(c + 1) * C, :], (C, SMALL_W)) for c in range(2)],
            axis=0)
        g_tot.append(G_tot)
        for h in range(GDN_HEADS):
            q = act_ref[ps, h * D:(h + 1) * D]
            k = act_ref[ps, W + h * D:W + (h + 1) * D]
            v = act_ref[ps, 2 * W + h * D:2 * W + (h + 1) * D]
            Gc = G_all[:, a_lane + h:a_lane + h + 1]
            Gr = G_rows[h:h + 1, :]
            Gt = G_tot[:, a_lane + h:a_lane + h + 1]
            beta = beta_all[ps, b_lane + h:b_lane + h + 1]
            gamma = jnp.exp(jnp.where(tri_bd, Gc - Gr, NEG_INF))
            kb = k * beta
            kb16, k16 = kb.astype(BF16), k.astype(BF16)
            ms.append(jnp.where(diag, 0.0, _dot_nt(kb16, k16) * gamma))
            eG = jnp.exp(Gc)
            rhss.append(jnp.concatenate([v * beta, kb * eG], axis=1))
            where.append((h, ps))
            aqk_ref[h, ps, :] = (_dot_nt(q.astype(BF16), k16) * gamma).astype(BF16)
            qd_ref[h, ps, :] = (q * eG).astype(BF16)
            kd_ref[h, ps, :] = (k * jnp.exp(Gt - Gc)).astype(BF16)
    for (h, ps), sol in zip(where, _unit_lower_solves(ms, rhss, P, C)):
        sol_ref[h, ps, :] = sol
    for c in range(n_chunks):
        rs = slice(c * C, (c + 1) * C)
        for h in range(GDN_HEADS):
            ks_ref[h, c] = _dot_tn(kd_ref[h, rs, :], sol_ref[h, rs, :].astype(BF16))

    states = [state[h] for h in range(GDN_HEADS)]
    for c in range(n_chunks):
        rs = slice(c * C, (c + 1) * C)
        first = c % 2 == 0
        for h in range(GDN_HEADS):
            sl = slice(h * D, (h + 1) * D)
            St = states[h]
            St16 = St.astype(BF16)
            decay = jnp.exp(g_tot[c // 2][(c % 2) * C:(c % 2) * C + 1, a_lane + h:a_lane + h + 1])
            states[h] = (decay * St + ks_ref[h, c, :, 0:D]
                         - _dot(ks_ref[h, c, :, D:2 * D].astype(BF16), St16))
            u = sol_ref[h, rs, 0:D] - _dot(sol_ref[h, rs, D:2 * D].astype(BF16), St16)
            u16 = u.astype(BF16)
            o = _dot(qd_ref[h, rs, :], St16)
            if first:
                ubuf[h] = jnp.concatenate([u16, jnp.zeros_like(u16)], axis=0)
                o = o + _dot(aqk_ref[h, rs, 0:C], u16)
            else:
                ubuf[h, C:P, :] = u16
                o = o + _dot(aqk_ref[h, rs, :], ubuf[h])
            o = _rms(o, gain_ref[...]) * gate_ref[rs, sl]
            o_ref[rs, sl] = o.astype(BF16)
    for h in range(GDN_HEADS):
        state[h] = states[h]


def _gdn(act, gate, small, B, S, alog_row, dtb_row, gain_row, a_lane, b_lane, rows=256):
    T = B * S
    W = GROUP_WIDTH
    H, D = GDN_HEADS, GDN_HEAD_DIM
    P = 2 * GDN_CHUNK
    nt = S // rows
    row = lambda b, i: (b * nt + i, 0)
    return pl.pallas_call(
        functools.partial(_gdn_kernel, a_lane=a_lane, b_lane=b_lane, rows=rows),
        grid=(B, nt),
        in_specs=[pl.BlockSpec((rows, 3 * W), row),
                  pl.BlockSpec((rows, W), row),
                  pl.BlockSpec((rows, SMALL_W), row),
                  _resident((1, SMALL_W)),
                  _resident((1, SMALL_W)),
                  _resident((1, D))],
        out_specs=pl.BlockSpec((rows, W), row),
        out_shape=jax.ShapeDtypeStruct((T, W), BF16),
        scratch_shapes=[pltpu.VMEM((H, D, D), F32),
                        pltpu.VMEM((H, rows, 2 * D), F32),
                        pltpu.VMEM((H, rows, D), BF16),
                        pltpu.VMEM((H, rows, D), BF16),
                        pltpu.VMEM((H, rows, P), BF16),
                        pltpu.VMEM((H, rows // GDN_CHUNK, D, 2 * D), F32),
                        pltpu.VMEM((H, P, D), BF16)],
        compiler_params=_params(2),
        name="gdn",
    )(act, gate, small, alog_row, dtb_row, gain_row)


def _block_ref_rows(x, half):
    R, L = x.shape
    if half >= SUBLANES:
        xb = x.reshape(R // (2 * half), 2 * half, L)
        return jnp.broadcast_to(xb[:, half:half + 1, :], xb.shape).reshape(R, L)
    xb = x.reshape(R // SUBLANES, SUBLANES, L)
    sub = lax.broadcasted_iota(jnp.int32, (1, SUBLANES, 1), 1)
    out = None
    for start in range(SUBLANES - 2 * half, -1, -2 * half):
        row = jnp.broadcast_to(xb[:, start + half:start + half + 1, :], xb.shape)
        out = row if out is None else jnp.where(sub < start + 2 * half, row, out)
    return out.reshape(R, L)


def _hgrn_kernel(q_ref, f_ref, k_ref, i_ref, g_ref, gain_ref, o_ref, state, *, rows):
    C = HGRN_CHUNK
    P = 2 * C
    D = HGRN_HEAD_DIM
    halves = [1 << b for b in range(C.bit_length() - 1)]

    @pl.when(pl.program_id(1) == 0)
    def _():
        state[...] = jnp.zeros_like(state)

    r_id = lax.broadcasted_iota(jnp.int32, (P, P), 0)
    c_id = lax.broadcasted_iota(jnp.int32, (P, P), 1)
    tri16 = ((c_id >= r_id - (r_id & (C - 1))) & (c_id <= r_id)).astype(BF16)
    differ = r_id ^ c_id
    row_id = lax.broadcasted_iota(jnp.int32, (P, 1), 0)
    gain = gain_ref[...]
    for pr in range(rows // P):
        ps = slice(pr * P, (pr + 1) * P)
        k_all = k_ref[ps, :]
        q_all = q_ref[ps, :]
        G_all = _exact_dot(tri16, f_ref[ps, :])
        decays = [jnp.exp(-jnp.abs(G_all - _block_ref_rows(G_all, half))) for half in halves]
        for h in range(HGRN_HEADS):
            sl = slice(h * D, (h + 1) * D)
            q, k, G = q_all[:, sl], k_all[:, sl], G_all[:, sl]
            q16, k16 = q.astype(BF16), k.astype(BF16)
            v16 = i_ref[ps, sl].astype(BF16)
            a = jnp.zeros((P, P), F32)
            for half, dec in reversed(list(zip(halves, decays))):
                upper = (row_id & half) != 0
                z16 = (jnp.where(upper, q, k) * dec[:, sl]).astype(BF16)
                a = jnp.where(differ < 2 * half, _dot_nt(z16, z16), a)
            a = jnp.where(differ == 0, _dot_nt(q16, k16), a)
            a = jnp.where(r_id >= c_id, a, 0.0)
            o_intra = _dot(a.astype(BF16), v16)
            q_in = (q * jnp.exp(G)).astype(BF16)
            for c in range(2):
                cs = slice(c * C, (c + 1) * C)
                G_last = G[(c + 1) * C - 1:(c + 1) * C, :]
                St = state[h]
                o = _dot_nt(q_in[cs, :], St.astype(BF16)) + o_intra[cs, :]
                k_out = (k[cs, :] * jnp.exp(G_last - G[cs, :])).astype(BF16)
                state[h] = jnp.exp(G_last) * St + _dot_tn(v16[cs, :], k_out)
                rs = slice(pr * P + c * C, pr * P + (c + 1) * C)
                o_ref[rs, sl] = (_rms(o, gain) * g_ref[rs, sl]).astype(BF16)


def _hgrn(proj, first_blk, B, S, gain_row, rows=256):
    T = B * S
    W = GROUP_WIDTH
    nt = S // rows
    spec = lambda j: pl.BlockSpec((rows, W), lambda b, i: (b * nt + i, j))
    return pl.pallas_call(
        functools.partial(_hgrn_kernel, rows=rows),
        grid=(B, nt),
        in_specs=[spec(first_blk + j) for j in range(5)] + [_resident((1, HGRN_HEAD_DIM))],
        out_specs=spec(0),
        out_shape=jax.ShapeDtypeStruct((T, W), BF16),
        scratch_shapes=[pltpu.VMEM((HGRN_HEADS, HGRN_HEAD_DIM, HGRN_HEAD_DIM), F32)],
        compiler_params=_params(2),
        name="hgrn2",
    )(proj, proj, proj, proj, proj, gain_row)


def _ssd_kernel(xbc_ref, z_ref, small_ref, alog_ref, dtb_ref, dvec_ref, gain_ref,
                o_ref, state, *, rows):
    L = M2_CHUNK
    W = GROUP_WIDTH
    N = M2_STATE
    pairs_per_group = M2_HEADS // M2_GROUPS // 2

    @pl.when(pl.program_id(1) == 0)
    def _():
        state[...] = jnp.zeros_like(state)

    dt_all = _softplus(small_ref[...] + dtb_ref[...])
    tri = _tril(L)
    tri16 = tri.astype(BF16)
    neg_a = -jnp.exp(alog_ref[...])
    lo = _lane_lo((1, LANES))
    for ck in range(rows // L):
        rs = slice(ck * L, (ck + 1) * L)
        xbc = xbc_ref[rs, :]
        dt = dt_all[rs, :]
        A_cs = _exact_dot(tri16, dt * neg_a)
        A_rows = _rows_as_lanes(A_cs, 0, M2_HEADS)
        for g in range(M2_GROUPS):
            Bg = xbc[:, W + g * N:W + (g + 1) * N]
            Cg = xbc[:, W + M2_GROUPS * N + g * N:W + M2_GROUPS * N + (g + 1) * N]
            cb = _dot_nt(Cg.astype(BF16), Bg.astype(BF16))
            ys = []
            for pp in range(pairs_per_group):
                p = g * pairs_per_group + pp
                xs = xbc[:, p * LANES:(p + 1) * LANES]
                dtl = jnp.where(lo, dt[:, 2 * p:2 * p + 1], dt[:, 2 * p + 1:2 * p + 2])
                X16 = (xs * dtl).astype(BF16)
                St = state[p]
                rhs = jnp.concatenate([X16, St.astype(BF16)], axis=0)
                y_h, st_h = [], []
                for e in range(2):
                    h = 2 * p + e
                    ac = A_cs[:, h:h + 1]
                    ar = A_rows[h:h + 1, :]
                    decay = jnp.exp(jnp.where(tri, ac - ar, NEG_INF))
                    lhs = jnp.concatenate([(cb * decay).astype(BF16),
                                           (Cg * jnp.exp(ac)).astype(BF16)], axis=1)
                    y_h.append(_dot(lhs, rhs))
                    last = ac[L - 1:L, :]
                    st_h.append(jnp.exp(last) * St
                                + _dot_tn((Bg * jnp.exp(last - ac)).astype(BF16), X16))
                state[p] = jnp.where(lo, st_h[0], st_h[1])
                ys.append(jnp.where(lo, y_h[0], y_h[1]) + dvec_ref[:, p * LANES:(p + 1) * LANES] * xs)
            gs = slice(g * (W // M2_GROUPS), (g + 1) * (W // M2_GROUPS))
            y = jnp.concatenate(ys, axis=1) * z_ref[rs, gs]
            o_ref[rs, gs] = _rms(y, gain_ref[:, gs]).astype(BF16)


def _ssd(proj, z_blk, small_blk, B, S, alog_row, dtb_row, dvec, gain_row, rows=256):
    T = B * S
    W = GROUP_WIDTH
    CW = W + 2 * M2_GROUPS * M2_STATE
    nt = S // rows
    row = lambda b, i: b * nt + i
    return pl.pallas_call(
        functools.partial(_ssd_kernel, rows=rows),
        grid=(B, nt),
        in_specs=[pl.BlockSpec((rows, CW), lambda b, i: (row(b, i), 0)),
                  pl.BlockSpec((rows, W), lambda b, i: (row(b, i), z_blk)),
                  pl.BlockSpec((rows, SMALL_W), lambda b, i: (row(b, i), small_blk)),
                  _resident((1, SMALL_W)),
                  _resident((1, SMALL_W)),
                  _resident((1, W)),
                  _resident((1, W))],
        out_specs=pl.BlockSpec((rows, W), lambda b, i: (row(b, i), 0)),
        out_shape=jax.ShapeDtypeStruct((T, W), BF16),
        scratch_shapes=[pltpu.VMEM((M2_HEADS // 2, M2_STATE, LANES), F32)],
        compiler_params=_params(2),
        name="ssd",
    )(proj, proj, proj, alog_row, dtb_row, dvec, gain_row)


def _pad_lanes(v, first, width=SMALL_W):
    v = v.astype(F32)
    return jnp.pad(v, (first, width - first - v.shape[0])).reshape(1, width)


def kernel(x, norm_gains, w_out, ffn_w_up, ffn_conv_w, ffn_conv_b, ffn_w_down,
           even_w_in, fox_f_bias, gdn_conv_w, gdn_A_log, gdn_dt_bias, gdn_norm_gain,
           odd_w_in, hgrn_lb_logits, hgrn_norm_gain, m2_conv_w, m2_conv_b,
           m2_A_log, m2_dt_bias, m2_D, m2_norm_gain):
    B, S, D = x.shape
    assert D == D_MODEL and S % 512 == 0
    T = B * S
    W = GROUP_WIDTH
    tm = 512
    row = lambda v: v.astype(F32).reshape(1, -1)
    x2 = x.reshape(T, D).astype(F32)

    o_ff = 3 * W
    o_qkv = o_ff + FOX_HEADS
    o_a = o_qkv + 3 * W
    o_b = o_a + GDN_HEADS
    o_gate = o_b + GDN_HEADS
    a_lane, b_lane = FOX_HEADS, FOX_HEADS + GDN_HEADS
    pad = jnp.zeros((D, SMALL_W - FOX_HEADS - 2 * GDN_HEADS), even_w_in.dtype)
    w_even = jnp.concatenate([even_w_in[:, :o_ff], even_w_in[:, o_qkv:o_a], even_w_in[:, o_gate:],
                              even_w_in[:, o_ff:o_qkv], even_w_in[:, o_a:o_gate], pad],
                             axis=1).astype(BF16)
    g = norm_gains[0]
    qkv, gdn_act, gdn_gate, small = _inproj_even(x2, row(g[0]), w_even, gdn_conv_w.astype(F32),
                                                 tm, S // tm)
    fox_blk = 256
    qt, ka, vt = _foxgate(qkv, small, B, S, 0, _pad_lanes(fox_f_bias, 0), fox_blk)
    o_fox = _fox(qt, ka, vt, B, S, fox_blk)
    o_gdn = _gdn(gdn_act, gdn_gate, small, B, S, _pad_lanes(gdn_A_log, a_lane),
                 _pad_lanes(gdn_dt_bias, a_lane), row(gdn_norm_gain), a_lane, b_lane)
    x2 = _mix_ffn(o_fox, o_gdn, w_out[0].astype(BF16), x2, B, S, row(g[1]), row(g[2]),
                  ffn_w_up[0].astype(BF16), ffn_conv_w[0].astype(F32), row(ffn_conv_b[0]),
                  ffn_w_down[0].astype(BF16), row(g[3]), tm)

    assert hgrn_lb_logits.shape == (2, W)
    CW = W + 2 * M2_GROUPS * M2_STATE
    pad = jnp.zeros((D, SMALL_W - M2_HEADS), odd_w_in.dtype)
    w_odd = jnp.concatenate([odd_w_in[:, 5 * W:5 * W + CW], odd_w_in[:, :5 * W],
                             odd_w_in[:, 5 * W + CW:], pad], axis=1).astype(BF16)
    g = norm_gains[1]
    proj = _inproj_odd(x2, row(g[0]), w_odd, hgrn_lb_logits.astype(F32), m2_conv_w.astype(F32),
                       row(m2_conv_b), tm, S // tm)
    o_hgrn = _hgrn(proj, CW // W, B, S, row(hgrn_norm_gain))
    o_ssd = _ssd(proj, (CW + 5 * W) // W, (CW + 6 * W) // SMALL_W, B, S, _pad_lanes(m2_A_log, 0),
                 _pad_lanes(m2_dt_bias, 0), row(jnp.repeat(m2_D, M2_HEAD_DIM)), row(m2_norm_gain))
    x2 = _mix_ffn(o_hgrn, o_ssd, w_out[1].astype(BF16), x2, B, S, row(g[1]), row(g[2]),
                  ffn_w_up[1].astype(BF16), ffn_conv_w[1].astype(F32), row(ffn_conv_b[1]),
                  ffn_w_down[1].astype(BF16), row(g[3]), tm)
    return x2.reshape(B, S, D).astype(x.dtype)
```

```python
import functools

import jax
import jax.numpy as jnp
from jax import lax
from jax.experimental import pallas as pl
from jax.experimental.pallas import tpu as pltpu

F32 = jnp.float32
BF16 = jnp.bfloat16

D_MODEL = 1024
GROUP_WIDTH = D_MODEL // 2
FOX_HEAD_DIM = 64
FOX_HEADS = GROUP_WIDTH // FOX_HEAD_DIM
FOX_AUG = 3
GDN_HEAD_DIM = 128
GDN_HEADS = GROUP_WIDTH // GDN_HEAD_DIM
GDN_CHUNK = 64
SHORT_CONV = 4
HGRN_HEAD_DIM = 128
HGRN_HEADS = GROUP_WIDTH // HGRN_HEAD_DIM
HGRN_CHUNK = 64
M2_HEAD_DIM = 64
M2_HEADS = GROUP_WIDTH // M2_HEAD_DIM
M2_GROUPS = 2
M2_STATE = 128
M2_CHUNK = 128
D_FF = 2816
FFN_CONV = 3
NORM_EPS = 1e-6

LANES = 128
SUBLANES = 8
SMALL_W = LANES
VMEM_LIMIT = 56 * 1024 * 1024

NEG_INF = float("-inf")


def _dot(a, b):
    return jnp.dot(a, b, preferred_element_type=F32)


def _dot_nt(a, b):
    return lax.dot_general(a, b, (((1,), (1,)), ((), ())), preferred_element_type=F32)


def _dot_tn(a, b):
    return lax.dot_general(a, b, (((0,), (0,)), ((), ())), preferred_element_type=F32)


def _bf16_pieces(x):
    hi = x.astype(BF16)
    rest = x - hi.astype(F32)
    mid = rest.astype(BF16)
    return hi, mid, (rest - mid.astype(F32)).astype(BF16)


def _exact_dot(sel16, x):
    hi, mid, lo = _bf16_pieces(x)
    return _dot(sel16, hi) + _dot(sel16, mid) + _dot(sel16, lo)


def _rms(x, gain):
    return x * lax.rsqrt(jnp.mean(x * x, axis=-1, keepdims=True) + NORM_EPS) * gain


def _sigmoid(x):
    return 1.0 / (1.0 + jnp.exp(-x))


def _silu(x):
    h = 0.5 * x
    return h * jnp.tanh(h) + h


def _softplus(x):
    return jnp.maximum(x, 0.0) + jnp.log1p(jnp.exp(-jnp.abs(x)))


def _log_sigmoid(x):
    return jnp.minimum(x, 0.0) - jnp.log1p(jnp.exp(-jnp.abs(x)))


def _tril(n, strict=False):
    r = lax.broadcasted_iota(jnp.int32, (n, n), 0)
    c = lax.broadcasted_iota(jnp.int32, (n, n), 1)
    return (r > c) if strict else (r >= c)


def _lane_lo(shape):
    return lax.broadcasted_iota(jnp.int32, shape, len(shape) - 1) < (LANES // 2)


def _rows_as_lanes(cols, first, n):
    sel = (lax.broadcasted_iota(jnp.int32, (n, LANES), 1)
           == lax.broadcasted_iota(jnp.int32, (n, LANES), 0) + first).astype(BF16)
    hi, mid, lo = _bf16_pieces(cols)
    return _dot_nt(sel, hi) + _dot_nt(sel, mid) + _dot_nt(sel, lo)


def _params(n_grid):
    return pltpu.CompilerParams(dimension_semantics=("arbitrary",) * n_grid,
                                vmem_limit_bytes=VMEM_LIMIT)


def _resident(shape):
    nd = len(shape)
    return pl.BlockSpec(shape, lambda *_: (0,) * nd)


def _conv_tail_reset(tail, tiles_per_seq):
    @pl.when(pl.program_id(0) % tiles_per_seq == 0)
    def _():
        tail[...] = jnp.zeros_like(tail)


def _conv_chunk(p, tail, w_ref, cols, width):
    rows = p.shape[0]
    ext = jnp.concatenate([tail[:, cols], p], axis=0)
    tail[:, cols] = p[rows - SUBLANES:rows, :]
    y = None
    for k in range(width):
        start = SUBLANES - (width - 1) + k
        term = ext[start:start + rows, :] * w_ref[k:k + 1, cols]
        y = term if y is None else y + term
    return y


def _staggered(stages):
    pending = None
    for produce, consume in stages:
        value = produce()
        if pending is not None:
            pending[1](pending[0])
        pending = (value, consume)
    pending[1](pending[0])


def _inproj_even_kernel(x_ref, g_ref, w_ref, cw_ref, qkv_ref, act_ref, gate_ref, small_ref, tail,
                        *, tm, tiles_per_seq):
    W = GROUP_WIDTH
    D = GDN_HEAD_DIM
    _conv_tail_reset(tail, tiles_per_seq)
    hn = _rms(x_ref[...], g_ref[...]).astype(BF16)
    proj = lambda c0, n: (lambda: _dot(hn, w_ref[:, c0:c0 + n]))

    PW = 2 * D

    def fox_out(c0):
        def consume(p):
            qkv_ref[:, c0:c0 + PW] = p.astype(BF16)
        return consume

    def gdn_out(c0):
        def consume(p):
            a = _silu(_conv_chunk(p, tail, cw_ref, slice(c0, c0 + PW), SHORT_CONV))
            for h in range(PW // D):
                ah = a[:, h * D:(h + 1) * D]
                if c0 < 2 * W:
                    ah = ah * lax.rsqrt(jnp.sum(ah * ah, axis=-1, keepdims=True) + NORM_EPS)
                if c0 < W:
                    ah = ah * D ** -0.5
                act_ref[:, c0 + h * D:c0 + (h + 1) * D] = ah
        return consume

    def gate_out(c0):
        def consume(p):
            gate_ref[:, c0:c0 + PW] = _silu(p)
        return consume

    def small_out(p):
        small_ref[...] = p

    stages = []
    for c0 in range(0, 3 * W, PW):
        stages += [(proj(3 * W + c0, PW), gdn_out(c0)), (proj(c0, PW), fox_out(c0))]
    stages += [(proj(6 * W + c0, PW), gate_out(c0)) for c0 in range(0, W, PW)]
    stages += [(proj(7 * W, SMALL_W), small_out)]
    _staggered(stages)


def _inproj_even(x2, gain, w, conv_w, tm, tiles_per_seq):
    T = x2.shape[0]
    W = GROUP_WIDTH
    widths = (3 * W, 3 * W, W, SMALL_W)
    dtypes = (BF16, F32, F32, F32)
    return pl.pallas_call(
        functools.partial(_inproj_even_kernel, tm=tm, tiles_per_seq=tiles_per_seq),
        grid=(T // tm,),
        in_specs=[pl.BlockSpec((tm, D_MODEL), lambda i: (i, 0)),
                  _resident((1, D_MODEL)),
                  _resident(w.shape),
                  _resident(conv_w.shape)],
        out_specs=[pl.BlockSpec((tm, n), lambda i: (i, 0)) for n in widths],
        out_shape=[jax.ShapeDtypeStruct((T, n), dt) for n, dt in zip(widths, dtypes)],
        scratch_shapes=[pltpu.VMEM((SUBLANES, 3 * W), F32)],
        compiler_params=_params(1),
        name="inproj_even",
    )(x2, gain, w, conv_w)


def _inproj_odd_kernel(x_ref, g_ref, w_ref, lbl_ref, cw_ref, cb_ref, o_ref, tail,
                       *, tm, tiles_per_seq):
    W = GROUP_WIDTH
    CW = W + 2 * M2_GROUPS * M2_STATE
    _conv_tail_reset(tail, tiles_per_seq)
    hn = _rms(x_ref[...], g_ref[...]).astype(BF16)
    proj = lambda c0, n: (lambda: _dot(hn, w_ref[:, c0:c0 + n]))

    logits = lbl_ref[...]
    e = jnp.exp(logits - jnp.max(logits, axis=0, keepdims=True))
    prob = e / jnp.sum(e, axis=0, keepdims=True)
    lb = (prob[0:1, :] + prob[1:2, :]) - prob[0:1, :]

    def conv_out(c):
        def consume(p):
            cols = slice(c * W, (c + 1) * W)
            o_ref[:, cols] = _silu(_conv_chunk(p, tail, cw_ref, cols, SHORT_CONV) + cb_ref[:, cols])
        return consume

    def mapped_out(col, fn):
        def consume(p):
            o_ref[:, col:col + p.shape[1]] = fn(p)
        return consume

    def forget_out(col):
        def consume(p):
            gate = _sigmoid(p)
            o_ref[:, col:col + W] = jnp.log(lb + (1.0 - lb) * gate)
            o_ref[:, col + W:col + 2 * W] = (1.0 - lb) * (1.0 - gate)
        return consume

    keep = lambda p: p
    stages = [(proj(c * W, W), conv_out(c)) for c in range(CW // W)]
    stages += [(proj(CW, W), mapped_out(CW, _silu)),
               (proj(CW + W, W), forget_out(CW + W)),
               (proj(CW + 2 * W, W), mapped_out(CW + 3 * W, keep)),
               (proj(CW + 3 * W, W), mapped_out(CW + 4 * W, _silu)),
               (proj(CW + 4 * W, W), mapped_out(CW + 5 * W, _silu)),
               (proj(CW + 5 * W, SMALL_W), mapped_out(CW + 6 * W, keep))]
    _staggered(stages)


def _inproj_odd(x2, gain, w, lb_logits, conv_w, conv_b, tm, tiles_per_seq):
    T = x2.shape[0]
    W = GROUP_WIDTH
    CW = W + 2 * M2_GROUPS * M2_STATE
    n_out = CW + 6 * W + SMALL_W
    return pl.pallas_call(
        functools.partial(_inproj_odd_kernel, tm=tm, tiles_per_seq=tiles_per_seq),
        grid=(T // tm,),
        in_specs=[pl.BlockSpec((tm, D_MODEL), lambda i: (i, 0)),
                  _resident((1, D_MODEL)),
                  _resident(w.shape),
                  _resident(lb_logits.shape),
                  _resident(conv_w.shape),
                  _resident(conv_b.shape)],
        out_specs=pl.BlockSpec((tm, n_out), lambda i: (i, 0)),
        out_shape=jax.ShapeDtypeStruct((T, n_out), F32),
        scratch_shapes=[pltpu.VMEM((SUBLANES, CW), F32)],
        compiler_params=_params(1),
        name="inproj_odd",
    )(x2, gain, w, lb_logits, conv_w, conv_b)


def _mix_ffn_kernel(ma_ref, mb_ref, wo_ref, x_ref, gmix_ref, gpre_ref, wup_ref, cw_ref, cb_ref,
                    wdn_ref, gpost_ref, o_ref, carry, act, *, tm, fc, group):
    W = GROUP_WIDTH
    n_chunks = D_FF // fc

    @pl.when(pl.program_id(1) == 0)
    def _():
        carry[...] = jnp.zeros_like(carry)

    mix = _dot(ma_ref[...], wo_ref[0:W, :]) + _dot(mb_ref[...], wo_ref[W:2 * W, :])
    x = x_ref[...] + _rms(mix, gmix_ref[...])
    hn = _rms(x, gpre_ref[...]).astype(BF16)
    y = None
    for c in range(n_chunks):
        halves = []
        for half in range(2):
            cols = slice(half * D_FF + c * fc, half * D_FF + (c + 1) * fc)
            u = _dot(hn, wup_ref[:, cols])
            halves.append(_conv_chunk(u, carry, cw_ref, cols, FFN_CONV) + cb_ref[:, cols])
        act[:, c * fc:(c + 1) * fc] = (_silu(halves[0]) * halves[1]).astype(BF16)
        if (c + 1) % group == 0 or c + 1 == n_chunks:
            rows = slice((c // group) * group * fc, (c + 1) * fc)
            part = _dot(act[:, rows], wdn_ref[rows, :])
            y = part if y is None else y + part
    o_ref[...] = x + _rms(y, gpost_ref[...])


def _mix_ffn(mix_a, mix_b, w_out, x2, B, S, gmix, gpre, w_up, conv_w, conv_b, w_down, gpost,
             tm, fc=256):
    W = GROUP_WIDTH
    nt = S // tm
    row = lambda b, i: (b * nt + i, 0)
    return pl.pallas_call(
        functools.partial(_mix_ffn_kernel, tm=tm, fc=fc, group=4),
        grid=(B, nt),
        in_specs=[pl.BlockSpec((tm, W), row),
                  pl.BlockSpec((tm, W), row),
                  _resident((D_MODEL, D_MODEL)),
                  pl.BlockSpec((tm, D_MODEL), row),
                  _resident((1, D_MODEL)),
                  _resident((1, D_MODEL)),
                  _resident((D_MODEL, 2 * D_FF)),
                  _resident((FFN_CONV, 2 * D_FF)),
                  _resident((1, 2 * D_FF)),
                  _resident((D_FF, D_MODEL)),
                  _resident((1, D_MODEL))],
        out_specs=pl.BlockSpec((tm, D_MODEL), row),
        out_shape=jax.ShapeDtypeStruct(x2.shape, F32),
        scratch_shapes=[pltpu.VMEM((SUBLANES, 2 * D_FF), F32),
                        pltpu.VMEM((tm, D_FF), BF16)],
        compiler_params=_params(2),
        name="mixffn",
    )(mix_a, mix_b, w_out, x2, gmix, gpre, w_up, conv_w, conv_b, w_down, gpost)


def _foxgate_kernel(s_ref, b_ref, qk_ref, qt_ref, ka_ref, vt_ref, carry, *, blk):
    W = GROUP_WIDTH

    @pl.when(pl.program_id(1) == 0)
    def _():
        carry[...] = jnp.zeros_like(carry)

    z = s_ref[...] + b_ref[...]
    cs = _exact_dot(_tril(blk).astype(BF16), _log_sigmoid(z)) + carry[...]
    carry[...] = cs[blk - 1:blk, :]
    lane = lax.broadcasted_iota(jnp.int32, (1, LANES), 1)
    lo = lane < FOX_HEAD_DIM
    ones = jnp.where(lane < FOX_HEAD_DIM + FOX_AUG, 1.0, 0.0)
    scale = FOX_HEAD_DIM ** -0.5
    for p in range(FOX_HEADS // 2):
        q = qk_ref[:, p * LANES:(p + 1) * LANES].astype(F32) * scale
        k = qk_ref[:, W + p * LANES:W + (p + 1) * LANES].astype(F32)
        for e in range(2):
            h = 2 * p + e
            qe = pltpu.roll(q, FOX_HEAD_DIM, 1) if e else q
            ke = pltpu.roll(k, FOX_HEAD_DIM, 1) if e else k
            rem = -cs[:, h:h + 1]
            aug = jnp.zeros((blk, LANES), F32)
            for piece in range(FOX_AUG):
                part = rem.astype(BF16).astype(F32)
                aug = jnp.where(lane == FOX_HEAD_DIM + piece, part, aug)
                rem = rem - part
            qt_ref[0, 0, h * LANES:(h + 1) * LANES, :] = jnp.where(lo, qe, ones).T.astype(BF16)
            ka_ref[:, h * LANES:(h + 1) * LANES] = jnp.where(lo, ke, aug).astype(BF16)
        v = qk_ref[:, 2 * W + p * LANES:2 * W + (p + 1) * LANES].astype(F32)
        vt_ref[0, 0, p * LANES:(p + 1) * LANES, :] = v.T.astype(BF16)


def _foxgate(qkv, small, B, S, small_blk, bias_row, blk):
    T = B * S
    W = GROUP_WIDTH
    nb = S // blk
    wide = FOX_HEADS * LANES
    row = lambda b, j: b * nb + j
    return pl.pallas_call(
        functools.partial(_foxgate_kernel, blk=blk),
        grid=(B, nb),
        in_specs=[pl.BlockSpec((blk, SMALL_W), lambda b, j: (row(b, j), small_blk)),
                  _resident((1, SMALL_W)),
                  pl.BlockSpec((blk, 3 * W), lambda b, j: (row(b, j), 0))],
        out_specs=[pl.BlockSpec((1, 1, wide, blk), lambda b, j: (b, j, 0, 0)),
                   pl.BlockSpec((blk, wide), lambda b, j: (row(b, j), 0)),
                   pl.BlockSpec((1, 1, W, blk), lambda b, j: (b, j, 0, 0))],
        out_shape=[jax.ShapeDtypeStruct((B, nb, wide, blk), BF16),
                   jax.ShapeDtypeStruct((T, wide), BF16),
                   jax.ShapeDtypeStruct((B, nb, W, blk), BF16)],
        scratch_shapes=[pltpu.VMEM((1, SMALL_W), F32)],
        compiler_params=_params(2),
        name="foxgate",
    )(small, bias_row, qkv)


def _fox_kernel(qt_ref, k_ref, vt_ref, o_ref, m_ref, l_ref, acc_ref, *, blk, kvb):
    i = pl.program_id(1)
    kv_id = lax.broadcasted_iota(jnp.int32, (kvb, blk), 0)
    q_id = lax.broadcasted_iota(jnp.int32, (kvb, blk), 1)
    first_head = lax.broadcasted_iota(jnp.int32, (LANES, 1), 0) < FOX_HEAD_DIM
    m_ref[...] = jnp.full(m_ref.shape, NEG_INF, F32)
    l_ref[...] = jnp.zeros_like(l_ref)
    acc_ref[...] = jnp.zeros_like(acc_ref)

    def sub_block(j, sub, masked):
        off = pl.multiple_of(j * blk + sub * kvb, kvb)
        for p in range(FOX_HEADS // 2):
            vt = vt_ref[0, j, p * LANES:(p + 1) * LANES, sub * kvb:(sub + 1) * kvb]
            alpha, pv = [], []
            for e in range(2):
                h = 2 * p + e
                hs = slice(h * LANES, (h + 1) * LANES)
                s = _dot(k_ref[pl.ds(off, kvb), hs], qt_ref[0, 0, hs, :])
                if masked:
                    s = jnp.where(kv_id + sub * kvb <= q_id, s, NEG_INF)
                m_old = m_ref[h:h + 1, :]
                m_new = jnp.maximum(m_old, jnp.max(s, axis=0, keepdims=True))
                pe = jnp.exp(s - m_new)
                a = jnp.exp(m_old - m_new)
                m_ref[h:h + 1, :] = m_new
                l_ref[h:h + 1, :] = a * l_ref[h:h + 1, :] + jnp.sum(pe, axis=0, keepdims=True)
                alpha.append(a)
                pv.append(_dot(vt, pe.astype(BF16)))
            acc_ref[p] = (jnp.where(first_head, alpha[0], alpha[1]) * acc_ref[p]
                          + jnp.where(first_head, pv[0], pv[1]))

    def block(j, masked):
        for sub in range(blk // kvb):
            sub_block(j, sub, masked)

    def body(j, carry):
        block(j, False)
        return carry

    lax.fori_loop(0, i, body, 0)
    block(i, True)
    for p in range(FOX_HEADS // 2):
        l = jnp.where(first_head, l_ref[2 * p:2 * p + 1, :], l_ref[2 * p + 1:2 * p + 2, :])
        o_ref[:, p * LANES:(p + 1) * LANES] = (acc_ref[p] / l).T.astype(BF16)


def _fox(qt, ka, vt, B, S, blk):
    T = B * S
    nq = S // blk
    W = GROUP_WIDTH
    wide = FOX_HEADS * LANES
    return pl.pallas_call(
        functools.partial(_fox_kernel, blk=blk, kvb=128),
        grid=(B, nq),
        in_specs=[pl.BlockSpec((1, 1, wide, blk), lambda b, i: (b, i, 0, 0)),
                  pl.BlockSpec((S, wide), lambda b, i: (b, 0)),
                  pl.BlockSpec((1, nq, W, blk), lambda b, i: (b, 0, 0, 0))],
        out_specs=pl.BlockSpec((blk, W), lambda b, i: (b * nq + i, 0)),
        out_shape=jax.ShapeDtypeStruct((T, W), BF16),
        scratch_shapes=[pltpu.VMEM((FOX_HEADS, blk), F32),
                        pltpu.VMEM((FOX_HEADS, blk), F32),
                        pltpu.VMEM((FOX_HEADS // 2, LANES, blk), F32)],
        compiler_params=_params(2),
        name="fox",
    )(qt, ka, vt)


def _unit_lower_solves(ms, rhss, n, nilpotent):
    eye = (lax.broadcasted_iota(jnp.int32, (n, n), 0)
           == lax.broadcasted_iota(jnp.int32, (n, n), 1)).astype(F32)
    xs = [-m for m in ms]
    sols = rhss
    power = 1
    while power < nilpotent:
        x16 = [x.astype(BF16) for x in xs]
        x2 = [_dot(x, x) for x in x16]
        x2_16 = [x.astype(BF16) for x in x2]
        factors = [eye + x + y + _dot(xb, yb) for x, y, xb, yb in zip(xs, x2, x16, x2_16)]
        sols = [_dot(f.astype(BF16), r.astype(BF16)) for f, r in zip(factors, sols)]
        power *= 4
        if power < nilpotent:
            xs = [_dot(y, y) for y in x2_16]
    assert power == nilpotent
    return sols


def _gdn_kernel(act_ref, gate_ref, small_ref, alog_ref, dtb_ref, gain_ref, o_ref,
                state, sol_ref, qd_ref, kd_ref, aqk_ref, ks_ref, ubuf, *, a_lane, b_lane, rows):
    C = GDN_CHUNK
    P = 2 * C
    W = GROUP_WIDTH
    D = GDN_HEAD_DIM
    n_chunks = rows // C

    @pl.when(pl.program_id(1) == 0)
    def _():
        state[...] = jnp.zeros_like(state)

    small = small_ref[...]
    g_all = -jnp.exp(alog_ref[...]) * _softplus(small + dtb_ref[...])
    beta_all = _sigmoid(small)
    r_id = lax.broadcasted_iota(jnp.int32, (P, P), 0)
    c_id = lax.broadcasted_iota(jnp.int32, (P, P), 1)
    tri_bd = (c_id >= r_id - (r_id & (C - 1))) & (c_id <= r_id)
    diag = r_id == c_id
    tri16 = tri_bd.astype(BF16)
    g_tot, ms, rhss, where = [], [], [], []
    for pr in range(rows // P):
        ps = slice(pr * P, (pr + 1) * P)
        G_all = _exact_dot(tri16, g_all[ps, :])
        G_rows = _rows_as_lanes(G_all, a_lane, SUBLANES)
        G_tot = jnp.concatenate(
            [jnp.broadcast_to(G_all[(c + 1) * C - 1:(c + 1) * C, :], (C, SMALL_W)) for c in range(2)],
            axis=0)
        g_tot.append(G_tot)
        for h in range(GDN_HEADS):
            q = act_ref[ps, h * D:(h + 1) * D]
            k = act_ref[ps, W + h * D:W + (h + 1) * D]
            v = act_ref[ps, 2 * W + h * D:2 * W + (h + 1) * D]
            Gc = G_all[:, a_lane + h:a_lane + h + 1]
            Gr = G_rows[h:h + 1, :]
            Gt = G_tot[:, a_lane + h:a_lane + h + 1]
            beta = beta_all[ps, b_lane + h:b_lane + h + 1]
            gamma = jnp.exp(jnp.where(tri_bd, Gc - Gr, NEG_INF))
            kb = k * beta
            kb16, k16 = kb.astype(BF16), k.astype(BF16)
            ms.append(jnp.where(diag, 0.0, _dot_nt(kb16, k16) * gamma))
            eG = jnp.exp(Gc)
            rhss.append(jnp.concatenate([v * beta, kb * eG], axis=1))
            where.append((h, ps))
            aqk_ref[h, ps, :] = (_dot_nt(q.astype(BF16), k16) * gamma).astype(BF16)
            qd_ref[h, ps, :] = (q * eG).astype(BF16)
            kd_ref[h, ps, :] = (k * jnp.exp(Gt - Gc)).astype(BF16)
    for (h, ps), sol in zip(where, _unit_lower_solves(ms, rhss, P, C)):
        sol_ref[h, ps, :] = sol
    for c in range(n_chunks):
        rs = slice(c * C, (c + 1) * C)
        for h in range(GDN_HEADS):
            ks_ref[h, c] = _dot_tn(kd_ref[h, rs, :], sol_ref[h, rs, :].astype(BF16))

    states = [state[h] for h in range(GDN_HEADS)]
    for c in range(n_chunks):
        rs = slice(c * C, (c + 1) * C)
        first = c % 2 == 0
        for h in range(GDN_HEADS):
            sl = slice(h * D, (h + 1) * D)
            St = states[h]
            St16 = St.astype(BF16)
            decay = jnp.exp(g_tot[c // 2][(c % 2) * C:(c % 2) * C + 1, a_lane + h:a_lane + h + 1])
            states[h] = (decay * St + ks_ref[h, c, :, 0:D]
                         - _dot(ks_ref[h, c, :, D:2 * D].astype(BF16), St16))
            u = sol_ref[h, rs, 0:D] - _dot(sol_ref[h, rs, D:2 * D].astype(BF16), St16)
            u16 = u.astype(BF16)
            o = _dot(qd_ref[h, rs, :], St16)
            if first:
                ubuf[h] = jnp.concatenate([u16, jnp.zeros_like(u16)], axis=0)
                o = o + _dot(aqk_ref[h, rs, 0:C], u16)
            else:
                ubuf[h, C:P, :] = u16
                o = o + _dot(aqk_ref[h, rs, :], ubuf[h])
            o = _rms(o, gain_ref[...]) * gate_ref[rs, sl]
            o_ref[rs, sl] = o.astype(BF16)
    for h in range(GDN_HEADS):
        state[h] = states[h]


def _gdn(act, gate, small, B, S, alog_row, dtb_row, gain_row, a_lane, b_lane, rows=512):
    T = B * S
    W = GROUP_WIDTH
    H, D = GDN_HEADS, GDN_HEAD_DIM
    P = 2 * GDN_CHUNK
    nt = S // rows
    row = lambda b, i: (b * nt + i, 0)
    return pl.pallas_call(
        functools.partial(_gdn_kernel, a_lane=a_lane, b_lane=b_lane, rows=rows),
        grid=(B, nt),
        in_specs=[pl.BlockSpec((rows, 3 * W), row),
                  pl.BlockSpec((rows, W), row),
                  pl.BlockSpec((rows, SMALL_W), row),
                  _resident((1, SMALL_W)),
                  _resident((1, SMALL_W)),
                  _resident((1, D))],
        out_specs=pl.BlockSpec((rows, W), row),
        out_shape=jax.ShapeDtypeStruct((T, W), BF16),
        scratch_shapes=[pltpu.VMEM((H, D, D), F32),
                        pltpu.VMEM((H, rows, 2 * D), F32),
                        pltpu.VMEM((H, rows, D), BF16),
                        pltpu.VMEM((H, rows, D), BF16),
                        pltpu.VMEM((H, rows, P), BF16),
                        pltpu.VMEM((H, rows // GDN_CHUNK, D, 2 * D), F32),
                        pltpu.VMEM((H, P, D), BF16)],
        compiler_params=_params(2),
        name="gdn",
    )(act, gate, small, alog_row, dtb_row, gain_row)


def _block_ref_rows(x, half):
    R, L = x.shape
    if half >= SUBLANES:
        xb = x.reshape(R // (2 * half), 2 * half, L)
        return jnp.broadcast_to(xb[:, half:half + 1, :], xb.shape).reshape(R, L)
    xb = x.reshape(R // SUBLANES, SUBLANES, L)
    sub = lax.broadcasted_iota(jnp.int32, (1, SUBLANES, 1), 1)
    out = None
    for start in range(SUBLANES - 2 * half, -1, -2 * half):
        row = jnp.broadcast_to(xb[:, start + half:start + half + 1, :], xb.shape)
        out = row if out is None else jnp.where(sub < start + 2 * half, row, out)
    return out.reshape(R, L)


def _hgrn_kernel(q_ref, f_ref, k_ref, i_ref, g_ref, gain_ref, o_ref, state, *, rows):
    C = HGRN_CHUNK
    P = 2 * C
    D = HGRN_HEAD_DIM
    halves = [1 << b for b in range(C.bit_length() - 1)]

    @pl.when(pl.program_id(1) == 0)
    def _():
        state[...] = jnp.zeros_like(state)

    r_id = lax.broadcasted_iota(jnp.int32, (P, P), 0)
    c_id = lax.broadcasted_iota(jnp.int32, (P, P), 1)
    tri16 = ((c_id >= r_id - (r_id & (C - 1))) & (c_id <= r_id)).astype(BF16)
    differ = r_id ^ c_id
    row_id = lax.broadcasted_iota(jnp.int32, (P, 1), 0)
    gain = gain_ref[...]
    for pr in range(rows // P):
        ps = slice(pr * P, (pr + 1) * P)
        k_all = k_ref[ps, :]
        q_all = q_ref[ps, :]
        G_all = _exact_dot(tri16, f_ref[ps, :])
        decays = [jnp.exp(-jnp.abs(G_all - _block_ref_rows(G_all, half))) for half in halves]
        for h in range(HGRN_HEADS):
            sl = slice(h * D, (h + 1) * D)
            q, k, G = q_all[:, sl], k_all[:, sl], G_all[:, sl]
            q16, k16 = q.astype(BF16), k.astype(BF16)
            v16 = i_ref[ps, sl].astype(BF16)
            a = jnp.zeros((P, P), F32)
            for half, dec in reversed(list(zip(halves, decays))):
                upper = (row_id & half) != 0
                z16 = (jnp.where(upper, q, k) * dec[:, sl]).astype(BF16)
                a = jnp.where(differ < 2 * half, _dot_nt(z16, z16), a)
            a = jnp.where(differ == 0, _dot_nt(q16, k16), a)
            a = jnp.where(r_id >= c_id, a, 0.0)
            o_intra = _dot(a.astype(BF16), v16)
            q_in = (q * jnp.exp(G)).astype(BF16)
            for c in range(2):
                cs = slice(c * C, (c + 1) * C)
                G_last = G[(c + 1) * C - 1:(c + 1) * C, :]
                St = state[h]
                o = _dot_nt(q_in[cs, :], St.astype(BF16)) + o_intra[cs, :]
                k_out = (k[cs, :] * jnp.exp(G_last - G[cs, :])).astype(BF16)
                state[h] = jnp.exp(G_last) * St + _dot_tn(v16[cs, :], k_out)
                rs = slice(pr * P + c * C, pr * P + (c + 1) * C)
                o_ref[rs, sl] = (_rms(o, gain) * g_ref[rs, sl]).astype(BF16)


def _hgrn(proj, first_blk, B, S, gain_row, rows=256):
    T = B * S
    W = GROUP_WIDTH
    nt = S // rows
    spec = lambda j: pl.BlockSpec((rows, W), lambda b, i: (b * nt + i, j))
    return pl.pallas_call(
        functools.partial(_hgrn_kernel, rows=rows),
        grid=(B, nt),
        in_specs=[spec(first_blk + j) for j in range(5)] + [_resident((1, HGRN_HEAD_DIM))],
        out_specs=spec(0),
        out_shape=jax.ShapeDtypeStruct((T, W), BF16),
        scratch_shapes=[pltpu.VMEM((HGRN_HEADS, HGRN_HEAD_DIM, HGRN_HEAD_DIM), F32)],
        compiler_params=_params(2),
        name="hgrn2",
    )(proj, proj, proj, proj, proj, gain_row)


def _ssd_kernel(xbc_ref, z_ref, small_ref, alog_ref, dtb_ref, dvec_ref, gain_ref,
                o_ref, state, *, rows):
    L = M2_CHUNK
    W = GROUP_WIDTH
    N = M2_STATE
    pairs_per_group = M2_HEADS // M2_GROUPS // 2

    @pl.when(pl.program_id(1) == 0)
    def _():
        state[...] = jnp.zeros_like(state)

    dt_all = _softplus(small_ref[...] + dtb_ref[...])
    tri = _tril(L)
    tri16 = tri.astype(BF16)
    neg_a = -jnp.exp(alog_ref[...])
    lo = _lane_lo((1, LANES))
    for ck in range(rows // L):
        rs = slice(ck * L, (ck + 1) * L)
        xbc = xbc_ref[rs, :]
        dt = dt_all[rs, :]
        A_cs = _exact_dot(tri16, dt * neg_a)
        A_rows = _rows_as_lanes(A_cs, 0, M2_HEADS)
        for g in range(M2_GROUPS):
            Bg = xbc[:, W + g * N:W + (g + 1) * N]
            Cg = xbc[:, W + M2_GROUPS * N + g * N:W + M2_GROUPS * N + (g + 1) * N]
            cb = _dot_nt(Cg.astype(BF16), Bg.astype(BF16))
            ys = []
            for pp in range(pairs_per_group):
                p = g * pairs_per_group + pp
                xs = xbc[:, p * LANES:(p + 1) * LANES]
                dtl = jnp.where(lo, dt[:, 2 * p:2 * p + 1], dt[:, 2 * p + 1:2 * p + 2])
                X16 = (xs * dtl).astype(BF16)
                St = state[p]
                rhs = jnp.concatenate([X16, St.astype(BF16)], axis=0)
                y_h, st_h = [], []
                for e in range(2):
                    h = 2 * p + e
                    ac = jnp.broadcast_to(A_cs[:, h:h + 1], (L, LANES))
                    ar = A_rows[h:h + 1, :]
                    decay = jnp.exp(jnp.where(tri, ac - ar, NEG_INF))
                    lhs = jnp.concatenate([(cb * decay).astype(BF16),
                                           (Cg * jnp.exp(ac)).astype(BF16)], axis=1)
                    y_h.append(_dot(lhs, rhs))
                    last = ac[L - 1:L, :]
                    st_h.append(jnp.exp(last) * St
                                + _dot_tn((Bg * jnp.exp(last - ac)).astype(BF16), X16))
                state[p] = jnp.where(lo, st_h[0], st_h[1])
                ys.append(jnp.where(lo, y_h[0], y_h[1]) + dvec_ref[:, p * LANES:(p + 1) * LANES] * xs)
            gs = slice(g * (W // M2_GROUPS), (g + 1) * (W // M2_GROUPS))
            y = jnp.concatenate(ys, axis=1) * z_ref[rs, gs]
            o_ref[rs, gs] = _rms(y, gain_ref[:, gs]).astype(BF16)


def _ssd(proj, z_blk, small_blk, B, S, alog_row, dtb_row, dvec, gain_row, rows=256):
    T = B * S
    W = GROUP_WIDTH
    CW = W + 2 * M2_GROUPS * M2_STATE
    nt = S // rows
    row = lambda b, i: b * nt + i
    return pl.pallas_call(
        functools.partial(_ssd_kernel, rows=rows),
        grid=(B, nt),
        in_specs=[pl.BlockSpec((rows, CW), lambda b, i: (row(b, i), 0)),
                  pl.BlockSpec((rows, W), lambda b, i: (row(b, i), z_blk)),
                  pl.BlockSpec((rows, SMALL_W), lambda b, i: (row(b, i), small_blk)),
                  _resident((1, SMALL_W)),
                  _resident((1, SMALL_W)),
                  _resident((1, W)),
                  _resident((1, W))],
        out_specs=pl.BlockSpec((rows, W), lambda b, i: (row(b, i), 0)),
        out_shape=jax.ShapeDtypeStruct((T, W), BF16),
        scratch_shapes=[pltpu.VMEM((M2_HEADS // 2, M2_STATE, LANES), F32)],
        compiler_params=_params(2),
        name="ssd",
    )(proj, proj, proj, alog_row, dtb_row, dvec, gain_row)


def _pad_lanes(v, first, width=SMALL_W):
    v = v.astype(F32)
    return jnp.pad(v, (first, width - first - v.shape[0])).reshape(1, width)


def kernel(x, norm_gains, w_out, ffn_w_up, ffn_conv_w, ffn_conv_b, ffn_w_down,
           even_w_in, fox_f_bias, gdn_conv_w, gdn_A_log, gdn_dt_bias, gdn_norm_gain,
           odd_w_in, hgrn_lb_logits, hgrn_norm_gain, m2_conv_w, m2_conv_b,
           m2_A_log, m2_dt_bias, m2_D, m2_norm_gain):
    B, S, D = x.shape
    assert D == D_MODEL and S % 512 == 0
    T = B * S
    W = GROUP_WIDTH
    tm = 512
    tm_in = 256
    tm_ffn = 512
    row = lambda v: v.astype(F32).reshape(1, -1)
    x2 = x.reshape(T, D).astype(F32)

    o_ff = 3 * W
    o_qkv = o_ff + FOX_HEADS
    o_a = o_qkv + 3 * W
    o_b = o_a + GDN_HEADS
    o_gate = o_b + GDN_HEADS
    a_lane, b_lane = FOX_HEADS, FOX_HEADS + GDN_HEADS
    pad = jnp.zeros((D, SMALL_W - FOX_HEADS - 2 * GDN_HEADS), even_w_in.dtype)
    w_even = jnp.concatenate([even_w_in[:, :o_ff], even_w_in[:, o_qkv:o_a], even_w_in[:, o_gate:],
                              even_w_in[:, o_ff:o_qkv], even_w_in[:, o_a:o_gate], pad],
                             axis=1).astype(BF16)
    g = norm_gains[0]
    qkv, gdn_act, gdn_gate, small = _inproj_even(x2, row(g[0]), w_even, gdn_conv_w.astype(F32),
                                                 tm_in, S // tm_in)
    fox_blk = 256
    qt, ka, vt = _foxgate(qkv, small, B, S, 0, _pad_lanes(fox_f_bias, 0), fox_blk)
    o_fox = _fox(qt, ka, vt, B, S, fox_blk)
    o_gdn = _gdn(gdn_act, gdn_gate, small, B, S, _pad_lanes(gdn_A_log, a_lane),
                 _pad_lanes(gdn_dt_bias, a_lane), row(gdn_norm_gain), a_lane, b_lane)
    x2 = _mix_ffn(o_fox, o_gdn, w_out[0].astype(BF16), x2, B, S, row(g[1]), row(g[2]),
                  ffn_w_up[0].astype(BF16), ffn_conv_w[0].astype(F32), row(ffn_conv_b[0]),
                  ffn_w_down[0].astype(BF16), row(g[3]), tm_ffn)

    assert hgrn_lb_logits.shape == (2, W)
    CW = W + 2 * M2_GROUPS * M2_STATE
    pad = jnp.zeros((D, SMALL_W - M2_HEADS), odd_w_in.dtype)
    w_odd = jnp.concatenate([odd_w_in[:, 5 * W:5 * W + CW], odd_w_in[:, :5 * W],
                             odd_w_in[:, 5 * W + CW:], pad], axis=1).astype(BF16)
    g = norm_gains[1]
    proj = _inproj_odd(x2, row(g[0]), w_odd, hgrn_lb_logits.astype(F32), m2_conv_w.astype(F32),
                       row(m2_conv_b), tm, S // tm)
    o_hgrn = _hgrn(proj, CW // W, B, S, row(hgrn_norm_gain))
    o_ssd = _ssd(proj, (CW + 5 * W) // W, (CW + 6 * W) // SMALL_W, B, S, _pad_lanes(m2_A_log, 0),
                 _pad_lanes(m2_dt_bias, 0), row(jnp.repeat(m2_D, M2_HEAD_DIM)), row(m2_norm_gain))
    x2 = _mix_ffn(o_hgrn, o_ssd, w_out[1].astype(BF16), x2, B, S, row(g[1]), row(g[2]),
                  ffn_w_up[1].astype(BF16), ffn_conv_w[1].astype(F32), row(ffn_conv_b[1]),
                  ffn_w_down[1].astype(BF16), row(g[3]), tm_ffn)
    return x2.reshape(B, S, D).astype(x.dtype)
```

```python
import functools

import jax
import jax.numpy as jnp
from jax import lax
from jax.experimental import pallas as pl
from jax.experimental.pallas import tpu as pltpu

F32 = jnp.float32
BF16 = jnp.bfloat16

D_MODEL = 1024
GROUP_WIDTH = D_MODEL // 2
FOX_HEAD_DIM = 64
FOX_HEADS = GROUP_WIDTH // FOX_HEAD_DIM
FOX_AUG = 3
GDN_HEAD_DIM = 128
GDN_HEADS = GROUP_WIDTH // GDN_HEAD_DIM
GDN_CHUNK = 64
SHORT_CONV = 4
HGRN_HEAD_DIM = 128
HGRN_HEADS = GROUP_WIDTH // HGRN_HEAD_DIM
HGRN_CHUNK = 64
M2_HEAD_DIM = 64
M2_HEADS = GROUP_WIDTH // M2_HEAD_DIM
M2_GROUPS = 2
M2_STATE = 128
M2_CHUNK = 128
D_FF = 2816
FFN_CONV = 3
NORM_EPS = 1e-6

LANES = 128
SUBLANES = 8
SMALL_W = LANES
VMEM_LIMIT = 56 * 1024 * 1024

NEG_INF = float("-inf")


def _dot(a, b):
    return jnp.dot(a, b, preferred_element_type=F32)


def _dot_nt(a, b):
    return lax.dot_general(a, b, (((1,), (1,)), ((), ())), preferred_element_type=F32)


def _dot_tn(a, b):
    return lax.dot_general(a, b, (((0,), (0,)), ((), ())), preferred_element_type=F32)


def _bf16_pieces(x):
    hi = x.astype(BF16)
    rest = x - hi.astype(F32)
    mid = rest.astype(BF16)
    return hi, mid, (rest - mid.astype(F32)).astype(BF16)


def _exact_dot(sel16, x):
    hi, mid, lo = _bf16_pieces(x)
    return _dot(sel16, hi) + _dot(sel16, mid) + _dot(sel16, lo)


def _rms(x, gain):
    return x * lax.rsqrt(jnp.mean(x * x, axis=-1, keepdims=True) + NORM_EPS) * gain


def _sigmoid(x):
    return 1.0 / (1.0 + jnp.exp(-x))


def _silu(x):
    h = 0.5 * x
    return h * jnp.tanh(h) + h


def _softplus(x):
    return jnp.maximum(x, 0.0) + jnp.log1p(jnp.exp(-jnp.abs(x)))


def _log_sigmoid(x):
    return jnp.minimum(x, 0.0) - jnp.log1p(jnp.exp(-jnp.abs(x)))


def _tril(n, strict=False):
    r = lax.broadcasted_iota(jnp.int32, (n, n), 0)
    c = lax.broadcasted_iota(jnp.int32, (n, n), 1)
    return (r > c) if strict else (r >= c)


def _lane_lo(shape):
    return lax.broadcasted_iota(jnp.int32, shape, len(shape) - 1) < (LANES // 2)


def _rows_as_lanes(cols, first, n):
    sel = (lax.broadcasted_iota(jnp.int32, (n, LANES), 1)
           == lax.broadcasted_iota(jnp.int32, (n, LANES), 0) + first).astype(BF16)
    hi, mid, lo = _bf16_pieces(cols)
    return _dot_nt(sel, hi) + _dot_nt(sel, mid) + _dot_nt(sel, lo)


def _params(n_grid):
    return pltpu.CompilerParams(dimension_semantics=("arbitrary",) * n_grid,
                                vmem_limit_bytes=VMEM_LIMIT)


def _resident(shape):
    nd = len(shape)
    return pl.BlockSpec(shape, lambda *_: (0,) * nd)


def _conv_tail_reset(tail, tiles_per_seq):
    @pl.when(pl.program_id(0) % tiles_per_seq == 0)
    def _():
        tail[...] = jnp.zeros_like(tail)


def _conv_chunk(p, tail, w_ref, cols, width):
    rows = p.shape[0]
    ext = jnp.concatenate([tail[:, cols], p], axis=0)
    tail[:, cols] = p[rows - SUBLANES:rows, :]
    y = None
    for k in range(width):
        start = SUBLANES - (width - 1) + k
        term = ext[start:start + rows, :] * w_ref[k:k + 1, cols]
        y = term if y is None else y + term
    return y


def _staggered(stages):
    pending = None
    for produce, consume in stages:
        value = produce()
        if pending is not None:
            pending[1](pending[0])
        pending = (value, consume)
    pending[1](pending[0])


def _inproj_even_kernel(x_ref, g_ref, w_ref, cw_ref, qkv_ref, act_ref, gate_ref, small_ref, tail,
                        *, tm, tiles_per_seq):
    W = GROUP_WIDTH
    D = GDN_HEAD_DIM
    _conv_tail_reset(tail, tiles_per_seq)
    hn = _rms(x_ref[...], g_ref[...]).astype(BF16)
    proj = lambda c0, n: (lambda: _dot(hn, w_ref[:, c0:c0 + n]))

    PW = 2 * D

    def fox_out(c0):
        def consume(p):
            qkv_ref[:, c0:c0 + PW] = p.astype(BF16)
        return consume

    def gdn_out(c0):
        def consume(p):
            a = _silu(_conv_chunk(p, tail, cw_ref, slice(c0, c0 + PW), SHORT_CONV))
            for h in range(PW // D):
                ah = a[:, h * D:(h + 1) * D]
                if c0 < 2 * W:
                    ah = ah * lax.rsqrt(jnp.sum(ah * ah, axis=-1, keepdims=True) + NORM_EPS)
                if c0 < W:
                    ah = ah * D ** -0.5
                act_ref[:, c0 + h * D:c0 + (h + 1) * D] = ah
        return consume

    def gate_out(c0):
        def consume(p):
            gate_ref[:, c0:c0 + PW] = _silu(p)
        return consume

    def small_out(p):
        small_ref[...] = p

    stages = []
    for c0 in range(0, 3 * W, PW):
        stages += [(proj(3 * W + c0, PW), gdn_out(c0)), (proj(c0, PW), fox_out(c0))]
    stages += [(proj(6 * W + c0, PW), gate_out(c0)) for c0 in range(0, W, PW)]
    stages += [(proj(7 * W, SMALL_W), small_out)]
    _staggered(stages)


def _inproj_even(x2, gain, w, conv_w, tm, tiles_per_seq):
    T = x2.shape[0]
    W = GROUP_WIDTH
    widths = (3 * W, 3 * W, W, SMALL_W)
    dtypes = (BF16, F32, F32, F32)
    return pl.pallas_call(
        functools.partial(_inproj_even_kernel, tm=tm, tiles_per_seq=tiles_per_seq),
        grid=(T // tm,),
        in_specs=[pl.BlockSpec((tm, D_MODEL), lambda i: (i, 0)),
                  _resident((1, D_MODEL)),
                  _resident(w.shape),
                  _resident(conv_w.shape)],
        out_specs=[pl.BlockSpec((tm, n), lambda i: (i, 0)) for n in widths],
        out_shape=[jax.ShapeDtypeStruct((T, n), dt) for n, dt in zip(widths, dtypes)],
        scratch_shapes=[pltpu.VMEM((SUBLANES, 3 * W), F32)],
        compiler_params=_params(1),
        name="inproj_even",
    )(x2, gain, w, conv_w)


def _inproj_odd_kernel(x_ref, g_ref, w_ref, lbl_ref, cw_ref, cb_ref, o_ref, tail,
                       *, tm, tiles_per_seq):
    W = GROUP_WIDTH
    CW = W + 2 * M2_GROUPS * M2_STATE
    _conv_tail_reset(tail, tiles_per_seq)
    hn = _rms(x_ref[...], g_ref[...]).astype(BF16)
    proj = lambda c0, n: (lambda: _dot(hn, w_ref[:, c0:c0 + n]))

    logits = lbl_ref[...]
    e = jnp.exp(logits - jnp.max(logits, axis=0, keepdims=True))
    prob = e / jnp.sum(e, axis=0, keepdims=True)
    lb = (prob[0:1, :] + prob[1:2, :]) - prob[0:1, :]

    def conv_out(c):
        def consume(p):
            cols = slice(c * W, (c + 1) * W)
            o_ref[:, cols] = _silu(_conv_chunk(p, tail, cw_ref, cols, SHORT_CONV) + cb_ref[:, cols])
        return consume

    def mapped_out(col, fn):
        def consume(p):
            o_ref[:, col:col + p.shape[1]] = fn(p)
        return consume

    def forget_out(col):
        def consume(p):
            gate = _sigmoid(p)
            o_ref[:, col:col + W] = jnp.log(lb + (1.0 - lb) * gate)
            o_ref[:, col + W:col + 2 * W] = (1.0 - lb) * (1.0 - gate)
        return consume

    keep = lambda p: p
    stages = [(proj(c * W, W), conv_out(c)) for c in range(CW // W)]
    stages += [(proj(CW, W), mapped_out(CW, _silu)),
               (proj(CW + W, W), forget_out(CW + W)),
               (proj(CW + 2 * W, W), mapped_out(CW + 3 * W, keep)),
               (proj(CW + 3 * W, W), mapped_out(CW + 4 * W, _silu)),
               (proj(CW + 4 * W, W), mapped_out(CW + 5 * W, _silu)),
               (proj(CW + 5 * W, SMALL_W), mapped_out(CW + 6 * W, keep))]
    _staggered(stages)


def _inproj_odd(x2, gain, w, lb_logits, conv_w, conv_b, tm, tiles_per_seq):
    T = x2.shape[0]
    W = GROUP_WIDTH
    CW = W + 2 * M2_GROUPS * M2_STATE
    n_out = CW + 6 * W + SMALL_W
    return pl.pallas_call(
        functools.partial(_inproj_odd_kernel, tm=tm, tiles_per_seq=tiles_per_seq),
        grid=(T // tm,),
        in_specs=[pl.BlockSpec((tm, D_MODEL), lambda i: (i, 0)),
                  _resident((1, D_MODEL)),
                  _resident(w.shape),
                  _resident(lb_logits.shape),
                  _resident(conv_w.shape),
                  _resident(conv_b.shape)],
        out_specs=pl.BlockSpec((tm, n_out), lambda i: (i, 0)),
        out_shape=jax.ShapeDtypeStruct((T, n_out), F32),
        scratch_shapes=[pltpu.VMEM((SUBLANES, CW), F32)],
        compiler_params=_params(1),
        name="inproj_odd",
    )(x2, gain, w, lb_logits, conv_w, conv_b)


def _mix_ffn_kernel(ma_ref, mb_ref, wo_ref, x_ref, gmix_ref, gpre_ref, wup_ref, cw_ref, cb_ref,
                    wdn_ref, gpost_ref, o_ref, carry, act, *, tm, fc, group):
    W = GROUP_WIDTH
    n_chunks = D_FF // fc

    @pl.when(pl.program_id(1) == 0)
    def _():
        carry[...] = jnp.zeros_like(carry)

    mix = _dot(ma_ref[...], wo_ref[0:W, :]) + _dot(mb_ref[...], wo_ref[W:2 * W, :])
    x = x_ref[...] + _rms(mix, gmix_ref[...])
    hn = _rms(x, gpre_ref[...]).astype(BF16)
    y = None
    for c in range(n_chunks):
        halves = []
        for half in range(2):
            cols = slice(half * D_FF + c * fc, half * D_FF + (c + 1) * fc)
            u = _dot(hn, wup_ref[:, cols])
            halves.append(_conv_chunk(u, carry, cw_ref, cols, FFN_CONV) + cb_ref[:, cols])
        act[:, c * fc:(c + 1) * fc] = (_silu(halves[0]) * halves[1]).astype(BF16)
        if (c + 1) % group == 0 or c + 1 == n_chunks:
            rows = slice((c // group) * group * fc, (c + 1) * fc)
            part = _dot(act[:, rows], wdn_ref[rows, :])
            y = part if y is None else y + part
    o_ref[...] = x + _rms(y, gpost_ref[...])


def _mix_ffn(mix_a, mix_b, w_out, x2, B, S, gmix, gpre, w_up, conv_w, conv_b, w_down, gpost,
             tm, fc=256):
    W = GROUP_WIDTH
    nt = S // tm
    row = lambda b, i: (b * nt + i, 0)
    return pl.pallas_call(
        functools.partial(_mix_ffn_kernel, tm=tm, fc=fc, group=6),
        grid=(B, nt),
        in_specs=[pl.BlockSpec((tm, W), row),
                  pl.BlockSpec((tm, W), row),
                  _resident((D_MODEL, D_MODEL)),
                  pl.BlockSpec((tm, D_MODEL), row),
                  _resident((1, D_MODEL)),
                  _resident((1, D_MODEL)),
                  _resident((D_MODEL, 2 * D_FF)),
                  _resident((FFN_CONV, 2 * D_FF)),
                  _resident((1, 2 * D_FF)),
                  _resident((D_FF, D_MODEL)),
                  _resident((1, D_MODEL))],
        out_specs=pl.BlockSpec((tm, D_MODEL), row),
        out_shape=jax.ShapeDtypeStruct(x2.shape, F32),
        scratch_shapes=[pltpu.VMEM((SUBLANES, 2 * D_FF), F32),
                        pltpu.VMEM((tm, D_FF), BF16)],
        compiler_params=_params(2),
        name="mixffn",
    )(mix_a, mix_b, w_out, x2, gmix, gpre, w_up, conv_w, conv_b, w_down, gpost)


def _foxgate_kernel(s_ref, b_ref, qk_ref, qt_ref, ka_ref, vt_ref, carry, *, blk):
    W = GROUP_WIDTH

    @pl.when(pl.program_id(1) == 0)
    def _():
        carry[...] = jnp.zeros_like(carry)

    z = s_ref[...] + b_ref[...]
    cs = _exact_dot(_tril(blk).astype(BF16), _log_sigmoid(z)) + carry[...]
    carry[...] = cs[blk - 1:blk, :]
    lane = lax.broadcasted_iota(jnp.int32, (1, LANES), 1)
    lo = lane < FOX_HEAD_DIM
    ones = jnp.where(lane < FOX_HEAD_DIM + FOX_AUG, 1.0, 0.0)
    scale = FOX_HEAD_DIM ** -0.5
    for p in range(FOX_HEADS // 2):
        q = qk_ref[:, p * LANES:(p + 1) * LANES].astype(F32) * scale
        k = qk_ref[:, W + p * LANES:W + (p + 1) * LANES].astype(F32)
        for e in range(2):
            h = 2 * p + e
            qe = pltpu.roll(q, FOX_HEAD_DIM, 1) if e else q
            ke = pltpu.roll(k, FOX_HEAD_DIM, 1) if e else k
            rem = -cs[:, h:h + 1]
            aug = jnp.zeros((blk, LANES), F32)
            for piece in range(FOX_AUG):
                part = rem.astype(BF16).astype(F32)
                aug = jnp.where(lane == FOX_HEAD_DIM + piece, part, aug)
                rem = rem - part
            qt_ref[0, 0, h * LANES:(h + 1) * LANES, :] = jnp.where(lo, qe, ones).T.astype(BF16)
            ka_ref[:, h * LANES:(h + 1) * LANES] = jnp.where(lo, ke, aug).astype(BF16)
        v = qk_ref[:, 2 * W + p * LANES:2 * W + (p + 1) * LANES].astype(F32)
        vt_ref[0, 0, p * LANES:(p + 1) * LANES, :] = v.T.astype(BF16)


def _foxgate(qkv, small, B, S, small_blk, bias_row, blk):
    T = B * S
    W = GROUP_WIDTH
    nb = S // blk
    wide = FOX_HEADS * LANES
    row = lambda b, j: b * nb + j
    return pl.pallas_call(
        functools.partial(_foxgate_kernel, blk=blk),
        grid=(B, nb),
        in_specs=[pl.BlockSpec((blk, SMALL_W), lambda b, j: (row(b, j), small_blk)),
                  _resident((1, SMALL_W)),
                  pl.BlockSpec((blk, 3 * W), lambda b, j: (row(b, j), 0))],
        out_specs=[pl.BlockSpec((1, 1, wide, blk), lambda b, j: (b, j, 0, 0)),
                   pl.BlockSpec((blk, wide), lambda b, j: (row(b, j), 0)),
                   pl.BlockSpec((1, 1, W, blk), lambda b, j: (b, j, 0, 0))],
        out_shape=[jax.ShapeDtypeStruct((B, nb, wide, blk), BF16),
                   jax.ShapeDtypeStruct((T, wide), BF16),
                   jax.ShapeDtypeStruct((B, nb, W, blk), BF16)],
        scratch_shapes=[pltpu.VMEM((1, SMALL_W), F32)],
        compiler_params=_params(2),
        name="foxgate",
    )(small, bias_row, qkv)


def _fox_kernel(qt_ref, k_ref, vt_ref, o_ref, m_ref, l_ref, acc_ref, *, blk, kvb):
    i = pl.program_id(1)
    kv_id = lax.broadcasted_iota(jnp.int32, (kvb, blk), 0)
    q_id = lax.broadcasted_iota(jnp.int32, (kvb, blk), 1)
    first_head = lax.broadcasted_iota(jnp.int32, (LANES, 1), 0) < FOX_HEAD_DIM
    m_ref[...] = jnp.full(m_ref.shape, NEG_INF, F32)
    l_ref[...] = jnp.zeros_like(l_ref)
    acc_ref[...] = jnp.zeros_like(acc_ref)

    def sub_block(j, sub, masked):
        off = pl.multiple_of(j * blk + sub * kvb, kvb)
        for p in range(FOX_HEADS // 2):
            vt = vt_ref[0, j, p * LANES:(p + 1) * LANES, sub * kvb:(sub + 1) * kvb]
            alpha, pv = [], []
            for e in range(2):
                h = 2 * p + e
                hs = slice(h * LANES, (h + 1) * LANES)
                s = _dot(k_ref[pl.ds(off, kvb), hs], qt_ref[0, 0, hs, :])
                if masked:
                    s = jnp.where(kv_id + sub * kvb <= q_id, s, NEG_INF)
                m_old = m_ref[h:h + 1, :]
                m_new = jnp.maximum(m_old, jnp.max(s, axis=0, keepdims=True))
                pe = jnp.exp(s - m_new)
                a = jnp.exp(m_old - m_new)
                m_ref[h:h + 1, :] = m_new
                l_ref[h:h + 1, :] = a * l_ref[h:h + 1, :] + jnp.sum(pe, axis=0, keepdims=True)
                alpha.append(a)
                pv.append(_dot(vt, pe.astype(BF16)))
            acc_ref[p] = (jnp.where(first_head, alpha[0], alpha[1]) * acc_ref[p]
                          + jnp.where(first_head, pv[0], pv[1]))

    def block(j, masked):
        for sub in range(blk // kvb):
            sub_block(j, sub, masked)

    def body(j, carry):
        block(j, False)
        return carry

    lax.fori_loop(0, i, body, 0)
    block(i, True)
    for p in range(FOX_HEADS // 2):
        l = jnp.where(first_head, l_ref[2 * p:2 * p + 1, :], l_ref[2 * p + 1:2 * p + 2, :])
        o_ref[:, p * LANES:(p + 1) * LANES] = (acc_ref[p] / l).T.astype(BF16)


def _fox(qt, ka, vt, B, S, blk):
    T = B * S
    nq = S // blk
    W = GROUP_WIDTH
    wide = FOX_HEADS * LANES
    return pl.pallas_call(
        functools.partial(_fox_kernel, blk=blk, kvb=128),
        grid=(B, nq),
        in_specs=[pl.BlockSpec((1, 1, wide, blk), lambda b, i: (b, i, 0, 0)),
                  pl.BlockSpec((S, wide), lambda b, i: (b, 0)),
                  pl.BlockSpec((1, nq, W, blk), lambda b, i: (b, 0, 0, 0))],
        out_specs=pl.BlockSpec((blk, W), lambda b, i: (b * nq + i, 0)),
        out_shape=jax.ShapeDtypeStruct((T, W), BF16),
        scratch_shapes=[pltpu.VMEM((FOX_HEADS, blk), F32),
                        pltpu.VMEM((FOX_HEADS, blk), F32),
                        pltpu.VMEM((FOX_HEADS // 2, LANES, blk), F32)],
        compiler_params=_params(2),
        name="fox",
    )(qt, ka, vt)


def _unit_lower_solves(ms, rhss, n, nilpotent):
    eye = (lax.broadcasted_iota(jnp.int32, (n, n), 0)
           == lax.broadcasted_iota(jnp.int32, (n, n), 1)).astype(F32)
    xs = [-m for m in ms]
    sols = rhss
    power = 1
    while power < nilpotent:
        x16 = [x.astype(BF16) for x in xs]
        x2 = [_dot(x, x) for x in x16]
        x2_16 = [x.astype(BF16) for x in x2]
        factors = [eye + x + y + _dot(xb, yb) for x, y, xb, yb in zip(xs, x2, x16, x2_16)]
        sols = [_dot(f.astype(BF16), r.astype(BF16)) for f, r in zip(factors, sols)]
        power *= 4
        if power < nilpotent:
            xs = [_dot(y, y) for y in x2_16]
    assert power == nilpotent
    return sols


def _gdn_kernel(act_ref, gate_ref, small_ref, alog_ref, dtb_ref, gain_ref, o_ref,
                state, sol_ref, qd_ref, kd_ref, aqk_ref, ks_ref, ubuf, *, a_lane, b_lane, rows):
    C = GDN_CHUNK
    P = 2 * C
    W = GROUP_WIDTH
    D = GDN_HEAD_DIM
    n_chunks = rows // C

    @pl.when(pl.program_id(1) == 0)
    def _():
        state[...] = jnp.zeros_like(state)

    small = small_ref[...]
    g_all = -jnp.exp(alog_ref[...]) * _softplus(small + dtb_ref[...])
    beta_all = _sigmoid(small)
    r_id = lax.broadcasted_iota(jnp.int32, (P, P), 0)
    c_id = lax.broadcasted_iota(jnp.int32, (P, P), 1)
    tri_bd = (c_id >= r_id - (r_id & (C - 1))) & (c_id <= r_id)
    diag = r_id == c_id
    tri16 = tri_bd.astype(BF16)
    g_tot, ms, rhss, where = [], [], [], []
    for pr in range(rows // P):
        ps = slice(pr * P, (pr + 1) * P)
        G_all = _exact_dot(tri16, g_all[ps, :])
        G_rows = _rows_as_lanes(G_all, a_lane, SUBLANES)
        G_tot = jnp.concatenate(
            [jnp.broadcast_to(G_all[(c + 1) * C - 1:(c + 1) * C, :], (C, SMALL_W)) for c in range(2)],
            axis=0)
        g_tot.append(G_tot)
        for h in range(GDN_HEADS):
            q = act_ref[ps, h * D:(h + 1) * D]
            k = act_ref[ps, W + h * D:W + (h + 1) * D]
            v = act_ref[ps, 2 * W + h * D:2 * W + (h + 1) * D]
            Gc = G_all[:, a_lane + h:a_lane + h + 1]
            Gr = G_rows[h:h + 1, :]
            Gt = G_tot[:, a_lane + h:a_lane + h + 1]
            beta = beta_all[ps, b_lane + h:b_lane + h + 1]
            gamma = jnp.exp(jnp.where(tri_bd, Gc - Gr, NEG_INF))
            kb = k * beta
            kb16, k16 = kb.astype(BF16), k.astype(BF16)
            ms.append(jnp.where(diag, 0.0, _dot_nt(kb16, k16) * gamma))
            eG = jnp.exp(Gc)
            rhss.append(jnp.concatenate([v * beta, kb * eG], axis=1))
            where.append((h, ps))
            aqk_ref[h, ps, :] = (_dot_nt(q.astype(BF16), k16) * gamma).astype(BF16)
            qd_ref[h, ps, :] = (q * eG).astype(BF16)
            kd_ref[h, ps, :] = (k * jnp.exp(Gt - Gc)).astype(BF16)
    for (h, ps), sol in zip(where, _unit_lower_solves(ms, rhss, P, C)):
        sol_ref[h, ps, :] = sol
    for c in range(n_chunks):
        rs = slice(c * C, (c + 1) * C)
        for h in range(GDN_HEADS):
            ks_ref[h, c] = _dot_tn(kd_ref[h, rs, :], sol_ref[h, rs, :].astype(BF16))

    states = [state[h] for h in range(GDN_HEADS)]
    for c in range(n_chunks):
        rs = slice(c * C, (c + 1) * C)
        first = c % 2 == 0
        for h in range(GDN_HEADS):
            sl = slice(h * D, (h + 1) * D)
            St = states[h]
            St16 = St.astype(BF16)
            decay = jnp.exp(g_tot[c // 2][(c % 2) * C:(c % 2) * C + 1, a_lane + h:a_lane + h + 1])
            states[h] = (decay * St + ks_ref[h, c, :, 0:D]
                         - _dot(ks_ref[h, c, :, D:2 * D].astype(BF16), St16))
            u = sol_ref[h, rs, 0:D] - _dot(sol_ref[h, rs, D:2 * D].astype(BF16), St16)
            u16 = u.astype(BF16)
            o = _dot(qd_ref[h, rs, :], St16)
            if first:
                ubuf[h] = jnp.concatenate([u16, jnp.zeros_like(u16)], axis=0)
                o = o + _dot(aqk_ref[h, rs, 0:C], u16)
            else:
                ubuf[h, C:P, :] = u16
                o = o + _dot(aqk_ref[h, rs, :], ubuf[h])
            o = _rms(o, gain_ref[...]) * gate_ref[rs, sl]
            o_ref[rs, sl] = o.astype(BF16)
    for h in range(GDN_HEADS):
        state[h] = states[h]


def _gdn(act, gate, small, B, S, alog_row, dtb_row, gain_row, a_lane, b_lane, rows=512):
    T = B * S
    W = GROUP_WIDTH
    H, D = GDN_HEADS, GDN_HEAD_DIM
    P = 2 * GDN_CHUNK
    nt = S // rows
    row = lambda b, i: (b * nt + i, 0)
    return pl.pallas_call(
        functools.partial(_gdn_kernel, a_lane=a_lane, b_lane=b_lane, rows=rows),
        grid=(B, nt),
        in_specs=[pl.BlockSpec((rows, 3 * W), row),
                  pl.BlockSpec((rows, W), row),
                  pl.BlockSpec((rows, SMALL_W), row),
                  _resident((1, SMALL_W)),
                  _resident((1, SMALL_W)),
                  _resident((1, D))],
        out_specs=pl.BlockSpec((rows, W), row),
        out_shape=jax.ShapeDtypeStruct((T, W), BF16),
        scratch_shapes=[pltpu.VMEM((H, D, D), F32),
                        pltpu.VMEM((H, rows, 2 * D), F32),
                        pltpu.VMEM((H, rows, D), BF16),
                        pltpu.VMEM((H, rows, D), BF16),
                        pltpu.VMEM((H, rows, P), BF16),
                        pltpu.VMEM((H, rows // GDN_CHUNK, D, 2 * D), F32),
                        pltpu.VMEM((H, P, D), BF16)],
        compiler_params=_params(2),
        name="gdn",
    )(act, gate, small, alog_row, dtb_row, gain_row)


def _block_ref_rows(x, half):
    R, L = x.shape
    if half >= SUBLANES:
        xb = x.reshape(R // (2 * half), 2 * half, L)
        return jnp.broadcast_to(xb[:, half:half + 1, :], xb.shape).reshape(R, L)
    xb = x.reshape(R // SUBLANES, SUBLANES, L)
    sub = lax.broadcasted_iota(jnp.int32, (1, SUBLANES, 1), 1)
    out = None
    for start in range(SUBLANES - 2 * half, -1, -2 * half):
        row = jnp.broadcast_to(xb[:, start + half:start + half + 1, :], xb.shape)
        out = row if out is None else jnp.where(sub < start + 2 * half, row, out)
    return out.reshape(R, L)


def _hgrn_kernel(q_ref, f_ref, k_ref, i_ref, g_ref, gain_ref, o_ref, state, *, rows):
    C = HGRN_CHUNK
    P = 2 * C
    D = HGRN_HEAD_DIM
    halves = [1 << b for b in range(C.bit_length() - 1)]

    @pl.when(pl.program_id(1) == 0)
    def _():
        state[...] = jnp.zeros_like(state)

    r_id = lax.broadcasted_iota(jnp.int32, (P, P), 0)
    c_id = lax.broadcasted_iota(jnp.int32, (P, P), 1)
    tri16 = ((c_id >= r_id - (r_id & (C - 1))) & (c_id <= r_id)).astype(BF16)
    differ = r_id ^ c_id
    row_id = lax.broadcasted_iota(jnp.int32, (P, 1), 0)
    gain = gain_ref[...]
    for pr in range(rows // P):
        ps = slice(pr * P, (pr + 1) * P)
        G_all = _exact_dot(tri16, f_ref[ps, :])
        decays = [jnp.exp(-jnp.abs(G_all - _block_ref_rows(G_all, half))) for half in halves]
        heads = [slice(h * D, (h + 1) * D) for h in range(HGRN_HEADS)]
        scores = [jnp.zeros((P, P), F32) for _ in heads]
        for half, dec in reversed(list(zip(halves, decays))):
            upper = (row_id & half) != 0
            for h, sl in enumerate(heads):
                z16 = (jnp.where(upper, q_ref[ps, sl], k_ref[ps, sl]) * dec[:, sl]).astype(BF16)
                scores[h] = jnp.where(differ < 2 * half, _dot_nt(z16, z16), scores[h])
        for h, sl in enumerate(heads):
            q, k, G = q_ref[ps, sl], k_ref[ps, sl], G_all[:, sl]
            v16 = i_ref[ps, sl].astype(BF16)
            a = jnp.where(differ == 0, _dot_nt(q.astype(BF16), k.astype(BF16)), scores[h])
            a = jnp.where(r_id >= c_id, a, 0.0)
            o_intra = _dot(a.astype(BF16), v16)
            q_in = (q * jnp.exp(G)).astype(BF16)
            for c in range(2):
                cs = slice(c * C, (c + 1) * C)
                G_last = G[(c + 1) * C - 1:(c + 1) * C, :]
                St = state[h]
                o = _dot_nt(q_in[cs, :], St.astype(BF16)) + o_intra[cs, :]
                k_out = (k[cs, :] * jnp.exp(G_last - G[cs, :])).astype(BF16)
                state[h] = jnp.exp(G_last) * St + _dot_tn(v16[cs, :], k_out)
                rs = slice(pr * P + c * C, pr * P + (c + 1) * C)
                o_ref[rs, sl] = (_rms(o, gain) * g_ref[rs, sl]).astype(BF16)


def _hgrn(proj, first_blk, B, S, gain_row, rows=256):
    T = B * S
    W = GROUP_WIDTH
    nt = S // rows
    spec = lambda j: pl.BlockSpec((rows, W), lambda b, i: (b * nt + i, j))
    return pl.pallas_call(
        functools.partial(_hgrn_kernel, rows=rows),
        grid=(B, nt),
        in_specs=[spec(first_blk + j) for j in range(5)] + [_resident((1, HGRN_HEAD_DIM))],
        out_specs=spec(0),
        out_shape=jax.ShapeDtypeStruct((T, W), BF16),
        scratch_shapes=[pltpu.VMEM((HGRN_HEADS, HGRN_HEAD_DIM, HGRN_HEAD_DIM), F32)],
        compiler_params=_params(2),
        name="hgrn2",
    )(proj, proj, proj, proj, proj, gain_row)


def _ssd_kernel(xbc_ref, z_ref, small_ref, alog_ref, dtb_ref, dvec_ref, gain_ref,
                o_ref, state, *, rows):
    L = M2_CHUNK
    W = GROUP_WIDTH
    N = M2_STATE
    pairs_per_group = M2_HEADS // M2_GROUPS // 2

    @pl.when(pl.program_id(1) == 0)
    def _():
        state[...] = jnp.zeros_like(state)

    dt_all = _softplus(small_ref[...] + dtb_ref[...])
    tri = _tril(L)
    tri16 = tri.astype(BF16)
    neg_a = -jnp.exp(alog_ref[...])
    lo = _lane_lo((1, LANES))
    for ck in range(rows // L):
        rs = slice(ck * L, (ck + 1) * L)
        xbc = xbc_ref[rs, :]
        dt = dt_all[rs, :]
        A_cs = _exact_dot(tri16, dt * neg_a)
        A_rows = _rows_as_lanes(A_cs, 0, M2_HEADS)
        for g in range(M2_GROUPS):
            Bg = xbc[:, W + g * N:W + (g + 1) * N]
            Cg = xbc[:, W + M2_GROUPS * N + g * N:W + M2_GROUPS * N + (g + 1) * N]
            cb = _dot_nt(Cg.astype(BF16), Bg.astype(BF16))
            ys = []
            for pp in range(pairs_per_group):
                p = g * pairs_per_group + pp
                xs = xbc[:, p * LANES:(p + 1) * LANES]
                dtl = jnp.where(lo, dt[:, 2 * p:2 * p + 1], dt[:, 2 * p + 1:2 * p + 2])
                X16 = (xs * dtl).astype(BF16)
                St = state[p]
                rhs = jnp.concatenate([X16, St.astype(BF16)], axis=0)
                y_h, st_h = [], []
                for e in range(2):
                    h = 2 * p + e
                    ac = jnp.broadcast_to(A_cs[:, h:h + 1], (L, LANES))
                    ar = A_rows[h:h + 1, :]
                    decay = jnp.exp(jnp.where(tri, ac - ar, NEG_INF))
                    lhs = jnp.concatenate([(cb * decay).astype(BF16),
                                           (Cg * jnp.exp(ac)).astype(BF16)], axis=1)
                    y_h.append(_dot(lhs, rhs))
                    last = ac[L - 1:L, :]
                    st_h.append(jnp.exp(last) * St
                                + _dot_tn((Bg * jnp.exp(last - ac)).astype(BF16), X16))
                state[p] = jnp.where(lo, st_h[0], st_h[1])
                ys.append(jnp.where(lo, y_h[0], y_h[1]) + dvec_ref[:, p * LANES:(p + 1) * LANES] * xs)
            gs = slice(g * (W // M2_GROUPS), (g + 1) * (W // M2_GROUPS))
            y = jnp.concatenate(ys, axis=1) * z_ref[rs, gs]
            o_ref[rs, gs] = _rms(y, gain_ref[:, gs]).astype(BF16)


def _ssd(proj, z_blk, small_blk, B, S, alog_row, dtb_row, dvec, gain_row, rows=256):
    T = B * S
    W = GROUP_WIDTH
    CW = W + 2 * M2_GROUPS * M2_STATE
    nt = S // rows
    row = lambda b, i: b * nt + i
    return pl.pallas_call(
        functools.partial(_ssd_kernel, rows=rows),
        grid=(B, nt),
        in_specs=[pl.BlockSpec((rows, CW), lambda b, i: (row(b, i), 0)),
                  pl.BlockSpec((rows, W), lambda b, i: (row(b, i), z_blk)),
                  pl.BlockSpec((rows, SMALL_W), lambda b, i: (row(b, i), small_blk)),
                  _resident((1, SMALL_W)),
                  _resident((1, SMALL_W)),
                  _resident((1, W)),
                  _resident((1, W))],
        out_specs=pl.BlockSpec((rows, W), lambda b, i: (row(b, i), 0)),
        out_shape=jax.ShapeDtypeStruct((T, W), BF16),
        scratch_shapes=[pltpu.VMEM((M2_HEADS // 2, M2_STATE, LANES), F32)],
        compiler_params=_params(2),
        name="ssd",
    )(proj, proj, proj, alog_row, dtb_row, dvec, gain_row)


def _pad_lanes(v, first, width=SMALL_W):
    v = v.astype(F32)
    return jnp.pad(v, (first, width - first - v.shape[0])).reshape(1, width)


def kernel(x, norm_gains, w_out, ffn_w_up, ffn_conv_w, ffn_conv_b, ffn_w_down,
           even_w_in, fox_f_bias, gdn_conv_w, gdn_A_log, gdn_dt_bias, gdn_norm_gain,
           odd_w_in, hgrn_lb_logits, hgrn_norm_gain, m2_conv_w, m2_conv_b,
           m2_A_log, m2_dt_bias, m2_D, m2_norm_gain):
    B, S, D = x.shape
    assert D == D_MODEL and S % 512 == 0
    T = B * S
    W = GROUP_WIDTH
    tm = 512
    tm_in = 256
    tm_ffn = 512
    row = lambda v: v.astype(F32).reshape(1, -1)
    x2 = x.reshape(T, D).astype(F32)

    o_ff = 3 * W
    o_qkv = o_ff + FOX_HEADS
    o_a = o_qkv + 3 * W
    o_b = o_a + GDN_HEADS
    o_gate = o_b + GDN_HEADS
    a_lane, b_lane = FOX_HEADS, FOX_HEADS + GDN_HEADS
    pad = jnp.zeros((D, SMALL_W - FOX_HEADS - 2 * GDN_HEADS), BF16)
    w16 = even_w_in.astype(BF16)
    w_even = jnp.concatenate([w16[:, :o_ff], w16[:, o_qkv:o_a], w16[:, o_gate:],
                              w16[:, o_ff:o_qkv], w16[:, o_a:o_gate], pad], axis=1)
    g = norm_gains[0]
    qkv, gdn_act, gdn_gate, small = _inproj_even(x2, row(g[0]), w_even, gdn_conv_w.astype(F32),
                                                 tm_in, S // tm_in)
    fox_blk = 256
    qt, ka, vt = _foxgate(qkv, small, B, S, 0, _pad_lanes(fox_f_bias, 0), fox_blk)
    o_fox = _fox(qt, ka, vt, B, S, fox_blk)
    o_gdn = _gdn(gdn_act, gdn_gate, small, B, S, _pad_lanes(gdn_A_log, a_lane),
                 _pad_lanes(gdn_dt_bias, a_lane), row(gdn_norm_gain), a_lane, b_lane)
    x2 = _mix_ffn(o_fox, o_gdn, w_out[0].astype(BF16), x2, B, S, row(g[1]), row(g[2]),
                  ffn_w_up[0].astype(BF16), ffn_conv_w[0].astype(F32), row(ffn_conv_b[0]),
                  ffn_w_down[0].astype(BF16), row(g[3]), tm_ffn)

    assert hgrn_lb_logits.shape == (2, W)
    CW = W + 2 * M2_GROUPS * M2_STATE
    pad = jnp.zeros((D, SMALL_W - M2_HEADS), BF16)
    w16 = odd_w_in.astype(BF16)
    w_odd = jnp.concatenate([w16[:, 5 * W:5 * W + CW], w16[:, :5 * W], w16[:, 5 * W + CW:], pad],
                            axis=1)
    g = norm_gains[1]
    proj = _inproj_odd(x2, row(g[0]), w_odd, hgrn_lb_logits.astype(F32), m2_conv_w.astype(F32),
                       row(m2_conv_b), tm, S // tm)
    o_hgrn = _hgrn(proj, CW // W, B, S, row(hgrn_norm_gain))
    o_ssd = _ssd(proj, (CW + 5 * W) // W, (CW + 6 * W) // SMALL_W, B, S, _pad_lanes(m2_A_log, 0),
                 _pad_lanes(m2_dt_bias, 0), row(jnp.repeat(m2_D, M2_HEAD_DIM)), row(m2_norm_gain))
    x2 = _mix_ffn(o_hgrn, o_ssd, w_out[1].astype(BF16), x2, B, S, row(g[1]), row(g[2]),
                  ffn_w_up[1].astype(BF16), ffn_conv_w[1].astype(F32), row(ffn_conv_b[1]),
                  ffn_w_down[1].astype(BF16), row(g[3]), tm_ffn)
    return x2.reshape(B, S, D).astype(x.dtype)
```

```python
import functools

import jax
import jax.numpy as jnp
from jax import lax
from jax.experimental import pallas as pl
from jax.experimental.pallas import tpu as pltpu

F32 = jnp.float32
BF16 = jnp.bfloat16

D_MODEL = 1024
GROUP_WIDTH = D_MODEL // 2
FOX_HEAD_DIM = 64
FOX_HEADS = GROUP_WIDTH // FOX_HEAD_DIM
FOX_AUG = 3
GDN_HEAD_DIM = 128
GDN_HEADS = GROUP_WIDTH // GDN_HEAD_DIM
GDN_CHUNK = 64
SHORT_CONV = 4
HGRN_HEAD_DIM = 128
HGRN_HEADS = GROUP_WIDTH // HGRN_HEAD_DIM
HGRN_CHUNK = 64
M2_HEAD_DIM = 64
M2_HEADS = GROUP_WIDTH // M2_HEAD_DIM
M2_GROUPS = 2
M2_STATE = 128
M2_CHUNK = 128
D_FF = 2816
FFN_CONV = 3
NORM_EPS = 1e-6

LANES = 128
SUBLANES = 8
SMALL_W = LANES
VMEM_LIMIT = 56 * 1024 * 1024

NEG_INF = float("-inf")


def _dot(a, b):
    return jnp.dot(a, b, preferred_element_type=F32)


def _dot_nt(a, b):
    return lax.dot_general(a, b, (((1,), (1,)), ((), ())), preferred_element_type=F32)


def _dot_tn(a, b):
    return lax.dot_general(a, b, (((0,), (0,)), ((), ())), preferred_element_type=F32)


def _bf16_pieces(x):
    hi = x.astype(BF16)
    rest = x - hi.astype(F32)
    mid = rest.astype(BF16)
    return hi, mid, (rest - mid.astype(F32)).astype(BF16)


def _exact_dot(sel16, x):
    hi, mid, lo = _bf16_pieces(x)
    return _dot(sel16, hi) + _dot(sel16, mid) + _dot(sel16, lo)


def _rms(x, gain):
    return x * lax.rsqrt(jnp.mean(x * x, axis=-1, keepdims=True) + NORM_EPS) * gain


def _sigmoid(x):
    return 1.0 / (1.0 + jnp.exp(-x))


def _silu(x):
    h = 0.5 * x
    return h * jnp.tanh(h) + h


def _softplus(x):
    return jnp.maximum(x, 0.0) + jnp.log1p(jnp.exp(-jnp.abs(x)))


def _log_sigmoid(x):
    return jnp.minimum(x, 0.0) - jnp.log1p(jnp.exp(-jnp.abs(x)))


def _tril(n, strict=False):
    r = lax.broadcasted_iota(jnp.int32, (n, n), 0)
    c = lax.broadcasted_iota(jnp.int32, (n, n), 1)
    return (r > c) if strict else (r >= c)


def _lane_lo(shape):
    return lax.broadcasted_iota(jnp.int32, shape, len(shape) - 1) < (LANES // 2)


def _rows_as_lanes(cols, first, n):
    sel = (lax.broadcasted_iota(jnp.int32, (n, LANES), 1)
           == lax.broadcasted_iota(jnp.int32, (n, LANES), 0) + first).astype(BF16)
    hi, mid, lo = _bf16_pieces(cols)
    return _dot_nt(sel, hi) + _dot_nt(sel, mid) + _dot_nt(sel, lo)


def _params(n_grid):
    return pltpu.CompilerParams(dimension_semantics=("arbitrary",) * n_grid,
                                vmem_limit_bytes=VMEM_LIMIT)


def _resident(shape):
    nd = len(shape)
    return pl.BlockSpec(shape, lambda *_: (0,) * nd)


def _conv_tail_reset(tail, tiles_per_seq):
    @pl.when(pl.program_id(0) % tiles_per_seq == 0)
    def _():
        tail[...] = jnp.zeros_like(tail)


def _conv_chunk(p, tail, w_ref, cols, width):
    rows = p.shape[0]
    ext = jnp.concatenate([tail[:, cols], p], axis=0)
    tail[:, cols] = p[rows - SUBLANES:rows, :]
    y = None
    for k in range(width):
        start = SUBLANES - (width - 1) + k
        term = ext[start:start + rows, :] * w_ref[k:k + 1, cols]
        y = term if y is None else y + term
    return y


def _staggered(stages, lag=2):
    pending = []
    for produce, consume in stages:
        pending.append((produce(), consume))
        if len(pending) > lag:
            value, done = pending.pop(0)
            done(value)
    for value, done in pending:
        done(value)


def _inproj_even_kernel(x_ref, g_ref, w_ref, cw_ref, qkv_ref, act_ref, gate_ref, small_ref, tail,
                        *, tm, tiles_per_seq):
    W = GROUP_WIDTH
    D = GDN_HEAD_DIM
    _conv_tail_reset(tail, tiles_per_seq)
    hn = _rms(x_ref[...], g_ref[...]).astype(BF16)
    proj = lambda c0, n: (lambda: _dot(hn, w_ref[:, c0:c0 + n]))

    PW = 2 * D

    def fox_out(c0):
        def consume(p):
            qkv_ref[:, c0:c0 + PW] = p.astype(BF16)
        return consume

    def gdn_out(c0):
        def consume(p):
            a = _silu(_conv_chunk(p, tail, cw_ref, slice(c0, c0 + PW), SHORT_CONV))
            for h in range(PW // D):
                ah = a[:, h * D:(h + 1) * D]
                if c0 < 2 * W:
                    ah = ah * lax.rsqrt(jnp.sum(ah * ah, axis=-1, keepdims=True) + NORM_EPS)
                if c0 < W:
                    ah = ah * D ** -0.5
                act_ref[:, c0 + h * D:c0 + (h + 1) * D] = ah
        return consume

    def gate_out(c0):
        def consume(p):
            gate_ref[:, c0:c0 + PW] = _silu(p)
        return consume

    def small_out(p):
        small_ref[...] = p

    stages = []
    for c0 in range(0, 3 * W, PW):
        stages += [(proj(3 * W + c0, PW), gdn_out(c0)), (proj(c0, PW), fox_out(c0))]
    stages += [(proj(6 * W + c0, PW), gate_out(c0)) for c0 in range(0, W, PW)]
    stages += [(proj(7 * W, SMALL_W), small_out)]
    _staggered(stages)


def _inproj_even(x2, gain, w, conv_w, tm, tiles_per_seq):
    T = x2.shape[0]
    W = GROUP_WIDTH
    widths = (3 * W, 3 * W, W, SMALL_W)
    dtypes = (BF16, F32, F32, F32)
    return pl.pallas_call(
        functools.partial(_inproj_even_kernel, tm=tm, tiles_per_seq=tiles_per_seq),
        grid=(T // tm,),
        in_specs=[pl.BlockSpec((tm, D_MODEL), lambda i: (i, 0)),
                  _resident((1, D_MODEL)),
                  _resident(w.shape),
                  _resident(conv_w.shape)],
        out_specs=[pl.BlockSpec((tm, n), lambda i: (i, 0)) for n in widths],
        out_shape=[jax.ShapeDtypeStruct((T, n), dt) for n, dt in zip(widths, dtypes)],
        scratch_shapes=[pltpu.VMEM((SUBLANES, 3 * W), F32)],
        compiler_params=_params(1),
        name="inproj_even",
    )(x2, gain, w, conv_w)


def _inproj_odd_kernel(x_ref, g_ref, w_ref, lbl_ref, cw_ref, cb_ref, o_ref, tail,
                       *, tm, tiles_per_seq):
    W = GROUP_WIDTH
    CW = W + 2 * M2_GROUPS * M2_STATE
    _conv_tail_reset(tail, tiles_per_seq)
    hn = _rms(x_ref[...], g_ref[...]).astype(BF16)
    proj = lambda c0, n: (lambda: _dot(hn, w_ref[:, c0:c0 + n]))

    logits = lbl_ref[...]
    e = jnp.exp(logits - jnp.max(logits, axis=0, keepdims=True))
    prob = e / jnp.sum(e, axis=0, keepdims=True)
    lb = (prob[0:1, :] + prob[1:2, :]) - prob[0:1, :]

    def conv_out(c):
        def consume(p):
            cols = slice(c * W, (c + 1) * W)
            o_ref[:, cols] = _silu(_conv_chunk(p, tail, cw_ref, cols, SHORT_CONV) + cb_ref[:, cols])
        return consume

    def mapped_out(col, fn):
        def consume(p):
            o_ref[:, col:col + p.shape[1]] = fn(p)
        return consume

    def forget_out(col):
        def consume(p):
            gate = _sigmoid(p)
            o_ref[:, col:col + W] = jnp.log(lb + (1.0 - lb) * gate)
            o_ref[:, col + W:col + 2 * W] = (1.0 - lb) * (1.0 - gate)
        return consume

    keep = lambda p: p
    stages = [(proj(0, W), conv_out(0)),
              (proj(CW + 2 * W, W), mapped_out(CW + 3 * W, keep)),
              (proj(W, W), conv_out(1)),
              (proj(CW, W), mapped_out(CW, _silu)),
              (proj(CW + W, W), forget_out(CW + W)),
              (proj(CW + 3 * W, W), mapped_out(CW + 4 * W, _silu)),
              (proj(CW + 4 * W, W), mapped_out(CW + 5 * W, _silu)),
              (proj(CW + 5 * W, SMALL_W), mapped_out(CW + 6 * W, keep))]
    _staggered(stages)


def _inproj_odd(x2, gain, w, lb_logits, conv_w, conv_b, tm, tiles_per_seq):
    T = x2.shape[0]
    W = GROUP_WIDTH
    CW = W + 2 * M2_GROUPS * M2_STATE
    n_out = CW + 6 * W + SMALL_W
    return pl.pallas_call(
        functools.partial(_inproj_odd_kernel, tm=tm, tiles_per_seq=tiles_per_seq),
        grid=(T // tm,),
        in_specs=[pl.BlockSpec((tm, D_MODEL), lambda i: (i, 0)),
                  _resident((1, D_MODEL)),
                  _resident(w.shape),
                  _resident(lb_logits.shape),
                  _resident(conv_w.shape),
                  _resident(conv_b.shape)],
        out_specs=pl.BlockSpec((tm, n_out), lambda i: (i, 0)),
        out_shape=jax.ShapeDtypeStruct((T, n_out), F32),
        scratch_shapes=[pltpu.VMEM((SUBLANES, CW), F32)],
        compiler_params=_params(1),
        name="inproj_odd",
    )(x2, gain, w, lb_logits, conv_w, conv_b)


def _mix_ffn_kernel(ma_ref, mb_ref, wo_ref, x_ref, gmix_ref, gpre_ref, wup_ref, cw_ref, cb_ref,
                    wdn_ref, gpost_ref, o_ref, carry, act, *, tm, fc, group):
    W = GROUP_WIDTH
    n_chunks = D_FF // fc

    @pl.when(pl.program_id(1) == 0)
    def _():
        carry[...] = jnp.zeros_like(carry)

    mix = _dot(ma_ref[...], wo_ref[0:W, :]) + _dot(mb_ref[...], wo_ref[W:2 * W, :])
    x = x_ref[...] + _rms(mix, gmix_ref[...])
    hn = _rms(x, gpre_ref[...]).astype(BF16)
    y = None
    for c in range(n_chunks):
        halves = []
        for half in range(2):
            cols = slice(half * D_FF + c * fc, half * D_FF + (c + 1) * fc)
            u = _dot(hn, wup_ref[:, cols])
            halves.append(_conv_chunk(u, carry, cw_ref, cols, FFN_CONV) + cb_ref[:, cols])
        act[:, c * fc:(c + 1) * fc] = (_silu(halves[0]) * halves[1]).astype(BF16)
        if (c + 1) % group == 0 or c + 1 == n_chunks:
            rows = slice((c // group) * group * fc, (c + 1) * fc)
            part = _dot(act[:, rows], wdn_ref[rows, :])
            y = part if y is None else y + part
    o_ref[...] = x + _rms(y, gpost_ref[...])


def _mix_ffn(mix_a, mix_b, w_out, x2, B, S, gmix, gpre, w_up, conv_w, conv_b, w_down, gpost,
             layer, tm, fc=256):
    W = GROUP_WIDTH
    nt = S // tm
    row = lambda b, i: (b * nt + i, 0)
    slab = lambda *dims: pl.BlockSpec((None,) + dims, lambda *_: (layer,) + (0,) * len(dims))
    return pl.pallas_call(
        functools.partial(_mix_ffn_kernel, tm=tm, fc=fc, group=6),
        grid=(B, nt),
        in_specs=[pl.BlockSpec((tm, W), row),
                  pl.BlockSpec((tm, W), row),
                  slab(D_MODEL, D_MODEL),
                  pl.BlockSpec((tm, D_MODEL), row),
                  _resident((1, D_MODEL)),
                  _resident((1, D_MODEL)),
                  slab(D_MODEL, 2 * D_FF),
                  slab(FFN_CONV, 2 * D_FF),
                  slab(1, 2 * D_FF),
                  slab(D_FF, D_MODEL),
                  _resident((1, D_MODEL))],
        out_specs=pl.BlockSpec((tm, D_MODEL), row),
        out_shape=jax.ShapeDtypeStruct(x2.shape, F32),
        scratch_shapes=[pltpu.VMEM((SUBLANES, 2 * D_FF), F32),
                        pltpu.VMEM((tm, D_FF), BF16)],
        compiler_params=_params(2),
        name="mixffn",
    )(mix_a, mix_b, w_out, x2, gmix, gpre, w_up, conv_w, conv_b, w_down, gpost)


def _foxgate_kernel(s_ref, b_ref, qk_ref, qt_ref, ka_ref, vt_ref, carry, *, blk):
    W = GROUP_WIDTH

    @pl.when(pl.program_id(1) == 0)
    def _():
        carry[...] = jnp.zeros_like(carry)

    z = s_ref[...] + b_ref[...]
    cs = _exact_dot(_tril(blk).astype(BF16), _log_sigmoid(z)) + carry[...]
    carry[...] = cs[blk - 1:blk, :]
    lane = lax.broadcasted_iota(jnp.int32, (1, LANES), 1)
    lo = lane < FOX_HEAD_DIM
    ones = jnp.where(lane < FOX_HEAD_DIM + FOX_AUG, 1.0, 0.0)
    scale = FOX_HEAD_DIM ** -0.5
    for p in range(FOX_HEADS // 2):
        q = qk_ref[:, p * LANES:(p + 1) * LANES].astype(F32) * scale
        k = qk_ref[:, W + p * LANES:W + (p + 1) * LANES].astype(F32)
        for e in range(2):
            h = 2 * p + e
            qe = pltpu.roll(q, FOX_HEAD_DIM, 1) if e else q
            ke = pltpu.roll(k, FOX_HEAD_DIM, 1) if e else k
            rem = -cs[:, h:h + 1]
            aug = jnp.zeros((blk, LANES), F32)
            for piece in range(FOX_AUG):
                part = rem.astype(BF16).astype(F32)
                aug = jnp.where(lane == FOX_HEAD_DIM + piece, part, aug)
                rem = rem - part
            qt_ref[0, 0, h * LANES:(h + 1) * LANES, :] = jnp.where(lo, qe, ones).T.astype(BF16)
            ka_ref[:, h * LANES:(h + 1) * LANES] = jnp.where(lo, ke, aug).astype(BF16)
        v = qk_ref[:, 2 * W + p * LANES:2 * W + (p + 1) * LANES].astype(F32)
        vt_ref[0, 0, p * LANES:(p + 1) * LANES, :] = v.T.astype(BF16)


def _foxgate(qkv, small, B, S, small_blk, bias_row, blk):
    T = B * S
    W = GROUP_WIDTH
    nb = S // blk
    wide = FOX_HEADS * LANES
    row = lambda b, j: b * nb + j
    return pl.pallas_call(
        functools.partial(_foxgate_kernel, blk=blk),
        grid=(B, nb),
        in_specs=[pl.BlockSpec((blk, SMALL_W), lambda b, j: (row(b, j), small_blk)),
                  _resident((1, SMALL_W)),
                  pl.BlockSpec((blk, 3 * W), lambda b, j: (row(b, j), 0))],
        out_specs=[pl.BlockSpec((1, 1, wide, blk), lambda b, j: (b, j, 0, 0)),
                   pl.BlockSpec((blk, wide), lambda b, j: (row(b, j), 0)),
                   pl.BlockSpec((1, 1, W, blk), lambda b, j: (b, j, 0, 0))],
        out_shape=[jax.ShapeDtypeStruct((B, nb, wide, blk), BF16),
                   jax.ShapeDtypeStruct((T, wide), BF16),
                   jax.ShapeDtypeStruct((B, nb, W, blk), BF16)],
        scratch_shapes=[pltpu.VMEM((1, SMALL_W), F32)],
        compiler_params=_params(2),
        name="foxgate",
    )(small, bias_row, qkv)


def _fox_kernel(qt_ref, k_ref, vt_ref, o_ref, m_ref, l_ref, acc_ref, *, blk, kvb):
    i = pl.program_id(1)
    kv_id = lax.broadcasted_iota(jnp.int32, (kvb, blk), 0)
    q_id = lax.broadcasted_iota(jnp.int32, (kvb, blk), 1)
    first_head = lax.broadcasted_iota(jnp.int32, (LANES, 1), 0) < FOX_HEAD_DIM
    m_ref[...] = jnp.full(m_ref.shape, NEG_INF, F32)
    l_ref[...] = jnp.zeros_like(l_ref)
    acc_ref[...] = jnp.zeros_like(acc_ref)

    def sub_block(j, sub, masked):
        off = pl.multiple_of(j * blk + sub * kvb, kvb)
        for p in range(FOX_HEADS // 2):
            vt = vt_ref[0, j, p * LANES:(p + 1) * LANES, sub * kvb:(sub + 1) * kvb]
            alpha, pv = [], []
            for e in range(2):
                h = 2 * p + e
                hs = slice(h * LANES, (h + 1) * LANES)
                s = _dot(k_ref[pl.ds(off, kvb), hs], qt_ref[0, 0, hs, :])
                if masked:
                    s = jnp.where(kv_id + sub * kvb <= q_id, s, NEG_INF)
                m_old = m_ref[h:h + 1, :]
                m_new = jnp.maximum(m_old, jnp.max(s, axis=0, keepdims=True))
                pe = jnp.exp(s - m_new)
                a = jnp.exp(m_old - m_new)
                m_ref[h:h + 1, :] = m_new
                l_ref[h:h + 1, :] = a * l_ref[h:h + 1, :] + jnp.sum(pe, axis=0, keepdims=True)
                alpha.append(a)
                pv.append(_dot(vt, pe.astype(BF16)))
            acc_ref[p] = (jnp.where(first_head, alpha[0], alpha[1]) * acc_ref[p]
                          + jnp.where(first_head, pv[0], pv[1]))

    def block(j, masked):
        for sub in range(blk // kvb):
            sub_block(j, sub, masked)

    def body(j, carry):
        block(j, False)
        return carry

    lax.fori_loop(0, i, body, 0)
    block(i, True)
    for p in range(FOX_HEADS // 2):
        l = jnp.where(first_head, l_ref[2 * p:2 * p + 1, :], l_ref[2 * p + 1:2 * p + 2, :])
        o_ref[:, p * LANES:(p + 1) * LANES] = (acc_ref[p] / l).T.astype(BF16)


def _fox(qt, ka, vt, B, S, blk):
    T = B * S
    nq = S // blk
    W = GROUP_WIDTH
    wide = FOX_HEADS * LANES
    return pl.pallas_call(
        functools.partial(_fox_kernel, blk=blk, kvb=128),
        grid=(B, nq),
        in_specs=[pl.BlockSpec((1, 1, wide, blk), lambda b, i: (b, i, 0, 0)),
                  pl.BlockSpec((S, wide), lambda b, i: (b, 0)),
                  pl.BlockSpec((1, nq, W, blk), lambda b, i: (b, 0, 0, 0))],
        out_specs=pl.BlockSpec((blk, W), lambda b, i: (b * nq + i, 0)),
        out_shape=jax.ShapeDtypeStruct((T, W), BF16),
        scratch_shapes=[pltpu.VMEM((FOX_HEADS, blk), F32),
                        pltpu.VMEM((FOX_HEADS, blk), F32),
                        pltpu.VMEM((FOX_HEADS // 2, LANES, blk), F32)],
        compiler_params=_params(2),
        name="fox",
    )(qt, ka, vt)


def _unit_lower_solves(ms, rhss, n, nilpotent):
    eye = (lax.broadcasted_iota(jnp.int32, (n, n), 0)
           == lax.broadcasted_iota(jnp.int32, (n, n), 1)).astype(F32)
    xs = [-m for m in ms]
    sols = rhss
    power = 1
    while power < nilpotent:
        x16 = [x.astype(BF16) for x in xs]
        x2 = [_dot(x, x) for x in x16]
        x2_16 = [x.astype(BF16) for x in x2]
        factors = [eye + x + y + _dot(xb, yb) for x, y, xb, yb in zip(xs, x2, x16, x2_16)]
        sols = [_dot(f.astype(BF16), r.astype(BF16)) for f, r in zip(factors, sols)]
        power *= 4
        if power < nilpotent:
            xs = [_dot(y, y) for y in x2_16]
    assert power == nilpotent
    return sols


def _gdn_kernel(act_ref, gate_ref, small_ref, alog_ref, dtb_ref, gain_ref, o_ref,
                state, sol_ref, qd_ref, kd_ref, aqk_ref, ks_ref, ubuf, *, a_lane, b_lane, rows):
    C = GDN_CHUNK
    P = 2 * C
    W = GROUP_WIDTH
    D = GDN_HEAD_DIM
    n_chunks = rows // C

    @pl.when(pl.program_id(1) == 0)
    def _():
        state[...] = jnp.zeros_like(state)

    small = small_ref[...]
    g_all = -jnp.exp(alog_ref[...]) * _softplus(small + dtb_ref[...])
    beta_all = _sigmoid(small)
    r_id = lax.broadcasted_iota(jnp.int32, (P, P), 0)
    c_id = lax.broadcasted_iota(jnp.int32, (P, P), 1)
    tri_bd = (c_id >= r_id - (r_id & (C - 1))) & (c_id <= r_id)
    diag = r_id == c_id
    tri16 = tri_bd.astype(BF16)
    g_tot, ms, rhss, where = [], [], [], []
    for pr in range(rows // P):
        ps = slice(pr * P, (pr + 1) * P)
        G_all = _exact_dot(tri16, g_all[ps, :])
        G_rows = _rows_as_lanes(G_all, a_lane, SUBLANES)
        G_tot = jnp.concatenate(
            [jnp.broadcast_to(G_all[(c + 1) * C - 1:(c + 1) * C, :], (C, SMALL_W)) for c in range(2)],
            axis=0)
        g_tot.append(G_tot)
        for h in range(GDN_HEADS):
            q = act_ref[ps, h * D:(h + 1) * D]
            k = act_ref[ps, W + h * D:W + (h + 1) * D]
            v = act_ref[ps, 2 * W + h * D:2 * W + (h + 1) * D]
            Gc = G_all[:, a_lane + h:a_lane + h + 1]
            Gr = G_rows[h:h + 1, :]
            Gt = G_tot[:, a_lane + h:a_lane + h + 1]
            beta = beta_all[ps, b_lane + h:b_lane + h + 1]
            gamma = jnp.exp(jnp.where(tri_bd, Gc - Gr, NEG_INF))
            kb = k * beta
            kb16, k16 = kb.astype(BF16), k.astype(BF16)
            ms.append(jnp.where(diag, 0.0, _dot_nt(kb16, k16) * gamma))
            eG = jnp.exp(Gc)
            rhss.append(jnp.concatenate([v * beta, kb * eG], axis=1))
            where.append((h, ps))
            aqk_ref[h, ps, :] = (_dot_nt(q.astype(BF16), k16) * gamma).astype(BF16)
            qd_ref[h, ps, :] = (q * eG).astype(BF16)
            kd_ref[h, ps, :] = (k * jnp.exp(Gt - Gc)).astype(BF16)
    for (h, ps), sol in zip(where, _unit_lower_solves(ms, rhss, P, C)):
        sol_ref[h, ps, :] = sol
    for c in range(n_chunks):
        rs = slice(c * C, (c + 1) * C)
        for h in range(GDN_HEADS):
            ks_ref[h, c] = _dot_tn(kd_ref[h, rs, :], sol_ref[h, rs, :].astype(BF16))

    states = [state[h] for h in range(GDN_HEADS)]
    for c in range(n_chunks):
        rs = slice(c * C, (c + 1) * C)
        first = c % 2 == 0
        for h in range(GDN_HEADS):
            sl = slice(h * D, (h + 1) * D)
            St = states[h]
            St16 = St.astype(BF16)
            decay = jnp.exp(g_tot[c // 2][(c % 2) * C:(c % 2) * C + 1, a_lane + h:a_lane + h + 1])
            states[h] = (decay * St + ks_ref[h, c, :, 0:D]
                         - _dot(ks_ref[h, c, :, D:2 * D].astype(BF16), St16))
            u = sol_ref[h, rs, 0:D] - _dot(sol_ref[h, rs, D:2 * D].astype(BF16), St16)
            u16 = u.astype(BF16)
            o = _dot(qd_ref[h, rs, :], St16)
            if first:
                ubuf[h] = jnp.concatenate([u16, jnp.zeros_like(u16)], axis=0)
                o = o + _dot(aqk_ref[h, rs, 0:C], u16)
            else:
                ubuf[h, C:P, :] = u16
                o = o + _dot(aqk_ref[h, rs, :], ubuf[h])
            o = _rms(o, gain_ref[...]) * gate_ref[rs, sl]
            o_ref[rs, sl] = o.astype(BF16)
    for h in range(GDN_HEADS):
        state[h] = states[h]


def _gdn(act, gate, small, B, S, alog_row, dtb_row, gain_row, a_lane, b_lane, rows=512):
    T = B * S
    W = GROUP_WIDTH
    H, D = GDN_HEADS, GDN_HEAD_DIM
    P = 2 * GDN_CHUNK
    nt = S // rows
    row = lambda b, i: (b * nt + i, 0)
    return pl.pallas_call(
        functools.partial(_gdn_kernel, a_lane=a_lane, b_lane=b_lane, rows=rows),
        grid=(B, nt),
        in_specs=[pl.BlockSpec((rows, 3 * W), row),
                  pl.BlockSpec((rows, W), row),
                  pl.BlockSpec((rows, SMALL_W), row),
                  _resident((1, SMALL_W)),
                  _resident((1, SMALL_W)),
                  _resident((1, D))],
        out_specs=pl.BlockSpec((rows, W), row),
        out_shape=jax.ShapeDtypeStruct((T, W), BF16),
        scratch_shapes=[pltpu.VMEM((H, D, D), F32),
                        pltpu.VMEM((H, rows, 2 * D), F32),
                        pltpu.VMEM((H, rows, D), BF16),
                        pltpu.VMEM((H, rows, D), BF16),
                        pltpu.VMEM((H, rows, P), BF16),
                        pltpu.VMEM((H, rows // GDN_CHUNK, D, 2 * D), F32),
                        pltpu.VMEM((H, P, D), BF16)],
        compiler_params=_params(2),
        name="gdn",
    )(act, gate, small, alog_row, dtb_row, gain_row)


def _block_ref_rows(x, half):
    R, L = x.shape
    if half >= SUBLANES:
        xb = x.reshape(R // (2 * half), 2 * half, L)
        return jnp.broadcast_to(xb[:, half:half + 1, :], xb.shape).reshape(R, L)
    xb = x.reshape(R // SUBLANES, SUBLANES, L)
    sub = lax.broadcasted_iota(jnp.int32, (1, SUBLANES, 1), 1)
    out = None
    for start in range(SUBLANES - 2 * half, -1, -2 * half):
        row = jnp.broadcast_to(xb[:, start + half:start + half + 1, :], xb.shape)
        out = row if out is None else jnp.where(sub < start + 2 * half, row, out)
    return out.reshape(R, L)


def _hgrn_kernel(q_ref, f_ref, k_ref, i_ref, g_ref, gain_ref, o_ref, state, *, rows):
    C = HGRN_CHUNK
    P = 2 * C
    D = HGRN_HEAD_DIM
    halves = [1 << b for b in range(C.bit_length() - 1)]

    @pl.when(pl.program_id(1) == 0)
    def _():
        state[...] = jnp.zeros_like(state)

    r_id = lax.broadcasted_iota(jnp.int32, (P, P), 0)
    c_id = lax.broadcasted_iota(jnp.int32, (P, P), 1)
    tri16 = ((c_id >= r_id - (r_id & (C - 1))) & (c_id <= r_id)).astype(BF16)
    differ = r_id ^ c_id
    row_id = lax.broadcasted_iota(jnp.int32, (P, 1), 0)
    gain = gain_ref[...]
    for pr in range(rows // P):
        ps = slice(pr * P, (pr + 1) * P)
        G_all = _exact_dot(tri16, f_ref[ps, :])
        decays = [jnp.exp(-jnp.abs(G_all - _block_ref_rows(G_all, half))) for half in halves]
        heads = [slice(h * D, (h + 1) * D) for h in range(HGRN_HEADS)]
        scores = [jnp.zeros((P, P), F32) for _ in heads]
        for half, dec in reversed(list(zip(halves, decays))):
            upper = (row_id & half) != 0
            for h, sl in enumerate(heads):
                z16 = (jnp.where(upper, q_ref[ps, sl], k_ref[ps, sl]) * dec[:, sl]).astype(BF16)
                scores[h] = jnp.where(differ < 2 * half, _dot_nt(z16, z16), scores[h])
        for h, sl in enumerate(heads):
            q, k, G = q_ref[ps, sl], k_ref[ps, sl], G_all[:, sl]
            v16 = i_ref[ps, sl].astype(BF16)
            a = jnp.where(differ == 0, _dot_nt(q.astype(BF16), k.astype(BF16)), scores[h])
            a = jnp.where(r_id >= c_id, a, 0.0)
            o_intra = _dot(a.astype(BF16), v16)
            q_in = (q * jnp.exp(G)).astype(BF16)
            for c in range(2):
                cs = slice(c * C, (c + 1) * C)
                G_last = G[(c + 1) * C - 1:(c + 1) * C, :]
                St = state[h]
                o = _dot_nt(q_in[cs, :], St.astype(BF16)) + o_intra[cs, :]
                k_out = (k[cs, :] * jnp.exp(G_last - G[cs, :])).astype(BF16)
                state[h] = jnp.exp(G_last) * St + _dot_tn(v16[cs, :], k_out)
                rs = slice(pr * P + c * C, pr * P + (c + 1) * C)
                o_ref[rs, sl] = (_rms(o, gain) * g_ref[rs, sl]).astype(BF16)


def _hgrn(proj, first_blk, B, S, gain_row, rows=512):
    T = B * S
    W = GROUP_WIDTH
    nt = S // rows
    spec = lambda j: pl.BlockSpec((rows, W), lambda b, i: (b * nt + i, j))
    return pl.pallas_call(
        functools.partial(_hgrn_kernel, rows=rows),
        grid=(B, nt),
        in_specs=[spec(first_blk + j) for j in range(5)] + [_resident((1, HGRN_HEAD_DIM))],
        out_specs=spec(0),
        out_shape=jax.ShapeDtypeStruct((T, W), BF16),
        scratch_shapes=[pltpu.VMEM((HGRN_HEADS, HGRN_HEAD_DIM, HGRN_HEAD_DIM), F32)],
        compiler_params=_params(2),
        name="hgrn2",
    )(proj, proj, proj, proj, proj, gain_row)


def _ssd_kernel(xbc_ref, z_ref, small_ref, alog_ref, dtb_ref, dvec_ref, gain_ref,
                o_ref, state, *, rows):
    L = M2_CHUNK
    W = GROUP_WIDTH
    N = M2_STATE
    pairs_per_group = M2_HEADS // M2_GROUPS // 2

    @pl.when(pl.program_id(1) == 0)
    def _():
        state[...] = jnp.zeros_like(state)

    dt_all = _softplus(small_ref[...] + dtb_ref[...])
    tri = _tril(L)
    tri16 = tri.astype(BF16)
    neg_a = -jnp.exp(alog_ref[...])
    lo = _lane_lo((1, LANES))
    for ck in range(rows // L):
        rs = slice(ck * L, (ck + 1) * L)
        xbc = xbc_ref[rs, :]
        dt = dt_all[rs, :]
        A_cs = _exact_dot(tri16, dt * neg_a)
        A_rows = _rows_as_lanes(A_cs, 0, M2_HEADS)
        for g in range(M2_GROUPS):
            Bg = xbc[:, W + g * N:W + (g + 1) * N]
            Cg = xbc[:, W + M2_GROUPS * N + g * N:W + M2_GROUPS * N + (g + 1) * N]
            cb = _dot_nt(Cg.astype(BF16), Bg.astype(BF16))
            ys = []
            for pp in range(pairs_per_group):
                p = g * pairs_per_group + pp
                xs = xbc[:, p * LANES:(p + 1) * LANES]
                dtl = jnp.where(lo, dt[:, 2 * p:2 * p + 1], dt[:, 2 * p + 1:2 * p + 2])
                X16 = (xs * dtl).astype(BF16)
                St = state[p]
                rhs = jnp.concatenate([X16, St.astype(BF16)], axis=0)
                y_h, st_h = [], []
                for e in range(2):
                    h = 2 * p + e
                    ac = jnp.broadcast_to(A_cs[:, h:h + 1], (L, LANES))
                    ar = A_rows[h:h + 1, :]
                    decay = jnp.exp(jnp.where(tri, ac - ar, NEG_INF))
                    lhs = jnp.concatenate([(cb * decay).astype(BF16),
                                           (Cg * jnp.exp(ac)).astype(BF16)], axis=1)
                    y_h.append(_dot(lhs, rhs))
                    last = ac[L - 1:L, :]
                    st_h.append(jnp.exp(last) * St
                                + _dot_tn((Bg * jnp.exp(last - ac)).astype(BF16), X16))
                state[p] = jnp.where(lo, st_h[0], st_h[1])
                ys.append(jnp.where(lo, y_h[0], y_h[1]) + dvec_ref[:, p * LANES:(p + 1) * LANES] * xs)
            gs = slice(g * (W // M2_GROUPS), (g + 1) * (W // M2_GROUPS))
            y = jnp.concatenate(ys, axis=1) * z_ref[rs, gs]
            o_ref[rs, gs] = _rms(y, gain_ref[:, gs]).astype(BF16)


def _ssd(proj, z_blk, small_blk, B, S, alog_row, dtb_row, dvec, gain_row, rows=512):
    T = B * S
    W = GROUP_WIDTH
    CW = W + 2 * M2_GROUPS * M2_STATE
    nt = S // rows
    row = lambda b, i: b * nt + i
    return pl.pallas_call(
        functools.partial(_ssd_kernel, rows=rows),
        grid=(B, nt),
        in_specs=[pl.BlockSpec((rows, CW), lambda b, i: (row(b, i), 0)),
                  pl.BlockSpec((rows, W), lambda b, i: (row(b, i), z_blk)),
                  pl.BlockSpec((rows, SMALL_W), lambda b, i: (row(b, i), small_blk)),
                  _resident((1, SMALL_W)),
                  _resident((1, SMALL_W)),
                  _resident((1, W)),
                  _resident((1, W))],
        out_specs=pl.BlockSpec((rows, W), lambda b, i: (row(b, i), 0)),
        out_shape=jax.ShapeDtypeStruct((T, W), BF16),
        scratch_shapes=[pltpu.VMEM((M2_HEADS // 2, M2_STATE, LANES), F32)],
        compiler_params=_params(2),
        name="ssd",
    )(proj, proj, proj, alog_row, dtb_row, dvec, gain_row)


def _pad_lanes(v, first, width=SMALL_W):
    v = v.astype(F32)
    return jnp.pad(v, (first, width - first - v.shape[0])).reshape(1, width)


def kernel(x, norm_gains, w_out, ffn_w_up, ffn_conv_w, ffn_conv_b, ffn_w_down,
           even_w_in, fox_f_bias, gdn_conv_w, gdn_A_log, gdn_dt_bias, gdn_norm_gain,
           odd_w_in, hgrn_lb_logits, hgrn_norm_gain, m2_conv_w, m2_conv_b,
           m2_A_log, m2_dt_bias, m2_D, m2_norm_gain):
    B, S, D = x.shape
    assert D == D_MODEL and S % 512 == 0
    T = B * S
    W = GROUP_WIDTH
    tm = 512
    tm_in = 256
    tm_ffn = 512
    row = lambda v: v.astype(F32).reshape(1, -1)
    x2 = x.reshape(T, D).astype(F32)
    w_out16, w_up16, w_down16 = (w.astype(BF16) for w in (w_out, ffn_w_up, ffn_w_down))
    conv_w32 = ffn_conv_w.astype(F32)
    conv_b32 = ffn_conv_b.astype(F32).reshape(ffn_conv_b.shape[0], 1, -1)

    o_ff = 3 * W
    o_qkv = o_ff + FOX_HEADS
    o_a = o_qkv + 3 * W
    o_b = o_a + GDN_HEADS
    o_gate = o_b + GDN_HEADS
    a_lane, b_lane = FOX_HEADS, FOX_HEADS + GDN_HEADS
    pad = jnp.zeros((D, SMALL_W - FOX_HEADS - 2 * GDN_HEADS), BF16)
    w16 = even_w_in.astype(BF16)
    w_even = jnp.concatenate([w16[:, :o_ff], w16[:, o_qkv:o_a], w16[:, o_gate:],
                              w16[:, o_ff:o_qkv], w16[:, o_a:o_gate], pad], axis=1)
    g = norm_gains[0]
    qkv, gdn_act, gdn_gate, small = _inproj_even(x2, row(g[0]), w_even, gdn_conv_w.astype(F32),
                                                 tm_in, S // tm_in)
    fox_blk = 256
    qt, ka, vt = _foxgate(qkv, small, B, S, 0, _pad_lanes(fox_f_bias, 0), fox_blk)
    o_fox = _fox(qt, ka, vt, B, S, fox_blk)
    o_gdn = _gdn(gdn_act, gdn_gate, small, B, S, _pad_lanes(gdn_A_log, a_lane),
                 _pad_lanes(gdn_dt_bias, a_lane), row(gdn_norm_gain), a_lane, b_lane)
    x2 = _mix_ffn(o_fox, o_gdn, w_out16, x2, B, S, row(g[1]), row(g[2]), w_up16, conv_w32, conv_b32,
                  w_down16, row(g[3]), 0, tm_ffn)

    assert hgrn_lb_logits.shape == (2, W)
    CW = W + 2 * M2_GROUPS * M2_STATE
    pad = jnp.zeros((D, SMALL_W - M2_HEADS), BF16)
    w16 = odd_w_in.astype(BF16)
    w_odd = jnp.concatenate([w16[:, 5 * W:5 * W + CW], w16[:, :5 * W], w16[:, 5 * W + CW:], pad],
                            axis=1)
    g = norm_gains[1]
    proj = _inproj_odd(x2, row(g[0]), w_odd, hgrn_lb_logits.astype(F32), m2_conv_w.astype(F32),
                       row(m2_conv_b), tm, S // tm)
    o_hgrn = _hgrn(proj, CW // W, B, S, row(hgrn_norm_gain))
    o_ssd = _ssd(proj, (CW + 5 * W) // W, (CW + 6 * W) // SMALL_W, B, S, _pad_lanes(m2_A_log, 0),
                 _pad_lanes(m2_dt_bias, 0), row(jnp.repeat(m2_D, M2_HEAD_DIM)), row(m2_norm_gain))
    x2 = _mix_ffn(o_hgrn, o_ssd, w_out16, x2, B, S, row(g[1]), row(g[2]), w_up16, conv_w32, conv_b32,
                  w_down16, row(g[3]), 1, tm_ffn)
    return x2.reshape(B, S, D).astype(x.dtype)
```

```python
import functools

import jax
import jax.numpy as jnp
from jax import lax
from jax.experimental import pallas as pl
from jax.experimental.pallas import tpu as pltpu

F32 = jnp.float32
BF16 = jnp.bfloat16

D_MODEL = 1024
GROUP_WIDTH = D_MODEL // 2
FOX_HEAD_DIM = 64
FOX_HEADS = GROUP_WIDTH // FOX_HEAD_DIM
FOX_AUG = 3
GDN_HEAD_DIM = 128
GDN_HEADS = GROUP_WIDTH // GDN_HEAD_DIM
GDN_CHUNK = 64
SHORT_CONV = 4
HGRN_HEAD_DIM = 128
HGRN_HEADS = GROUP_WIDTH // HGRN_HEAD_DIM
HGRN_CHUNK = 64
M2_HEAD_DIM = 64
M2_HEADS = GROUP_WIDTH // M2_HEAD_DIM
M2_GROUPS = 2
M2_STATE = 128
M2_CHUNK = 128
D_FF = 2816
FFN_CONV = 3
NORM_EPS = 1e-6

LANES = 128
SUBLANES = 8
SMALL_W = LANES
VMEM_LIMIT = 56 * 1024 * 1024

NEG_INF = float("-inf")


def _dot(a, b):
    return jnp.dot(a, b, preferred_element_type=F32)


def _dot_nt(a, b):
    return lax.dot_general(a, b, (((1,), (1,)), ((), ())), preferred_element_type=F32)


def _dot_tn(a, b):
    return lax.dot_general(a, b, (((0,), (0,)), ((), ())), preferred_element_type=F32)


def _bf16_pieces(x):
    hi = x.astype(BF16)
    rest = x - hi.astype(F32)
    mid = rest.astype(BF16)
    return hi, mid, (rest - mid.astype(F32)).astype(BF16)


def _exact_dot(sel16, x):
    hi, mid, lo = _bf16_pieces(x)
    return _dot(sel16, hi) + _dot(sel16, mid) + _dot(sel16, lo)


def _rms(x, gain):
    return x * lax.rsqrt(jnp.mean(x * x, axis=-1, keepdims=True) + NORM_EPS) * gain


def _sigmoid(x):
    return 1.0 / (1.0 + jnp.exp(-x))


def _silu(x):
    h = 0.5 * x
    return h * jnp.tanh(h) + h


def _softplus(x):
    return jnp.maximum(x, 0.0) + jnp.log1p(jnp.exp(-jnp.abs(x)))


def _log_sigmoid(x):
    return jnp.minimum(x, 0.0) - jnp.log1p(jnp.exp(-jnp.abs(x)))


def _tril(n, strict=False):
    r = lax.broadcasted_iota(jnp.int32, (n, n), 0)
    c = lax.broadcasted_iota(jnp.int32, (n, n), 1)
    return (r > c) if strict else (r >= c)


def _lane_lo(shape):
    return lax.broadcasted_iota(jnp.int32, shape, len(shape) - 1) < (LANES // 2)


def _rows_as_lanes(cols, first, n):
    sel = (lax.broadcasted_iota(jnp.int32, (n, LANES), 1)
           == lax.broadcasted_iota(jnp.int32, (n, LANES), 0) + first).astype(BF16)
    hi, mid, lo = _bf16_pieces(cols)
    return _dot_nt(sel, hi) + _dot_nt(sel, mid) + _dot_nt(sel, lo)


def _params(n_grid):
    return pltpu.CompilerParams(dimension_semantics=("arbitrary",) * n_grid,
                                vmem_limit_bytes=VMEM_LIMIT)


def _resident(shape):
    nd = len(shape)
    return pl.BlockSpec(shape, lambda *_: (0,) * nd)


def _conv_tail_reset(tail, tiles_per_seq):
    @pl.when(pl.program_id(0) % tiles_per_seq == 0)
    def _():
        tail[...] = jnp.zeros_like(tail)


def _conv_chunk(p, tail, w_ref, cols, width):
    rows = p.shape[0]
    ext = jnp.concatenate([tail[:, cols], p], axis=0)
    tail[:, cols] = p[rows - SUBLANES:rows, :]
    y = None
    for k in range(width):
        start = SUBLANES - (width - 1) + k
        term = ext[start:start + rows, :] * w_ref[k:k + 1, cols]
        y = term if y is None else y + term
    return y


def _staggered(stages, lag=2):
    pending = []
    for produce, consume in stages:
        pending.append((produce(), consume))
        if len(pending) > lag:
            value, done = pending.pop(0)
            done(value)
    for value, done in pending:
        done(value)


def _inproj_even_kernel(x_ref, g_ref, w_ref, cw_ref, qkv_ref, act_ref, gate_ref, small_ref, tail,
                        *, tm, tiles_per_seq):
    W = GROUP_WIDTH
    D = GDN_HEAD_DIM
    _conv_tail_reset(tail, tiles_per_seq)
    hn = _rms(x_ref[...], g_ref[...]).astype(BF16)
    proj = lambda c0, n: (lambda: _dot(hn, w_ref[:, c0:c0 + n]))

    PW = 2 * D

    def fox_out(c0):
        def consume(p):
            qkv_ref[:, c0:c0 + PW] = p.astype(BF16)
        return consume

    def gdn_out(c0):
        def consume(p):
            a = _silu(_conv_chunk(p, tail, cw_ref, slice(c0, c0 + PW), SHORT_CONV))
            for h in range(PW // D):
                ah = a[:, h * D:(h + 1) * D]
                if c0 < 2 * W:
                    ah = ah * lax.rsqrt(jnp.sum(ah * ah, axis=-1, keepdims=True) + NORM_EPS)
                if c0 < W:
                    ah = ah * D ** -0.5
                act_ref[:, c0 + h * D:c0 + (h + 1) * D] = ah
        return consume

    def gate_out(c0):
        def consume(p):
            gate_ref[:, c0:c0 + PW] = _silu(p)
        return consume

    def small_out(p):
        small_ref[...] = p

    stages = []
    for c0 in range(0, 3 * W, PW):
        stages += [(proj(3 * W + c0, PW), gdn_out(c0)), (proj(c0, PW), fox_out(c0))]
    stages += [(proj(6 * W + c0, PW), gate_out(c0)) for c0 in range(0, W, PW)]
    stages += [(proj(7 * W, SMALL_W), small_out)]
    _staggered(stages)


def _inproj_even(x2, gain, w, conv_w, tm, tiles_per_seq):
    T = x2.shape[0]
    W = GROUP_WIDTH
    widths = (3 * W, 3 * W, W, SMALL_W)
    dtypes = (BF16, F32, F32, F32)
    return pl.pallas_call(
        functools.partial(_inproj_even_kernel, tm=tm, tiles_per_seq=tiles_per_seq),
        grid=(T // tm,),
        in_specs=[pl.BlockSpec((tm, D_MODEL), lambda i: (i, 0)),
                  _resident((1, D_MODEL)),
                  _resident(w.shape),
                  _resident(conv_w.shape)],
        out_specs=[pl.BlockSpec((tm, n), lambda i: (i, 0)) for n in widths],
        out_shape=[jax.ShapeDtypeStruct((T, n), dt) for n, dt in zip(widths, dtypes)],
        scratch_shapes=[pltpu.VMEM((SUBLANES, 3 * W), F32)],
        compiler_params=_params(1),
        name="inproj_even",
    )(x2, gain, w, conv_w)


def _inproj_odd_kernel(x_ref, g_ref, w_ref, lbl_ref, cw_ref, cb_ref, o_ref, tail,
                       *, tm, tiles_per_seq):
    W = GROUP_WIDTH
    CW = W + 2 * M2_GROUPS * M2_STATE
    _conv_tail_reset(tail, tiles_per_seq)
    hn = _rms(x_ref[...], g_ref[...]).astype(BF16)
    proj = lambda c0, n: (lambda: _dot(hn, w_ref[:, c0:c0 + n]))

    logits = lbl_ref[...]
    e = jnp.exp(logits - jnp.max(logits, axis=0, keepdims=True))
    prob = e / jnp.sum(e, axis=0, keepdims=True)
    lb = (prob[0:1, :] + prob[1:2, :]) - prob[0:1, :]

    def conv_out(c):
        def consume(p):
            cols = slice(c * W, (c + 1) * W)
            o_ref[:, cols] = _silu(_conv_chunk(p, tail, cw_ref, cols, SHORT_CONV) + cb_ref[:, cols])
        return consume

    def mapped_out(col, fn):
        def consume(p):
            o_ref[:, col:col + p.shape[1]] = fn(p)
        return consume

    def forget_out(col):
        def consume(p):
            gate = _sigmoid(p)
            o_ref[:, col:col + W] = jnp.log(lb + (1.0 - lb) * gate)
            o_ref[:, col + W:col + 2 * W] = (1.0 - lb) * (1.0 - gate)
        return consume

    keep = lambda p: p
    stages = [(proj(0, W), conv_out(0)),
              (proj(CW + 2 * W, W), mapped_out(CW + 3 * W, keep)),
              (proj(W, W), conv_out(1)),
              (proj(CW, W), mapped_out(CW, _silu)),
              (proj(CW + W, W), forget_out(CW + W)),
              (proj(CW + 3 * W, W), mapped_out(CW + 4 * W, _silu)),
              (proj(CW + 4 * W, W), mapped_out(CW + 5 * W, _silu)),
              (proj(CW + 5 * W, SMALL_W), mapped_out(CW + 6 * W, keep))]
    _staggered(stages)


def _inproj_odd(x2, gain, w, lb_logits, conv_w, conv_b, tm, tiles_per_seq):
    T = x2.shape[0]
    W = GROUP_WIDTH
    CW = W + 2 * M2_GROUPS * M2_STATE
    n_out = CW + 6 * W + SMALL_W
    return pl.pallas_call(
        functools.partial(_inproj_odd_kernel, tm=tm, tiles_per_seq=tiles_per_seq),
        grid=(T // tm,),
        in_specs=[pl.BlockSpec((tm, D_MODEL), lambda i: (i, 0)),
                  _resident((1, D_MODEL)),
                  _resident(w.shape),
                  _resident(lb_logits.shape),
                  _resident(conv_w.shape),
                  _resident(conv_b.shape)],
        out_specs=pl.BlockSpec((tm, n_out), lambda i: (i, 0)),
        out_shape=jax.ShapeDtypeStruct((T, n_out), F32),
        scratch_shapes=[pltpu.VMEM((SUBLANES, CW), F32)],
        compiler_params=_params(1),
        name="inproj_odd",
    )(x2, gain, w, lb_logits, conv_w, conv_b)


def _mix_ffn_kernel(ma_ref, mb_ref, wo_ref, x_ref, gmix_ref, gpre_ref, wup_ref, cw_ref, cb_ref,
                    wdn_ref, gpost_ref, o_ref, carry, act, *, tm, fc, group):
    W = GROUP_WIDTH
    n_chunks = D_FF // fc

    @pl.when(pl.program_id(1) == 0)
    def _():
        carry[...] = jnp.zeros_like(carry)

    mix = _dot(ma_ref[...], wo_ref[0:W, :]) + _dot(mb_ref[...], wo_ref[W:2 * W, :])
    x = x_ref[...] + _rms(mix, gmix_ref[...])
    hn = _rms(x, gpre_ref[...]).astype(BF16)
    y = None
    for c in range(n_chunks):
        halves = []
        for half in range(2):
            cols = slice(half * D_FF + c * fc, half * D_FF + (c + 1) * fc)
            u = _dot(hn, wup_ref[:, cols])
            halves.append(_conv_chunk(u, carry, cw_ref, cols, FFN_CONV) + cb_ref[:, cols])
        act[:, c * fc:(c + 1) * fc] = (_silu(halves[0]) * halves[1]).astype(BF16)
        if (c + 1) % group == 0 or c + 1 == n_chunks:
            rows = slice((c // group) * group * fc, (c + 1) * fc)
            part = _dot(act[:, rows], wdn_ref[rows, :])
            y = part if y is None else y + part
    o_ref[...] = x + _rms(y, gpost_ref[...])


def _mix_ffn(mix_a, mix_b, w_out, x2, B, S, gmix, gpre, w_up, conv_w, conv_b, w_down, gpost,
             layer, tm, fc=256):
    W = GROUP_WIDTH
    nt = S // tm
    row = lambda b, i: (b * nt + i, 0)
    slab = lambda *dims: pl.BlockSpec((None,) + dims, lambda *_: (layer,) + (0,) * len(dims))
    return pl.pallas_call(
        functools.partial(_mix_ffn_kernel, tm=tm, fc=fc, group=6),
        grid=(B, nt),
        in_specs=[pl.BlockSpec((tm, W), row),
                  pl.BlockSpec((tm, W), row),
                  slab(D_MODEL, D_MODEL),
                  pl.BlockSpec((tm, D_MODEL), row),
                  _resident((1, D_MODEL)),
                  _resident((1, D_MODEL)),
                  slab(D_MODEL, 2 * D_FF),
                  slab(FFN_CONV, 2 * D_FF),
                  slab(1, 2 * D_FF),
                  slab(D_FF, D_MODEL),
                  _resident((1, D_MODEL))],
        out_specs=pl.BlockSpec((tm, D_MODEL), row),
        out_shape=jax.ShapeDtypeStruct(x2.shape, F32),
        scratch_shapes=[pltpu.VMEM((SUBLANES, 2 * D_FF), F32),
                        pltpu.VMEM((tm, D_FF), BF16)],
        compiler_params=_params(2),
        name="mixffn",
    )(mix_a, mix_b, w_out, x2, gmix, gpre, w_up, conv_w, conv_b, w_down, gpost)


def _foxgate_kernel(s_ref, b_ref, qk_ref, qt_ref, ka_ref, vt_ref, carry, *, blk, nsub):
    W = GROUP_WIDTH

    @pl.when(pl.program_id(1) == 0)
    def _():
        carry[...] = jnp.zeros_like(carry)

    lane = lax.broadcasted_iota(jnp.int32, (1, LANES), 1)
    lo = lane < FOX_HEAD_DIM
    ones = jnp.where(lane < FOX_HEAD_DIM + FOX_AUG, 1.0, 0.0)
    scale = FOX_HEAD_DIM ** -0.5
    tri16 = _tril(blk).astype(BF16)
    total = carry[...]
    for sb in range(nsub):
        rs = slice(sb * blk, (sb + 1) * blk)
        z = s_ref[rs, :] + b_ref[...]
        cs = _exact_dot(tri16, _log_sigmoid(z)) + total
        total = cs[blk - 1:blk, :]
        for p in range(FOX_HEADS // 2):
            q = qk_ref[rs, p * LANES:(p + 1) * LANES].astype(F32) * scale
            k = qk_ref[rs, W + p * LANES:W + (p + 1) * LANES].astype(F32)
            for e in range(2):
                h = 2 * p + e
                qe = pltpu.roll(q, FOX_HEAD_DIM, 1) if e else q
                ke = pltpu.roll(k, FOX_HEAD_DIM, 1) if e else k
                rem = -cs[:, h:h + 1]
                aug = jnp.zeros((blk, LANES), F32)
                for piece in range(FOX_AUG):
                    part = rem.astype(BF16).astype(F32)
                    aug = jnp.where(lane == FOX_HEAD_DIM + piece, part, aug)
                    rem = rem - part
                qt_ref[0, sb, h * LANES:(h + 1) * LANES, :] = jnp.where(lo, qe, ones).T.astype(BF16)
                ka_ref[rs, h * LANES:(h + 1) * LANES] = jnp.where(lo, ke, aug).astype(BF16)
            v = qk_ref[rs, 2 * W + p * LANES:2 * W + (p + 1) * LANES].astype(F32)
            vt_ref[0, sb, p * LANES:(p + 1) * LANES, :] = v.T.astype(BF16)
    carry[...] = total


def _foxgate(qkv, small, B, S, small_blk, bias_row, blk, nsub=2):
    T = B * S
    W = GROUP_WIDTH
    nb = S // blk
    ns = nb // nsub
    rows = nsub * blk
    wide = FOX_HEADS * LANES
    row = lambda b, j: b * ns + j
    return pl.pallas_call(
        functools.partial(_foxgate_kernel, blk=blk, nsub=nsub),
        grid=(B, ns),
        in_specs=[pl.BlockSpec((rows, SMALL_W), lambda b, j: (row(b, j), small_blk)),
                  _resident((1, SMALL_W)),
                  pl.BlockSpec((rows, 3 * W), lambda b, j: (row(b, j), 0))],
        out_specs=[pl.BlockSpec((1, nsub, wide, blk), lambda b, j: (b, j, 0, 0)),
                   pl.BlockSpec((rows, wide), lambda b, j: (row(b, j), 0)),
                   pl.BlockSpec((1, nsub, W, blk), lambda b, j: (b, j, 0, 0))],
        out_shape=[jax.ShapeDtypeStruct((B, nb, wide, blk), BF16),
                   jax.ShapeDtypeStruct((T, wide), BF16),
                   jax.ShapeDtypeStruct((B, nb, W, blk), BF16)],
        scratch_shapes=[pltpu.VMEM((1, SMALL_W), F32)],
        compiler_params=_params(2),
        name="foxgate",
    )(small, bias_row, qkv)


def _fox_kernel(qt_ref, k_ref, vt_ref, o_ref, m_ref, l_ref, acc_ref, *, blk, kvb):
    i = pl.program_id(1)
    kv_id = lax.broadcasted_iota(jnp.int32, (kvb, blk), 0)
    q_id = lax.broadcasted_iota(jnp.int32, (kvb, blk), 1)
    first_head = lax.broadcasted_iota(jnp.int32, (LANES, 1), 0) < FOX_HEAD_DIM
    m_ref[...] = jnp.full(m_ref.shape, NEG_INF, F32)
    l_ref[...] = jnp.zeros_like(l_ref)
    acc_ref[...] = jnp.zeros_like(acc_ref)

    def sub_block(j, sub, masked):
        off = pl.multiple_of(j * blk + sub * kvb, kvb)
        for p in range(FOX_HEADS // 2):
            vt = vt_ref[0, j, p * LANES:(p + 1) * LANES, sub * kvb:(sub + 1) * kvb]
            alpha, pv = [], []
            for e in range(2):
                h = 2 * p + e
                hs = slice(h * LANES, (h + 1) * LANES)
                s = _dot(k_ref[pl.ds(off, kvb), hs], qt_ref[0, 0, hs, :])
                if masked:
                    s = jnp.where(kv_id + sub * kvb <= q_id, s, NEG_INF)
                m_old = m_ref[h:h + 1, :]
                m_new = jnp.maximum(m_old, jnp.max(s, axis=0, keepdims=True))
                pe = jnp.exp(s - m_new)
                a = jnp.exp(m_old - m_new)
                m_ref[h:h + 1, :] = m_new
                l_ref[h:h + 1, :] = a * l_ref[h:h + 1, :] + jnp.sum(pe, axis=0, keepdims=True)
                alpha.append(a)
                pv.append(_dot(vt, pe.astype(BF16)))
            acc_ref[p] = (jnp.where(first_head, alpha[0], alpha[1]) * acc_ref[p]
                          + jnp.where(first_head, pv[0], pv[1]))

    def block(j, masked):
        for sub in range(blk // kvb):
            sub_block(j, sub, masked)

    def body(j, carry):
        block(j, False)
        return carry

    lax.fori_loop(0, i, body, 0)
    block(i, True)
    for p in range(FOX_HEADS // 2):
        l = jnp.where(first_head, l_ref[2 * p:2 * p + 1, :], l_ref[2 * p + 1:2 * p + 2, :])
        o_ref[:, p * LANES:(p + 1) * LANES] = (acc_ref[p] / l).T.astype(BF16)


def _fox(qt, ka, vt, B, S, blk):
    T = B * S
    nq = S // blk
    W = GROUP_WIDTH
    wide = FOX_HEADS * LANES
    return pl.pallas_call(
        functools.partial(_fox_kernel, blk=blk, kvb=128),
        grid=(B, nq),
        in_specs=[pl.BlockSpec((1, 1, wide, blk), lambda b, i: (b, i, 0, 0)),
                  pl.BlockSpec((S, wide), lambda b, i: (b, 0)),
                  pl.BlockSpec((1, nq, W, blk), lambda b, i: (b, 0, 0, 0))],
        out_specs=pl.BlockSpec((blk, W), lambda b, i: (b * nq + i, 0)),
        out_shape=jax.ShapeDtypeStruct((T, W), BF16),
        scratch_shapes=[pltpu.VMEM((FOX_HEADS, blk), F32),
                        pltpu.VMEM((FOX_HEADS, blk), F32),
                        pltpu.VMEM((FOX_HEADS // 2, LANES, blk), F32)],
        compiler_params=_params(2),
        name="fox",
    )(qt, ka, vt)


def _unit_lower_solves(ms, rhss, n, nilpotent):
    eye = (lax.broadcasted_iota(jnp.int32, (n, n), 0)
           == lax.broadcasted_iota(jnp.int32, (n, n), 1)).astype(F32)
    xs = [-m for m in ms]
    sols = rhss
    power = 1
    while power < nilpotent:
        x16 = [x.astype(BF16) for x in xs]
        x2 = [_dot(x, x) for x in x16]
        x2_16 = [x.astype(BF16) for x in x2]
        factors = [eye + x + y + _dot(xb, yb) for x, y, xb, yb in zip(xs, x2, x16, x2_16)]
        sols = [_dot(f.astype(BF16), r.astype(BF16)) for f, r in zip(factors, sols)]
        power *= 4
        if power < nilpotent:
            xs = [_dot(y, y) for y in x2_16]
    assert power == nilpotent
    return sols


def _gdn_kernel(act_ref, gate_ref, small_ref, alog_ref, dtb_ref, gain_ref, o_ref,
                state, sol_ref, qd_ref, kd_ref, aqk_ref, ks_ref, ubuf, *, a_lane, b_lane, rows):
    C = GDN_CHUNK
    P = 2 * C
    W = GROUP_WIDTH
    D = GDN_HEAD_DIM
    n_chunks = rows // C

    @pl.when(pl.program_id(1) == 0)
    def _():
        state[...] = jnp.zeros_like(state)

    small = small_ref[...]
    g_all = -jnp.exp(alog_ref[...]) * _softplus(small + dtb_ref[...])
    beta_all = _sigmoid(small)
    r_id = lax.broadcasted_iota(jnp.int32, (P, P), 0)
    c_id = lax.broadcasted_iota(jnp.int32, (P, P), 1)
    tri_bd = (c_id >= r_id - (r_id & (C - 1))) & (c_id <= r_id)
    diag = r_id == c_id
    tri16 = tri_bd.astype(BF16)
    g_tot, ms, rhss, where = [], [], [], []
    for pr in range(rows // P):
        ps = slice(pr * P, (pr + 1) * P)
        G_all = _exact_dot(tri16, g_all[ps, :])
        G_rows = _rows_as_lanes(G_all, a_lane, SUBLANES)
        G_tot = jnp.concatenate(
            [jnp.broadcast_to(G_all[(c + 1) * C - 1:(c + 1) * C, :], (C, SMALL_W)) for c in range(2)],
            axis=0)
        g_tot.append(G_tot)
        for h in range(GDN_HEADS):
            q = act_ref[ps, h * D:(h + 1) * D]
            k = act_ref[ps, W + h * D:W + (h + 1) * D]
            v = act_ref[ps, 2 * W + h * D:2 * W + (h + 1) * D]
            Gc = G_all[:, a_lane + h:a_lane + h + 1]
            Gr = G_rows[h:h + 1, :]
            Gt = G_tot[:, a_lane + h:a_lane + h + 1]
            beta = beta_all[ps, b_lane + h:b_lane + h + 1]
            gamma = jnp.exp(jnp.where(tri_bd, Gc - Gr, NEG_INF))
            kb = k * beta
            kb16, k16 = kb.astype(BF16), k.astype(BF16)
            ms.append(jnp.where(diag, 0.0, _dot_nt(kb16, k16) * gamma))
            eG = jnp.exp(Gc)
            rhss.append(jnp.concatenate([v * beta, kb * eG], axis=1))
            where.append((h, ps))
            aqk_ref[h, ps, :] = (_dot_nt(q.astype(BF16), k16) * gamma).astype(BF16)
            qd_ref[h, ps, :] = (q * eG).astype(BF16)
            kd_ref[h, ps, :] = (k * jnp.exp(Gt - Gc)).astype(BF16)
    for (h, ps), sol in zip(where, _unit_lower_solves(ms, rhss, P, C)):
        sol_ref[h, ps, :] = sol
    for c in range(n_chunks):
        rs = slice(c * C, (c + 1) * C)
        for h in range(GDN_HEADS):
            ks_ref[h, c] = _dot_tn(kd_ref[h, rs, :], sol_ref[h, rs, :].astype(BF16))

    states = [state[h] for h in range(GDN_HEADS)]
    for c in range(n_chunks):
        rs = slice(c * C, (c + 1) * C)
        first = c % 2 == 0
        for h in range(GDN_HEADS):
            sl = slice(h * D, (h + 1) * D)
            St = states[h]
            St16 = St.astype(BF16)
            decay = jnp.exp(g_tot[c // 2][(c % 2) * C:(c % 2) * C + 1, a_lane + h:a_lane + h + 1])
            states[h] = (decay * St + ks_ref[h, c, :, 0:D]
                         - _dot(ks_ref[h, c, :, D:2 * D].astype(BF16), St16))
            u = sol_ref[h, rs, 0:D] - _dot(sol_ref[h, rs, D:2 * D].astype(BF16), St16)
            u16 = u.astype(BF16)
            o = _dot(qd_ref[h, rs, :], St16)
            if first:
                ubuf[h] = jnp.concatenate([u16, jnp.zeros_like(u16)], axis=0)
                o = o + _dot(aqk_ref[h, rs, 0:C], u16)
            else:
                ubuf[h, C:P, :] = u16
                o = o + _dot(aqk_ref[h, rs, :], ubuf[h])
            o = _rms(o, gain_ref[...]) * gate_ref[rs, sl]
            o_ref[rs, sl] = o.astype(BF16)
    for h in range(GDN_HEADS):
        state[h] = states[h]


def _gdn(act, gate, small, B, S, alog_row, dtb_row, gain_row, a_lane, b_lane, rows=512):
    T = B * S
    W = GROUP_WIDTH
    H, D = GDN_HEADS, GDN_HEAD_DIM
    P = 2 * GDN_CHUNK
    nt = S // rows
    row = lambda b, i: (b * nt + i, 0)
    return pl.pallas_call(
        functools.partial(_gdn_kernel, a_lane=a_lane, b_lane=b_lane, rows=rows),
        grid=(B, nt),
        in_specs=[pl.BlockSpec((rows, 3 * W), row),
                  pl.BlockSpec((rows, W), row),
                  pl.BlockSpec((rows, SMALL_W), row),
                  _resident((1, SMALL_W)),
                  _resident((1, SMALL_W)),
                  _resident((1, D))],
        out_specs=pl.BlockSpec((rows, W), row),
        out_shape=jax.ShapeDtypeStruct((T, W), BF16),
        scratch_shapes=[pltpu.VMEM((H, D, D), F32),
                        pltpu.VMEM((H, rows, 2 * D), F32),
                        pltpu.VMEM((H, rows, D), BF16),
                        pltpu.VMEM((H, rows, D), BF16),
                        pltpu.VMEM((H, rows, P), BF16),
                        pltpu.VMEM((H, rows // GDN_CHUNK, D, 2 * D), F32),
                        pltpu.VMEM((H, P, D), BF16)],
        compiler_params=_params(2),
        name="gdn",
    )(act, gate, small, alog_row, dtb_row, gain_row)


def _block_ref_rows(x, half):
    R, L = x.shape
    if half >= SUBLANES:
        xb = x.reshape(R // (2 * half), 2 * half, L)
        return jnp.broadcast_to(xb[:, half:half + 1, :], xb.shape).reshape(R, L)
    xb = x.reshape(R // SUBLANES, SUBLANES, L)
    sub = lax.broadcasted_iota(jnp.int32, (1, SUBLANES, 1), 1)
    out = None
    for start in range(SUBLANES - 2 * half, -1, -2 * half):
        row = jnp.broadcast_to(xb[:, start + half:start + half + 1, :], xb.shape)
        out = row if out is None else jnp.where(sub < start + 2 * half, row, out)
    return out.reshape(R, L)


def _hgrn_kernel(q_ref, f_ref, k_ref, i_ref, g_ref, gain_ref, o_ref, state, *, rows):
    C = HGRN_CHUNK
    P = 2 * C
    D = HGRN_HEAD_DIM
    halves = [1 << b for b in range(C.bit_length() - 1)]

    @pl.when(pl.program_id(1) == 0)
    def _():
        state[...] = jnp.zeros_like(state)

    r_id = lax.broadcasted_iota(jnp.int32, (P, P), 0)
    c_id = lax.broadcasted_iota(jnp.int32, (P, P), 1)
    tri16 = ((c_id >= r_id - (r_id & (C - 1))) & (c_id <= r_id)).astype(BF16)
    differ = r_id ^ c_id
    row_id = lax.broadcasted_iota(jnp.int32, (P, 1), 0)
    gain = gain_ref[...]
    for pr in range(rows // P):
        ps = slice(pr * P, (pr + 1) * P)
        G_all = _exact_dot(tri16, f_ref[ps, :])
        decays = [jnp.exp(-jnp.abs(G_all - _block_ref_rows(G_all, half))) for half in halves]
        heads = [slice(h * D, (h + 1) * D) for h in range(HGRN_HEADS)]
        scores = [jnp.zeros((P, P), F32) for _ in heads]
        for half, dec in reversed(list(zip(halves, decays))):
            upper = (row_id & half) != 0
            for h, sl in enumerate(heads):
                z16 = (jnp.where(upper, q_ref[ps, sl], k_ref[ps, sl]) * dec[:, sl]).astype(BF16)
                scores[h] = jnp.where(differ < 2 * half, _dot_nt(z16, z16), scores[h])
        for h, sl in enumerate(heads):
            q, k, G = q_ref[ps, sl], k_ref[ps, sl], G_all[:, sl]
            v16 = i_ref[ps, sl].astype(BF16)
            a = jnp.where(differ == 0, _dot_nt(q.astype(BF16), k.astype(BF16)), scores[h])
            a = jnp.where(r_id >= c_id, a, 0.0)
            o_intra = _dot(a.astype(BF16), v16)
            q_in = (q * jnp.exp(G)).astype(BF16)
            for c in range(2):
                cs = slice(c * C, (c + 1) * C)
                G_last = G[(c + 1) * C - 1:(c + 1) * C, :]
                St = state[h]
                o = _dot_nt(q_in[cs, :], St.astype(BF16)) + o_intra[cs, :]
                k_out = (k[cs, :] * jnp.exp(G_last - G[cs, :])).astype(BF16)
                state[h] = jnp.exp(G_last) * St + _dot_tn(v16[cs, :], k_out)
                rs = slice(pr * P + c * C, pr * P + (c + 1) * C)
                o_ref[rs, sl] = (_rms(o, gain) * g_ref[rs, sl]).astype(BF16)


def _hgrn(proj, first_blk, B, S, gain_row, rows=512):
    T = B * S
    W = GROUP_WIDTH
    nt = S // rows
    spec = lambda j: pl.BlockSpec((rows, W), lambda b, i: (b * nt + i, j))
    return pl.pallas_call(
        functools.partial(_hgrn_kernel, rows=rows),
        grid=(B, nt),
        in_specs=[spec(first_blk + j) for j in range(5)] + [_resident((1, HGRN_HEAD_DIM))],
        out_specs=spec(0),
        out_shape=jax.ShapeDtypeStruct((T, W), BF16),
        scratch_shapes=[pltpu.VMEM((HGRN_HEADS, HGRN_HEAD_DIM, HGRN_HEAD_DIM), F32)],
        compiler_params=_params(2),
        name="hgrn2",
    )(proj, proj, proj, proj, proj, gain_row)


def _ssd_kernel(xbc_ref, z_ref, small_ref, alog_ref, dtb_ref, dvec_ref, gain_ref,
                o_ref, state, *, rows):
    L = M2_CHUNK
    W = GROUP_WIDTH
    N = M2_STATE
    pairs_per_group = M2_HEADS // M2_GROUPS // 2

    @pl.when(pl.program_id(1) == 0)
    def _():
        state[...] = jnp.zeros_like(state)

    dt_all = _softplus(small_ref[...] + dtb_ref[...])
    tri = _tril(L)
    tri16 = tri.astype(BF16)
    neg_a = -jnp.exp(alog_ref[...])
    lo = _lane_lo((1, LANES))
    for ck in range(rows // L):
        rs = slice(ck * L, (ck + 1) * L)
        xbc = xbc_ref[rs, :]
        dt = dt_all[rs, :]
        A_cs = _exact_dot(tri16, dt * neg_a)
        A_rows = _rows_as_lanes(A_cs, 0, M2_HEADS)
        for g in range(M2_GROUPS):
            Bg = xbc[:, W + g * N:W + (g + 1) * N]
            Cg = xbc[:, W + M2_GROUPS * N + g * N:W + M2_GROUPS * N + (g + 1) * N]
            cb = _dot_nt(Cg.astype(BF16), Bg.astype(BF16))
            ys = []
            for pp in range(pairs_per_group):
                p = g * pairs_per_group + pp
                xs = xbc[:, p * LANES:(p + 1) * LANES]
                dtl = jnp.where(lo, dt[:, 2 * p:2 * p + 1], dt[:, 2 * p + 1:2 * p + 2])
                X16 = (xs * dtl).astype(BF16)
                St = state[p]
                rhs = jnp.concatenate([X16, St.astype(BF16)], axis=0)
                y_h, st_h = [], []
                for e in range(2):
                    h = 2 * p + e
                    ac = jnp.broadcast_to(A_cs[:, h:h + 1], (L, LANES))
                    ar = A_rows[h:h + 1, :]
                    decay = jnp.exp(jnp.where(tri, ac - ar, NEG_INF))
                    lhs = jnp.concatenate([(cb * decay).astype(BF16),
                                           (Cg * jnp.exp(ac)).astype(BF16)], axis=1)
                    y_h.append(_dot(lhs, rhs))
                    last = ac[L - 1:L, :]
                    st_h.append(jnp.exp(last) * St
                                + _dot_tn((Bg * jnp.exp(last - ac)).astype(BF16), X16))
                state[p] = jnp.where(lo, st_h[0], st_h[1])
                ys.append(jnp.where(lo, y_h[0], y_h[1]) + dvec_ref[:, p * LANES:(p + 1) * LANES] * xs)
            gs = slice(g * (W // M2_GROUPS), (g + 1) * (W // M2_GROUPS))
            y = jnp.concatenate(ys, axis=1) * z_ref[rs, gs]
            o_ref[rs, gs] = _rms(y, gain_ref[:, gs]).astype(BF16)


def _ssd(proj, z_blk, small_blk, B, S, alog_row, dtb_row, dvec, gain_row, rows=512):
    T = B * S
    W = GROUP_WIDTH
    CW = W + 2 * M2_GROUPS * M2_STATE
    nt = S // rows
    row = lambda b, i: b * nt + i
    return pl.pallas_call(
        functools.partial(_ssd_kernel, rows=rows),
        grid=(B, nt),
        in_specs=[pl.BlockSpec((rows, CW), lambda b, i: (row(b, i), 0)),
                  pl.BlockSpec((rows, W), lambda b, i: (row(b, i), z_blk)),
                  pl.BlockSpec((rows, SMALL_W), lambda b, i: (row(b, i), small_blk)),
                  _resident((1, SMALL_W)),
                  _resident((1, SMALL_W)),
                  _resident((1, W)),
                  _resident((1, W))],
        out_specs=pl.BlockSpec((rows, W), lambda b, i: (row(b, i), 0)),
        out_shape=jax.ShapeDtypeStruct((T, W), BF16),
        scratch_shapes=[pltpu.VMEM((M2_HEADS // 2, M2_STATE, LANES), F32)],
        compiler_params=_params(2),
        name="ssd",
    )(proj, proj, proj, alog_row, dtb_row, dvec, gain_row)


def _pad_lanes(v, first, width=SMALL_W):
    v = v.astype(F32)
    return jnp.pad(v, (first, width - first - v.shape[0])).reshape(1, width)


def kernel(x, norm_gains, w_out, ffn_w_up, ffn_conv_w, ffn_conv_b, ffn_w_down,
           even_w_in, fox_f_bias, gdn_conv_w, gdn_A_log, gdn_dt_bias, gdn_norm_gain,
           odd_w_in, hgrn_lb_logits, hgrn_norm_gain, m2_conv_w, m2_conv_b,
           m2_A_log, m2_dt_bias, m2_D, m2_norm_gain):
    B, S, D = x.shape
    assert D == D_MODEL and S % 512 == 0
    T = B * S
    W = GROUP_WIDTH
    tm = 512
    tm_in = 256
    tm_ffn = 512
    row = lambda v: v.astype(F32).reshape(1, -1)
    x2 = x.reshape(T, D).astype(F32)
    w_out16, w_up16, w_down16 = (w.astype(BF16) for w in (w_out, ffn_w_up, ffn_w_down))
    conv_w32 = ffn_conv_w.astype(F32)
    conv_b32 = ffn_conv_b.astype(F32).reshape(ffn_conv_b.shape[0], 1, -1)

    o_ff = 3 * W
    o_qkv = o_ff + FOX_HEADS
    o_a = o_qkv + 3 * W
    o_b = o_a + GDN_HEADS
    o_gate = o_b + GDN_HEADS
    a_lane, b_lane = FOX_HEADS, FOX_HEADS + GDN_HEADS
    pad = jnp.zeros((D, SMALL_W - FOX_HEADS - 2 * GDN_HEADS), BF16)
    w16 = even_w_in.astype(BF16)
    w_even = jnp.concatenate([w16[:, :o_ff], w16[:, o_qkv:o_a], w16[:, o_gate:],
                              w16[:, o_ff:o_qkv], w16[:, o_a:o_gate], pad], axis=1)
    g = norm_gains[0]
    qkv, gdn_act, gdn_gate, small = _inproj_even(x2, row(g[0]), w_even, gdn_conv_w.astype(F32),
                                                 tm_in, S // tm_in)
    fox_blk = 256
    qt, ka, vt = _foxgate(qkv, small, B, S, 0, _pad_lanes(fox_f_bias, 0), fox_blk)
    o_fox = _fox(qt, ka, vt, B, S, fox_blk)
    o_gdn = _gdn(gdn_act, gdn_gate, small, B, S, _pad_lanes(gdn_A_log, a_lane),
                 _pad_lanes(gdn_dt_bias, a_lane), row(gdn_norm_gain), a_lane, b_lane)
    x2 = _mix_ffn(o_fox, o_gdn, w_out16, x2, B, S, row(g[1]), row(g[2]), w_up16, conv_w32, conv_b32,
                  w_down16, row(g[3]), 0, tm_ffn)

    assert hgrn_lb_logits.shape == (2, W)
    CW = W + 2 * M2_GROUPS * M2_STATE
    pad = jnp.zeros((D, SMALL_W - M2_HEADS), BF16)
    w16 = odd_w_in.astype(BF16)
    w_odd = jnp.concatenate([w16[:, 5 * W:5 * W + CW], w16[:, :5 * W], w16[:, 5 * W + CW:], pad],
                            axis=1)
    g = norm_gains[1]
    proj = _inproj_odd(x2, row(g[0]), w_odd, hgrn_lb_logits.astype(F32), m2_conv_w.astype(F32),
                       row(m2_conv_b), tm, S // tm)
    o_hgrn = _hgrn(proj, CW // W, B, S, row(hgrn_norm_gain))
    o_ssd = _ssd(proj, (CW + 5 * W) // W, (CW + 6 * W) // SMALL_W, B, S, _pad_lanes(m2_A_log, 0),
                 _pad_lanes(m2_dt_bias, 0), row(jnp.repeat(m2_D, M2_HEAD_DIM)), row(m2_norm_gain))
    x2 = _mix_ffn(o_hgrn, o_ssd, w_out16, x2, B, S, row(g[1]), row(g[2]), w_up16, conv_w32, conv_b32,
                  w_down16, row(g[3]), 1, tm_ffn)
    return x2.reshape(B, S, D).astype(x.dtype)
```

```python
import functools

import jax
import jax.numpy as jnp
from jax import lax
from jax.experimental import pallas as pl
from jax.experimental.pallas import tpu as pltpu

F32 = jnp.float32
BF16 = jnp.bfloat16

D_MODEL = 1024
GROUP_WIDTH = D_MODEL // 2
FOX_HEAD_DIM = 64
FOX_HEADS = GROUP_WIDTH // FOX_HEAD_DIM
FOX_AUG = 3
GDN_HEAD_DIM = 128
GDN_HEADS = GROUP_WIDTH // GDN_HEAD_DIM
GDN_CHUNK = 64
SHORT_CONV = 4
HGRN_HEAD_DIM = 128
HGRN_HEADS = GROUP_WIDTH // HGRN_HEAD_DIM
HGRN_CHUNK = 64
M2_HEAD_DIM = 64
M2_HEADS = GROUP_WIDTH // M2_HEAD_DIM
M2_GROUPS = 2
M2_STATE = 128
M2_CHUNK = 128
D_FF = 2816
FFN_CONV = 3
NORM_EPS = 1e-6

LANES = 128
SUBLANES = 8
SMALL_W = LANES
VMEM_LIMIT = 56 * 1024 * 1024

NEG_INF = float("-inf")


def _dot(a, b):
    return jnp.dot(a, b, preferred_element_type=F32)


def _dot_nt(a, b):
    return lax.dot_general(a, b, (((1,), (1,)), ((), ())), preferred_element_type=F32)


def _dot_tn(a, b):
    return lax.dot_general(a, b, (((0,), (0,)), ((), ())), preferred_element_type=F32)


def _bf16_pieces(x):
    hi = x.astype(BF16)
    rest = x - hi.astype(F32)
    mid = rest.astype(BF16)
    return hi, mid, (rest - mid.astype(F32)).astype(BF16)


def _exact_dot(sel16, x):
    hi, mid, lo = _bf16_pieces(x)
    return _dot(sel16, hi) + _dot(sel16, mid) + _dot(sel16, lo)


def _rms(x, gain):
    return x * lax.rsqrt(jnp.mean(x * x, axis=-1, keepdims=True) + NORM_EPS) * gain


def _sigmoid(x):
    return 1.0 / (1.0 + jnp.exp(-x))


def _silu(x):
    h = 0.5 * x
    return h * jnp.tanh(h) + h


def _softplus(x):
    return jnp.maximum(x, 0.0) + jnp.log1p(jnp.exp(-jnp.abs(x)))


def _log_sigmoid(x):
    return jnp.minimum(x, 0.0) - jnp.log1p(jnp.exp(-jnp.abs(x)))


def _tril(n, strict=False):
    r = lax.broadcasted_iota(jnp.int32, (n, n), 0)
    c = lax.broadcasted_iota(jnp.int32, (n, n), 1)
    return (r > c) if strict else (r >= c)


def _lane_lo(shape):
    return lax.broadcasted_iota(jnp.int32, shape, len(shape) - 1) < (LANES // 2)


def _rows_as_lanes(cols, first, n):
    sel = (lax.broadcasted_iota(jnp.int32, (n, LANES), 1)
           == lax.broadcasted_iota(jnp.int32, (n, LANES), 0) + first).astype(BF16)
    hi, mid, lo = _bf16_pieces(cols)
    return _dot_nt(sel, hi) + _dot_nt(sel, mid) + _dot_nt(sel, lo)


def _params(n_grid):
    return pltpu.CompilerParams(dimension_semantics=("arbitrary",) * n_grid,
                                vmem_limit_bytes=VMEM_LIMIT)


def _resident(shape):
    nd = len(shape)
    return pl.BlockSpec(shape, lambda *_: (0,) * nd)


def _conv_tail_reset(tail, tiles_per_seq):
    @pl.when(pl.program_id(0) % tiles_per_seq == 0)
    def _():
        tail[...] = jnp.zeros_like(tail)


def _conv_chunk(p, tail, w_ref, cols, width):
    rows = p.shape[0]
    ext = jnp.concatenate([tail[:, cols], p], axis=0)
    tail[:, cols] = p[rows - SUBLANES:rows, :]
    y = None
    for k in range(width):
        start = SUBLANES - (width - 1) + k
        term = ext[start:start + rows, :] * w_ref[k:k + 1, cols]
        y = term if y is None else y + term
    return y


def _staggered(stages, lag=2):
    pending = []
    for produce, consume in stages:
        pending.append((produce(), consume))
        if len(pending) > lag:
            value, done = pending.pop(0)
            done(value)
    for value, done in pending:
        done(value)


def _inproj_even_kernel(x_ref, g_ref, w_ref, cw_ref, qkv_ref, act_ref, gate_ref, small_ref, tail,
                        *, tm, tiles_per_seq):
    W = GROUP_WIDTH
    D = GDN_HEAD_DIM
    _conv_tail_reset(tail, tiles_per_seq)
    hn = _rms(x_ref[...], g_ref[...]).astype(BF16)
    proj = lambda c0, n: (lambda: _dot(hn, w_ref[:, c0:c0 + n]))

    PW = 2 * D

    def fox_out(c0):
        def consume(p):
            qkv_ref[:, c0:c0 + PW] = p.astype(BF16)
        return consume

    def gdn_out(c0):
        def consume(p):
            a = _silu(_conv_chunk(p, tail, cw_ref, slice(c0, c0 + PW), SHORT_CONV))
            for h in range(PW // D):
                ah = a[:, h * D:(h + 1) * D]
                if c0 < 2 * W:
                    ah = ah * lax.rsqrt(jnp.sum(ah * ah, axis=-1, keepdims=True) + NORM_EPS)
                if c0 < W:
                    ah = ah * D ** -0.5
                act_ref[:, c0 + h * D:c0 + (h + 1) * D] = ah
        return consume

    def gate_out(c0):
        def consume(p):
            gate_ref[:, c0:c0 + PW] = _silu(p)
        return consume

    def small_out(p):
        small_ref[...] = p

    stages = []
    for c0 in range(0, 3 * W, PW):
        stages += [(proj(3 * W + c0, PW), gdn_out(c0)), (proj(c0, PW), fox_out(c0))]
    stages += [(proj(6 * W + c0, PW), gate_out(c0)) for c0 in range(0, W, PW)]
    stages += [(proj(7 * W, SMALL_W), small_out)]
    _staggered(stages)


def _inproj_even(x2, gain, w, conv_w, tm, tiles_per_seq):
    T = x2.shape[0]
    W = GROUP_WIDTH
    widths = (3 * W, 3 * W, W, SMALL_W)
    dtypes = (BF16, F32, F32, F32)
    return pl.pallas_call(
        functools.partial(_inproj_even_kernel, tm=tm, tiles_per_seq=tiles_per_seq),
        grid=(T // tm,),
        in_specs=[pl.BlockSpec((tm, D_MODEL), lambda i: (i, 0)),
                  _resident((1, D_MODEL)),
                  _resident(w.shape),
                  _resident(conv_w.shape)],
        out_specs=[pl.BlockSpec((tm, n), lambda i: (i, 0)) for n in widths],
        out_shape=[jax.ShapeDtypeStruct((T, n), dt) for n, dt in zip(widths, dtypes)],
        scratch_shapes=[pltpu.VMEM((SUBLANES, 3 * W), F32)],
        compiler_params=_params(1),
        name="inproj_even",
    )(x2, gain, w, conv_w)


def _inproj_odd_kernel(x_ref, g_ref, w_ref, lbl_ref, cw_ref, cb_ref, o_ref, tail,
                       *, tm, tiles_per_seq):
    W = GROUP_WIDTH
    CW = W + 2 * M2_GROUPS * M2_STATE
    _conv_tail_reset(tail, tiles_per_seq)
    hn = _rms(x_ref[...], g_ref[...]).astype(BF16)
    proj = lambda c0, n: (lambda: _dot(hn, w_ref[:, c0:c0 + n]))

    logits = lbl_ref[...]
    e = jnp.exp(logits - jnp.max(logits, axis=0, keepdims=True))
    prob = e / jnp.sum(e, axis=0, keepdims=True)
    lb = (prob[0:1, :] + prob[1:2, :]) - prob[0:1, :]

    def conv_out(c):
        def consume(p):
            cols = slice(c * W, (c + 1) * W)
            o_ref[:, cols] = _silu(_conv_chunk(p, tail, cw_ref, cols, SHORT_CONV) + cb_ref[:, cols])
        return consume

    def mapped_out(col, fn):
        def consume(p):
            o_ref[:, col:col + p.shape[1]] = fn(p)
        return consume

    def forget_out(col):
        def consume(p):
            gate = _sigmoid(p)
            o_ref[:, col:col + W] = jnp.log(lb + (1.0 - lb) * gate)
            o_ref[:, col + W:col + 2 * W] = (1.0 - lb) * (1.0 - gate)
        return consume

    keep = lambda p: p
    stages = [(proj(0, W), conv_out(0)),
              (proj(CW + 2 * W, W), mapped_out(CW + 3 * W, keep)),
              (proj(W, W), conv_out(1)),
              (proj(CW, W), mapped_out(CW, _silu)),
              (proj(CW + W, W), forget_out(CW + W)),
              (proj(CW + 3 * W, W), mapped_out(CW + 4 * W, _silu)),
              (proj(CW + 4 * W, W), mapped_out(CW + 5 * W, _silu)),
              (proj(CW + 5 * W, SMALL_W), mapped_out(CW + 6 * W, keep))]
    _staggered(stages)


def _inproj_odd(x2, gain, w, lb_logits, conv_w, conv_b, tm, tiles_per_seq):
    T = x2.shape[0]
    W = GROUP_WIDTH
    CW = W + 2 * M2_GROUPS * M2_STATE
    n_out = CW + 6 * W + SMALL_W
    return pl.pallas_call(
        functools.partial(_inproj_odd_kernel, tm=tm, tiles_per_seq=tiles_per_seq),
        grid=(T // tm,),
        in_specs=[pl.BlockSpec((tm, D_MODEL), lambda i: (i, 0)),
                  _resident((1, D_MODEL)),
                  _resident(w.shape),
                  _resident(lb_logits.shape),
                  _resident(conv_w.shape),
                  _resident(conv_b.shape)],
        out_specs=pl.BlockSpec((tm, n_out), lambda i: (i, 0)),
        out_shape=jax.ShapeDtypeStruct((T, n_out), F32),
        scratch_shapes=[pltpu.VMEM((SUBLANES, CW), F32)],
        compiler_params=_params(1),
        name="inproj_odd",
    )(x2, gain, w, lb_logits, conv_w, conv_b)


def _mix_ffn_kernel(ma_ref, mb_ref, wo_ref, x_ref, gmix_ref, gpre_ref, wup_ref, cw_ref, cb_ref,
                    wdn_ref, gpost_ref, o_ref, carry, act, *, tm, fc, group):
    W = GROUP_WIDTH
    n_chunks = D_FF // fc

    @pl.when(pl.program_id(1) == 0)
    def _():
        carry[...] = jnp.zeros_like(carry)

    mix = _dot(ma_ref[...], wo_ref[0:W, :]) + _dot(mb_ref[...], wo_ref[W:2 * W, :])
    x = x_ref[...] + _rms(mix, gmix_ref[...])
    hn = _rms(x, gpre_ref[...]).astype(BF16)
    y = None
    for c in range(n_chunks):
        halves = []
        for half in range(2):
            cols = slice(half * D_FF + c * fc, half * D_FF + (c + 1) * fc)
            u = _dot(hn, wup_ref[:, cols])
            halves.append(_conv_chunk(u, carry, cw_ref, cols, FFN_CONV) + cb_ref[:, cols])
        act[:, c * fc:(c + 1) * fc] = (_silu(halves[0]) * halves[1]).astype(BF16)
        if (c + 1) % group == 0 or c + 1 == n_chunks:
            rows = slice((c // group) * group * fc, (c + 1) * fc)
            part = _dot(act[:, rows], wdn_ref[rows, :])
            y = part if y is None else y + part
    o_ref[...] = x + _rms(y, gpost_ref[...])


def _mix_ffn(mix_a, mix_b, w_out, x2, B, S, gmix, gpre, w_up, conv_w, conv_b, w_down, gpost,
             layer, tm, fc=256):
    W = GROUP_WIDTH
    nt = S // tm
    row = lambda b, i: (b * nt + i, 0)
    slab = lambda *dims: pl.BlockSpec((None,) + dims, lambda *_: (layer,) + (0,) * len(dims))
    return pl.pallas_call(
        functools.partial(_mix_ffn_kernel, tm=tm, fc=fc, group=6),
        grid=(B, nt),
        in_specs=[pl.BlockSpec((tm, W), row),
                  pl.BlockSpec((tm, W), row),
                  slab(D_MODEL, D_MODEL),
                  pl.BlockSpec((tm, D_MODEL), row),
                  _resident((1, D_MODEL)),
                  _resident((1, D_MODEL)),
                  slab(D_MODEL, 2 * D_FF),
                  slab(FFN_CONV, 2 * D_FF),
                  slab(1, 2 * D_FF),
                  slab(D_FF, D_MODEL),
                  _resident((1, D_MODEL))],
        out_specs=pl.BlockSpec((tm, D_MODEL), row),
        out_shape=jax.ShapeDtypeStruct(x2.shape, F32),
        scratch_shapes=[pltpu.VMEM((SUBLANES, 2 * D_FF), F32),
                        pltpu.VMEM((tm, D_FF), BF16)],
        compiler_params=_params(2),
        name="mixffn",
    )(mix_a, mix_b, w_out, x2, gmix, gpre, w_up, conv_w, conv_b, w_down, gpost)


def _foxgate_kernel(s_ref, b_ref, qk_ref, qt_ref, ka_ref, vt_ref, carry, *, blk, nsub):
    W = GROUP_WIDTH

    @pl.when(pl.program_id(1) == 0)
    def _():
        carry[...] = jnp.zeros_like(carry)

    lane = lax.broadcasted_iota(jnp.int32, (1, LANES), 1)
    lo = lane < FOX_HEAD_DIM
    ones = jnp.where(lane < FOX_HEAD_DIM + FOX_AUG, 1.0, 0.0)
    scale = FOX_HEAD_DIM ** -0.5
    tri16 = _tril(blk).astype(BF16)
    total = carry[...]
    for sb in range(nsub):
        rs = slice(sb * blk, (sb + 1) * blk)
        z = s_ref[rs, :] + b_ref[...]
        cs = _exact_dot(tri16, _log_sigmoid(z)) + total
        total = cs[blk - 1:blk, :]
        for p in range(FOX_HEADS // 2):
            q = qk_ref[rs, p * LANES:(p + 1) * LANES].astype(F32) * scale
            k = qk_ref[rs, W + p * LANES:W + (p + 1) * LANES].astype(F32)
            for e in range(2):
                h = 2 * p + e
                qe = pltpu.roll(q, FOX_HEAD_DIM, 1) if e else q
                ke = pltpu.roll(k, FOX_HEAD_DIM, 1) if e else k
                rem = -cs[:, h:h + 1]
                aug = jnp.zeros((blk, LANES), F32)
                for piece in range(FOX_AUG):
                    part = rem.astype(BF16).astype(F32)
                    aug = jnp.where(lane == FOX_HEAD_DIM + piece, part, aug)
                    rem = rem - part
                qt_ref[0, sb, h * LANES:(h + 1) * LANES, :] = jnp.where(lo, qe, ones).T.astype(BF16)
                ka_ref[rs, h * LANES:(h + 1) * LANES] = jnp.where(lo, ke, aug).astype(BF16)
            v = qk_ref[rs, 2 * W + p * LANES:2 * W + (p + 1) * LANES].astype(F32)
            vt_ref[0, sb, p * LANES:(p + 1) * LANES, :] = v.T.astype(BF16)
    carry[...] = total


def _foxgate(qkv, small, B, S, small_blk, bias_row, blk, nsub=2):
    T = B * S
    W = GROUP_WIDTH
    nb = S // blk
    ns = nb // nsub
    rows = nsub * blk
    wide = FOX_HEADS * LANES
    row = lambda b, j: b * ns + j
    return pl.pallas_call(
        functools.partial(_foxgate_kernel, blk=blk, nsub=nsub),
        grid=(B, ns),
        in_specs=[pl.BlockSpec((rows, SMALL_W), lambda b, j: (row(b, j), small_blk)),
                  _resident((1, SMALL_W)),
                  pl.BlockSpec((rows, 3 * W), lambda b, j: (row(b, j), 0))],
        out_specs=[pl.BlockSpec((1, nsub, wide, blk), lambda b, j: (b, j, 0, 0)),
                   pl.BlockSpec((rows, wide), lambda b, j: (row(b, j), 0)),
                   pl.BlockSpec((1, nsub, W, blk), lambda b, j: (b, j, 0, 0))],
        out_shape=[jax.ShapeDtypeStruct((B, nb, wide, blk), BF16),
                   jax.ShapeDtypeStruct((T, wide), BF16),
                   jax.ShapeDtypeStruct((B, nb, W, blk), BF16)],
        scratch_shapes=[pltpu.VMEM((1, SMALL_W), F32)],
        compiler_params=_params(2),
        name="foxgate",
    )(small, bias_row, qkv)


def _fox_kernel(qt_ref, k_ref, vt_ref, o_ref, m_ref, l_ref, acc_ref, *, blk, kvb):
    i = pl.program_id(1)
    kv_id = lax.broadcasted_iota(jnp.int32, (kvb, blk), 0)
    q_id = lax.broadcasted_iota(jnp.int32, (kvb, blk), 1)
    first_head = lax.broadcasted_iota(jnp.int32, (LANES, 1), 0) < FOX_HEAD_DIM
    m_ref[...] = jnp.full(m_ref.shape, NEG_INF, F32)
    l_ref[...] = jnp.zeros_like(l_ref)
    acc_ref[...] = jnp.zeros_like(acc_ref)

    def sub_block(j, sub, masked):
        off = pl.multiple_of(j * blk + sub * kvb, kvb)
        for p in range(FOX_HEADS // 2):
            vt = vt_ref[0, j, p * LANES:(p + 1) * LANES, sub * kvb:(sub + 1) * kvb]
            alpha, pv = [], []
            for e in range(2):
                h = 2 * p + e
                hs = slice(h * LANES, (h + 1) * LANES)
                s = _dot(k_ref[pl.ds(off, kvb), hs], qt_ref[0, 0, hs, :])
                if masked:
                    s = jnp.where(kv_id + sub * kvb <= q_id, s, NEG_INF)
                m_old = m_ref[h:h + 1, :]
                m_new = jnp.maximum(m_old, jnp.max(s, axis=0, keepdims=True))
                pe = jnp.exp(s - m_new)
                a = jnp.exp(m_old - m_new)
                m_ref[h:h + 1, :] = m_new
                l_ref[h:h + 1, :] = a * l_ref[h:h + 1, :] + jnp.sum(pe, axis=0, keepdims=True)
                alpha.append(a)
                pv.append(_dot(vt, pe.astype(BF16)))
            acc_ref[p] = (jnp.where(first_head, alpha[0], alpha[1]) * acc_ref[p]
                          + jnp.where(first_head, pv[0], pv[1]))

    def block(j, masked):
        for sub in range(blk // kvb):
            sub_block(j, sub, masked)

    def body(j, carry):
        block(j, False)
        return carry

    lax.fori_loop(0, i, body, 0)
    block(i, True)
    for p in range(FOX_HEADS // 2):
        l = jnp.where(first_head, l_ref[2 * p:2 * p + 1, :], l_ref[2 * p + 1:2 * p + 2, :])
        o_ref[:, p * LANES:(p + 1) * LANES] = (acc_ref[p] / l).T.astype(BF16)


def _fox(qt, ka, vt, B, S, blk):
    T = B * S
    nq = S // blk
    W = GROUP_WIDTH
    wide = FOX_HEADS * LANES
    return pl.pallas_call(
        functools.partial(_fox_kernel, blk=blk, kvb=128),
        grid=(B, nq),
        in_specs=[pl.BlockSpec((1, 1, wide, blk), lambda b, i: (b, i, 0, 0)),
                  pl.BlockSpec((S, wide), lambda b, i: (b, 0)),
                  pl.BlockSpec((1, nq, W, blk), lambda b, i: (b, 0, 0, 0))],
        out_specs=pl.BlockSpec((blk, W), lambda b, i: (b * nq + i, 0)),
        out_shape=jax.ShapeDtypeStruct((T, W), BF16),
        scratch_shapes=[pltpu.VMEM((FOX_HEADS, blk), F32),
                        pltpu.VMEM((FOX_HEADS, blk), F32),
                        pltpu.VMEM((FOX_HEADS // 2, LANES, blk), F32)],
        compiler_params=_params(2),
        name="fox",
    )(qt, ka, vt)


def _unit_lower_solves(ms, rhss, n, nilpotent):
    eye = (lax.broadcasted_iota(jnp.int32, (n, n), 0)
           == lax.broadcasted_iota(jnp.int32, (n, n), 1)).astype(F32)
    xs = [-m for m in ms]
    sols = rhss
    power = 1
    while power < nilpotent:
        x16 = [x.astype(BF16) for x in xs]
        x2 = [_dot(x, x) for x in x16]
        x2_16 = [x.astype(BF16) for x in x2]
        factors = [eye + x + y + _dot(xb, yb) for x, y, xb, yb in zip(xs, x2, x16, x2_16)]
        sols = [_dot(f.astype(BF16), r.astype(BF16)) for f, r in zip(factors, sols)]
        power *= 4
        if power < nilpotent:
            xs = [_dot(y, y) for y in x2_16]
    assert power == nilpotent
    return sols


def _gdn_kernel(act_ref, gate_ref, small_ref, alog_ref, dtb_ref, gain_ref, o_ref,
                state, sol_ref, qd_ref, kd_ref, aqk_ref, ks_ref, ubuf, *, a_lane, b_lane, rows):
    C = GDN_CHUNK
    P = 2 * C
    W = GROUP_WIDTH
    D = GDN_HEAD_DIM
    n_chunks = rows // C

    @pl.when(pl.program_id(1) == 0)
    def _():
        state[...] = jnp.zeros_like(state)

    small = small_ref[...]
    g_all = -jnp.exp(alog_ref[...]) * _softplus(small + dtb_ref[...])
    beta_all = _sigmoid(small)
    r_id = lax.broadcasted_iota(jnp.int32, (P, P), 0)
    c_id = lax.broadcasted_iota(jnp.int32, (P, P), 1)
    tri_bd = (c_id >= r_id - (r_id & (C - 1))) & (c_id <= r_id)
    diag = r_id == c_id
    tri16 = tri_bd.astype(BF16)
    g_tot, ms, rhss, where = [], [], [], []
    for pr in range(rows // P):
        ps = slice(pr * P, (pr + 1) * P)
        G_all = _exact_dot(tri16, g_all[ps, :])
        G_rows = _rows_as_lanes(G_all, a_lane, SUBLANES)
        G_tot = jnp.concatenate(
            [jnp.broadcast_to(G_all[(c + 1) * C - 1:(c + 1) * C, :], (C, SMALL_W)) for c in range(2)],
            axis=0)
        g_tot.append(G_tot)
        for h in range(GDN_HEADS):
            q = act_ref[ps, h * D:(h + 1) * D]
            k = act_ref[ps, W + h * D:W + (h + 1) * D]
            v = act_ref[ps, 2 * W + h * D:2 * W + (h + 1) * D]
            Gc = G_all[:, a_lane + h:a_lane + h + 1]
            Gr = G_rows[h:h + 1, :]
            Gt = G_tot[:, a_lane + h:a_lane + h + 1]
            beta = beta_all[ps, b_lane + h:b_lane + h + 1]
            gamma = jnp.exp(jnp.where(tri_bd, Gc - Gr, NEG_INF))
            kb = k * beta
            kb16, k16 = kb.astype(BF16), k.astype(BF16)
            kq = _dot_nt(jnp.concatenate([kb16, q.astype(BF16)], axis=0), k16)
            ms.append(jnp.where(diag, 0.0, kq[0:P, :] * gamma))
            eG = jnp.exp(Gc)
            rhss.append(jnp.concatenate([v * beta, kb * eG], axis=1))
            where.append((h, ps))
            aqk_ref[h, ps, :] = (kq[P:2 * P, :] * gamma).astype(BF16)
            qd_ref[h, ps, :] = (q * eG).astype(BF16)
            kd_ref[h, ps, :] = (k * jnp.exp(Gt - Gc)).astype(BF16)
    for (h, ps), sol in zip(where, _unit_lower_solves(ms, rhss, P, C)):
        sol_ref[h, ps, :] = sol
    for c in range(n_chunks):
        rs = slice(c * C, (c + 1) * C)
        for h in range(GDN_HEADS):
            ks_ref[h, c] = _dot_tn(kd_ref[h, rs, :], sol_ref[h, rs, :].astype(BF16))

    states = [state[h] for h in range(GDN_HEADS)]
    for c in range(n_chunks):
        rs = slice(c * C, (c + 1) * C)
        first = c % 2 == 0
        for h in range(GDN_HEADS):
            sl = slice(h * D, (h + 1) * D)
            St = states[h]
            St16 = St.astype(BF16)
            decay = jnp.exp(g_tot[c // 2][(c % 2) * C:(c % 2) * C + 1, a_lane + h:a_lane + h + 1])
            states[h] = (decay * St + ks_ref[h, c, :, 0:D]
                         - _dot(ks_ref[h, c, :, D:2 * D].astype(BF16), St16))
            u = sol_ref[h, rs, 0:D] - _dot(sol_ref[h, rs, D:2 * D].astype(BF16), St16)
            u16 = u.astype(BF16)
            o = _dot(qd_ref[h, rs, :], St16)
            if first:
                ubuf[h] = jnp.concatenate([u16, jnp.zeros_like(u16)], axis=0)
                o = o + _dot(aqk_ref[h, rs, 0:C], u16)
            else:
                ubuf[h, C:P, :] = u16
                o = o + _dot(aqk_ref[h, rs, :], ubuf[h])
            o = _rms(o, gain_ref[...]) * gate_ref[rs, sl]
            o_ref[rs, sl] = o.astype(BF16)
    for h in range(GDN_HEADS):
        state[h] = states[h]


def _gdn(act, gate, small, B, S, alog_row, dtb_row, gain_row, a_lane, b_lane, rows=512):
    T = B * S
    W = GROUP_WIDTH
    H, D = GDN_HEADS, GDN_HEAD_DIM
    P = 2 * GDN_CHUNK
    nt = S // rows
    row = lambda b, i: (b * nt + i, 0)
    return pl.pallas_call(
        functools.partial(_gdn_kernel, a_lane=a_lane, b_lane=b_lane, rows=rows),
        grid=(B, nt),
        in_specs=[pl.BlockSpec((rows, 3 * W), row),
                  pl.BlockSpec((rows, W), row),
                  pl.BlockSpec((rows, SMALL_W), row),
                  _resident((1, SMALL_W)),
                  _resident((1, SMALL_W)),
                  _resident((1, D))],
        out_specs=pl.BlockSpec((rows, W), row),
        out_shape=jax.ShapeDtypeStruct((T, W), BF16),
        scratch_shapes=[pltpu.VMEM((H, D, D), F32),
                        pltpu.VMEM((H, rows, 2 * D), F32),
                        pltpu.VMEM((H, rows, D), BF16),
                        pltpu.VMEM((H, rows, D), BF16),
                        pltpu.VMEM((H, rows, P), BF16),
                        pltpu.VMEM((H, rows // GDN_CHUNK, D, 2 * D), F32),
                        pltpu.VMEM((H, P, D), BF16)],
        compiler_params=_params(2),
        name="gdn",
    )(act, gate, small, alog_row, dtb_row, gain_row)


def _block_ref_rows(x, half):
    R, L = x.shape
    if half >= SUBLANES:
        xb = x.reshape(R // (2 * half), 2 * half, L)
        return jnp.broadcast_to(xb[:, half:half + 1, :], xb.shape).reshape(R, L)
    xb = x.reshape(R // SUBLANES, SUBLANES, L)
    sub = lax.broadcasted_iota(jnp.int32, (1, SUBLANES, 1), 1)
    out = None
    for start in range(SUBLANES - 2 * half, -1, -2 * half):
        row = jnp.broadcast_to(xb[:, start + half:start + half + 1, :], xb.shape)
        out = row if out is None else jnp.where(sub < start + 2 * half, row, out)
    return out.reshape(R, L)


def _hgrn_kernel(q_ref, f_ref, k_ref, i_ref, g_ref, gain_ref, o_ref, state, *, rows):
    C = HGRN_CHUNK
    P = 2 * C
    D = HGRN_HEAD_DIM
    halves = [1 << b for b in range(C.bit_length() - 1)]

    @pl.when(pl.program_id(1) == 0)
    def _():
        state[...] = jnp.zeros_like(state)

    r_id = lax.broadcasted_iota(jnp.int32, (P, P), 0)
    c_id = lax.broadcasted_iota(jnp.int32, (P, P), 1)
    tri16 = ((c_id >= r_id - (r_id & (C - 1))) & (c_id <= r_id)).astype(BF16)
    differ = r_id ^ c_id
    row_id = lax.broadcasted_iota(jnp.int32, (P, 1), 0)
    gain = gain_ref[...]
    for pr in range(rows // P):
        ps = slice(pr * P, (pr + 1) * P)
        G_all = _exact_dot(tri16, f_ref[ps, :])
        decays = [jnp.exp(-jnp.abs(G_all - _block_ref_rows(G_all, half))) for half in halves]
        heads = [slice(h * D, (h + 1) * D) for h in range(HGRN_HEADS)]
        scores = [jnp.zeros((P, P), F32) for _ in heads]
        for half, dec in reversed(list(zip(halves, decays))):
            upper = (row_id & half) != 0
            for h, sl in enumerate(heads):
                z16 = (jnp.where(upper, q_ref[ps, sl], k_ref[ps, sl]) * dec[:, sl]).astype(BF16)
                scores[h] = jnp.where(differ < 2 * half, _dot_nt(z16, z16), scores[h])
        for h, sl in enumerate(heads):
            q, k, G = q_ref[ps, sl], k_ref[ps, sl], G_all[:, sl]
            v16 = i_ref[ps, sl].astype(BF16)
            a = jnp.where(differ == 0, _dot_nt(q.astype(BF16), k.astype(BF16)), scores[h])
            a = jnp.where(r_id >= c_id, a, 0.0)
            o_intra = _dot(a.astype(BF16), v16)
            q_in = (q * jnp.exp(G)).astype(BF16)
            for c in range(2):
                cs = slice(c * C, (c + 1) * C)
                G_last = G[(c + 1) * C - 1:(c + 1) * C, :]
                St = state[h]
                o = _dot_nt(q_in[cs, :], St.astype(BF16)) + o_intra[cs, :]
                k_out = (k[cs, :] * jnp.exp(G_last - G[cs, :])).astype(BF16)
                state[h] = jnp.exp(G_last) * St + _dot_tn(v16[cs, :], k_out)
                rs = slice(pr * P + c * C, pr * P + (c + 1) * C)
                o_ref[rs, sl] = (_rms(o, gain) * g_ref[rs, sl]).astype(BF16)


def _hgrn(proj, first_blk, B, S, gain_row, rows=512):
    T = B * S
    W = GROUP_WIDTH
    nt = S // rows
    spec = lambda j: pl.BlockSpec((rows, W), lambda b, i: (b * nt + i, j))
    return pl.pallas_call(
        functools.partial(_hgrn_kernel, rows=rows),
        grid=(B, nt),
        in_specs=[spec(first_blk + j) for j in range(5)] + [_resident((1, HGRN_HEAD_DIM))],
        out_specs=spec(0),
        out_shape=jax.ShapeDtypeStruct((T, W), BF16),
        scratch_shapes=[pltpu.VMEM((HGRN_HEADS, HGRN_HEAD_DIM, HGRN_HEAD_DIM), F32)],
        compiler_params=_params(2),
        name="hgrn2",
    )(proj, proj, proj, proj, proj, gain_row)


def _ssd_kernel(xbc_ref, z_ref, small_ref, alog_ref, dtb_ref, dvec_ref, gain_ref,
                o_ref, state, *, rows):
    L = M2_CHUNK
    W = GROUP_WIDTH
    N = M2_STATE
    pairs_per_group = M2_HEADS // M2_GROUPS // 2

    @pl.when(pl.program_id(1) == 0)
    def _():
        state[...] = jnp.zeros_like(state)

    dt_all = _softplus(small_ref[...] + dtb_ref[...])
    tri = _tril(L)
    tri16 = tri.astype(BF16)
    neg_a = -jnp.exp(alog_ref[...])
    lo = _lane_lo((1, LANES))
    for ck in range(rows // L):
        rs = slice(ck * L, (ck + 1) * L)
        xbc = xbc_ref[rs, :]
        dt = dt_all[rs, :]
        A_cs = _exact_dot(tri16, dt * neg_a)
        A_rows = _rows_as_lanes(A_cs, 0, M2_HEADS)
        for g in range(M2_GROUPS):
            Bg = xbc[:, W + g * N:W + (g + 1) * N]
            Cg = xbc[:, W + M2_GROUPS * N + g * N:W + M2_GROUPS * N + (g + 1) * N]
            cb = _dot_nt(Cg.astype(BF16), Bg.astype(BF16))
            ys = []
            for pp in range(pairs_per_group):
                p = g * pairs_per_group + pp
                xs = xbc[:, p * LANES:(p + 1) * LANES]
                dtl = jnp.where(lo, dt[:, 2 * p:2 * p + 1], dt[:, 2 * p + 1:2 * p + 2])
                X16 = (xs * dtl).astype(BF16)
                St = state[p]
                rhs = jnp.concatenate([X16, St.astype(BF16)], axis=0)
                y_h, st_h = [], []
                for e in range(2):
                    h = 2 * p + e
                    ac = jnp.broadcast_to(A_cs[:, h:h + 1], (L, LANES))
                    ar = A_rows[h:h + 1, :]
                    decay = jnp.exp(jnp.where(tri, ac - ar, NEG_INF))
                    lhs = jnp.concatenate([(cb * decay).astype(BF16),
                                           (Cg * jnp.exp(ac)).astype(BF16)], axis=1)
                    y_h.append(_dot(lhs, rhs))
                    last = ac[L - 1:L, :]
                    st_h.append(jnp.exp(last) * St
                                + _dot_tn((Bg * jnp.exp(last - ac)).astype(BF16), X16))
                state[p] = jnp.where(lo, st_h[0], st_h[1])
                ys.append(jnp.where(lo, y_h[0], y_h[1]) + dvec_ref[:, p * LANES:(p + 1) * LANES] * xs)
            gs = slice(g * (W // M2_GROUPS), (g + 1) * (W // M2_GROUPS))
            y = jnp.concatenate(ys, axis=1) * z_ref[rs, gs]
            o_ref[rs, gs] = _rms(y, gain_ref[:, gs]).astype(BF16)


def _ssd(proj, z_blk, small_blk, B, S, alog_row, dtb_row, dvec, gain_row, rows=512):
    T = B * S
    W = GROUP_WIDTH
    CW = W + 2 * M2_GROUPS * M2_STATE
    nt = S // rows
    row = lambda b, i: b * nt + i
    return pl.pallas_call(
        functools.partial(_ssd_kernel, rows=rows),
        grid=(B, nt),
        in_specs=[pl.BlockSpec((rows, CW), lambda b, i: (row(b, i), 0)),
                  pl.BlockSpec((rows, W), lambda b, i: (row(b, i), z_blk)),
                  pl.BlockSpec((rows, SMALL_W), lambda b, i: (row(b, i), small_blk)),
                  _resident((1, SMALL_W)),
                  _resident((1, SMALL_W)),
                  _resident((1, W)),
                  _resident((1, W))],
        out_specs=pl.BlockSpec((rows, W), lambda b, i: (row(b, i), 0)),
        out_shape=jax.ShapeDtypeStruct((T, W), BF16),
        scratch_shapes=[pltpu.VMEM((M2_HEADS // 2, M2_STATE, LANES), F32)],
        compiler_params=_params(2),
        name="ssd",
    )(proj, proj, proj, alog_row, dtb_row, dvec, gain_row)


def _pad_lanes(v, first, width=SMALL_W):
    v = v.astype(F32)
    return jnp.pad(v, (first, width - first - v.shape[0])).reshape(1, width)


def kernel(x, norm_gains, w_out, ffn_w_up, ffn_conv_w, ffn_conv_b, ffn_w_down,
           even_w_in, fox_f_bias, gdn_conv_w, gdn_A_log, gdn_dt_bias, gdn_norm_gain,
           odd_w_in, hgrn_lb_logits, hgrn_norm_gain, m2_conv_w, m2_conv_b,
           m2_A_log, m2_dt_bias, m2_D, m2_norm_gain):
    B, S, D = x.shape
    assert D == D_MODEL and S % 512 == 0
    T = B * S
    W = GROUP_WIDTH
    tm = 512
    tm_in = 256
    tm_ffn = 512
    row = lambda v: v.astype(F32).reshape(1, -1)
    x2 = x.reshape(T, D).astype(F32)
    w_out16, w_up16, w_down16 = (w.astype(BF16) for w in (w_out, ffn_w_up, ffn_w_down))
    conv_w32 = ffn_conv_w.astype(F32)
    conv_b32 = ffn_conv_b.astype(F32).reshape(ffn_conv_b.shape[0], 1, -1)

    o_ff = 3 * W
    o_qkv = o_ff + FOX_HEADS
    o_a = o_qkv + 3 * W
    o_b = o_a + GDN_HEADS
    o_gate = o_b + GDN_HEADS
    a_lane, b_lane = FOX_HEADS, FOX_HEADS + GDN_HEADS
    pad = jnp.zeros((D, SMALL_W - FOX_HEADS - 2 * GDN_HEADS), BF16)
    w16 = even_w_in.astype(BF16)
    w_even = jnp.concatenate([w16[:, :o_ff], w16[:, o_qkv:o_a], w16[:, o_gate:],
                              w16[:, o_ff:o_qkv], w16[:, o_a:o_gate], pad], axis=1)
    g = norm_gains[0]
    qkv, gdn_act, gdn_gate, small = _inproj_even(x2, row(g[0]), w_even, gdn_conv_w.astype(F32),
                                                 tm_in, S // tm_in)
    fox_blk = 256
    qt, ka, vt = _foxgate(qkv, small, B, S, 0, _pad_lanes(fox_f_bias, 0), fox_blk)
    o_fox = _fox(qt, ka, vt, B, S, fox_blk)
    o_gdn = _gdn(gdn_act, gdn_gate, small, B, S, _pad_lanes(gdn_A_log, a_lane),
                 _pad_lanes(gdn_dt_bias, a_lane), row(gdn_norm_gain), a_lane, b_lane)
    x2 = _mix_ffn(o_fox, o_gdn, w_out16, x2, B, S, row(g[1]), row(g[2]), w_up16, conv_w32, conv_b32,
                  w_down16, row(g[3]), 0, tm_ffn)

    assert hgrn_lb_logits.shape == (2, W)
    CW = W + 2 * M2_GROUPS * M2_STATE
    pad = jnp.zeros((D, SMALL_W - M2_HEADS), BF16)
    w16 = odd_w_in.astype(BF16)
    w_odd = jnp.concatenate([w16[:, 5 * W:5 * W + CW], w16[:, :5 * W], w16[:, 5 * W + CW:], pad],
                            axis=1)
    g = norm_gains[1]
    proj = _inproj_odd(x2, row(g[0]), w_odd, hgrn_lb_logits.astype(F32), m2_conv_w.astype(F32),
                       row(m2_conv_b), tm, S // tm)
    o_hgrn = _hgrn(proj, CW // W, B, S, row(hgrn_norm_gain))
    o_ssd = _ssd(proj, (CW + 5 * W) // W, (CW + 6 * W) // SMALL_W, B, S, _pad_lanes(m2_A_log, 0),
                 _pad_lanes(m2_dt_bias, 0), row(jnp.repeat(m2_D, M2_HEAD_DIM)), row(m2_norm_gain))
    x2 = _mix_ffn(o_hgrn, o_ssd, w_out16, x2, B, S, row(g[1]), row(g[2]), w_up16, conv_w32, conv_b32,
                  w_down16, row(g[3]), 1, tm_ffn)
    return x2.reshape(B, S, D).astype(x.dtype)
```

```python
import functools

import jax
import jax.numpy as jnp
from jax import lax
from jax.experimental import pallas as pl
from jax.experimental.pallas import tpu as pltpu

F32 = jnp.float32
BF16 = jnp.bfloat16

D_MODEL = 1024
GROUP_WIDTH = D_MODEL // 2
FOX_HEAD_DIM = 64
FOX_HEADS = GROUP_WIDTH // FOX_HEAD_DIM
FOX_AUG = 3
GDN_HEAD_DIM = 128
GDN_HEADS = GROUP_WIDTH // GDN_HEAD_DIM
GDN_CHUNK = 64
SHORT_CONV = 4
HGRN_HEAD_DIM = 128
HGRN_HEADS = GROUP_WIDTH // HGRN_HEAD_DIM
HGRN_CHUNK = 64
M2_HEAD_DIM = 64
M2_HEADS = GROUP_WIDTH // M2_HEAD_DIM
M2_GROUPS = 2
M2_STATE = 128
M2_CHUNK = 128
D_FF = 2816
FFN_CONV = 3
NORM_EPS = 1e-6

LANES = 128
SUBLANES = 8
SMALL_W = LANES
VMEM_LIMIT = 56 * 1024 * 1024

NEG_INF = float("-inf")


def _dot(a, b):
    return jnp.dot(a, b, preferred_element_type=F32)


def _dot_nt(a, b):
    return lax.dot_general(a, b, (((1,), (1,)), ((), ())), preferred_element_type=F32)


def _dot_tn(a, b):
    return lax.dot_general(a, b, (((0,), (0,)), ((), ())), preferred_element_type=F32)


def _bf16_pieces(x):
    hi = x.astype(BF16)
    rest = x - hi.astype(F32)
    mid = rest.astype(BF16)
    return hi, mid, (rest - mid.astype(F32)).astype(BF16)


def _exact_dot(sel16, x):
    hi, mid, lo = _bf16_pieces(x)
    return _dot(sel16, hi) + _dot(sel16, mid) + _dot(sel16, lo)


def _rms(x, gain):
    return x * lax.rsqrt(jnp.mean(x * x, axis=-1, keepdims=True) + NORM_EPS) * gain


def _sigmoid(x):
    return 1.0 / (1.0 + jnp.exp(-x))


def _silu(x):
    h = 0.5 * x
    return h * jnp.tanh(h) + h


def _softplus(x):
    return jnp.maximum(x, 0.0) + jnp.log1p(jnp.exp(-jnp.abs(x)))


def _log_sigmoid(x):
    return jnp.minimum(x, 0.0) - jnp.log1p(jnp.exp(-jnp.abs(x)))


def _tril(n, strict=False):
    r = lax.broadcasted_iota(jnp.int32, (n, n), 0)
    c = lax.broadcasted_iota(jnp.int32, (n, n), 1)
    return (r > c) if strict else (r >= c)


def _lane_lo(shape):
    return lax.broadcasted_iota(jnp.int32, shape, len(shape) - 1) < (LANES // 2)


def _rows_as_lanes(cols, first, n):
    sel = (lax.broadcasted_iota(jnp.int32, (n, LANES), 1)
           == lax.broadcasted_iota(jnp.int32, (n, LANES), 0) + first).astype(BF16)
    hi, mid, lo = _bf16_pieces(cols)
    return _dot_nt(sel, hi) + _dot_nt(sel, mid) + _dot_nt(sel, lo)


def _params(n_grid):
    return pltpu.CompilerParams(dimension_semantics=("arbitrary",) * n_grid,
                                vmem_limit_bytes=VMEM_LIMIT)


def _resident(shape):
    nd = len(shape)
    return pl.BlockSpec(shape, lambda *_: (0,) * nd, pipeline_mode=pl.Buffered(1))


def _conv_tail_reset(tail, tiles_per_seq):
    @pl.when(pl.program_id(0) % tiles_per_seq == 0)
    def _():
        tail[...] = jnp.zeros_like(tail)


def _conv_chunk(p, tail, w_ref, cols, width):
    rows = p.shape[0]
    ext = jnp.concatenate([tail[:, cols], p], axis=0)
    tail[:, cols] = p[rows - SUBLANES:rows, :]
    y = None
    for k in range(width):
        start = SUBLANES - (width - 1) + k
        term = ext[start:start + rows, :] * w_ref[k:k + 1, cols]
        y = term if y is None else y + term
    return y


def _staggered(stages, lag=2):
    pending = []
    for produce, consume in stages:
        pending.append((produce(), consume))
        if len(pending) > lag:
            value, done = pending.pop(0)
            done(value)
    for value, done in pending:
        done(value)


def _inproj_even_kernel(x_ref, g_ref, w_ref, cw_ref, qkv_ref, act_ref, gate_ref, small_ref, tail,
                        *, tm, tiles_per_seq):
    W = GROUP_WIDTH
    D = GDN_HEAD_DIM
    _conv_tail_reset(tail, tiles_per_seq)
    hn = _rms(x_ref[...], g_ref[...]).astype(BF16)
    proj = lambda c0, n: (lambda: _dot(hn, w_ref[:, c0:c0 + n]))

    PW = 2 * D

    def fox_out(c0):
        def consume(p):
            qkv_ref[:, c0:c0 + PW] = p.astype(BF16)
        return consume

    def gdn_out(c0):
        def consume(p):
            a = _silu(_conv_chunk(p, tail, cw_ref, slice(c0, c0 + PW), SHORT_CONV))
            for h in range(PW // D):
                ah = a[:, h * D:(h + 1) * D]
                if c0 < 2 * W:
                    ah = ah * lax.rsqrt(jnp.sum(ah * ah, axis=-1, keepdims=True) + NORM_EPS)
                if c0 < W:
                    ah = ah * D ** -0.5
                act_ref[:, c0 + h * D:c0 + (h + 1) * D] = ah
        return consume

    def gate_out(c0):
        def consume(p):
            gate_ref[:, c0:c0 + PW] = _silu(p)
        return consume

    def small_out(p):
        small_ref[...] = p

    stages = []
    for c0 in range(0, 3 * W, PW):
        stages += [(proj(3 * W + c0, PW), gdn_out(c0)), (proj(c0, PW), fox_out(c0))]
    stages += [(proj(6 * W + c0, PW), gate_out(c0)) for c0 in range(0, W, PW)]
    stages += [(proj(7 * W, SMALL_W), small_out)]
    _staggered(stages)


def _inproj_even(x2, gain, w, conv_w, tm, tiles_per_seq):
    T = x2.shape[0]
    W = GROUP_WIDTH
    widths = (3 * W, 3 * W, W, SMALL_W)
    dtypes = (BF16, F32, F32, F32)
    return pl.pallas_call(
        functools.partial(_inproj_even_kernel, tm=tm, tiles_per_seq=tiles_per_seq),
        grid=(T // tm,),
        in_specs=[pl.BlockSpec((tm, D_MODEL), lambda i: (i, 0)),
                  _resident((1, D_MODEL)),
                  _resident(w.shape),
                  _resident(conv_w.shape)],
        out_specs=[pl.BlockSpec((tm, n), lambda i: (i, 0)) for n in widths],
        out_shape=[jax.ShapeDtypeStruct((T, n), dt) for n, dt in zip(widths, dtypes)],
        scratch_shapes=[pltpu.VMEM((SUBLANES, 3 * W), F32)],
        compiler_params=_params(1),
        name="inproj_even",
    )(x2, gain, w, conv_w)


def _inproj_odd_kernel(x_ref, g_ref, w_ref, lbl_ref, cw_ref, cb_ref, o_ref, tail,
                       *, tm, tiles_per_seq):
    W = GROUP_WIDTH
    CW = W + 2 * M2_GROUPS * M2_STATE
    _conv_tail_reset(tail, tiles_per_seq)
    hn = _rms(x_ref[...], g_ref[...]).astype(BF16)
    proj = lambda c0, n: (lambda: _dot(hn, w_ref[:, c0:c0 + n]))

    logits = lbl_ref[...]
    e = jnp.exp(logits - jnp.max(logits, axis=0, keepdims=True))
    prob = e / jnp.sum(e, axis=0, keepdims=True)
    lb = (prob[0:1, :] + prob[1:2, :]) - prob[0:1, :]

    def conv_out(c):
        def consume(p):
            cols = slice(c * W, (c + 1) * W)
            o_ref[:, cols] = _silu(_conv_chunk(p, tail, cw_ref, cols, SHORT_CONV) + cb_ref[:, cols])
        return consume

    def mapped_out(col, fn):
        def consume(p):
            o_ref[:, col:col + p.shape[1]] = fn(p)
        return consume

    def forget_out(col):
        def consume(p):
            gate = _sigmoid(p)
            o_ref[:, col:col + W] = jnp.log(lb + (1.0 - lb) * gate)
            o_ref[:, col + W:col + 2 * W] = (1.0 - lb) * (1.0 - gate)
        return consume

    keep = lambda p: p
    stages = [(proj(0, W), conv_out(0)),
              (proj(CW + 2 * W, W), mapped_out(CW + 3 * W, keep)),
              (proj(W, W), conv_out(1)),
              (proj(CW, W), mapped_out(CW, _silu)),
              (proj(CW + W, W), forget_out(CW + W)),
              (proj(CW + 3 * W, W), mapped_out(CW + 4 * W, _silu)),
              (proj(CW + 4 * W, W), mapped_out(CW + 5 * W, _silu)),
              (proj(CW + 5 * W, SMALL_W), mapped_out(CW + 6 * W, keep))]
    _staggered(stages)


def _inproj_odd(x2, gain, w, lb_logits, conv_w, conv_b, tm, tiles_per_seq):
    T = x2.shape[0]
    W = GROUP_WIDTH
    CW = W + 2 * M2_GROUPS * M2_STATE
    n_out = CW + 6 * W + SMALL_W
    return pl.pallas_call(
        functools.partial(_inproj_odd_kernel, tm=tm, tiles_per_seq=tiles_per_seq),
        grid=(T // tm,),
        in_specs=[pl.BlockSpec((tm, D_MODEL), lambda i: (i, 0)),
                  _resident((1, D_MODEL)),
                  _resident(w.shape),
                  _resident(lb_logits.shape),
                  _resident(conv_w.shape),
                  _resident(conv_b.shape)],
        out_specs=pl.BlockSpec((tm, n_out), lambda i: (i, 0)),
        out_shape=jax.ShapeDtypeStruct((T, n_out), F32),
        scratch_shapes=[pltpu.VMEM((SUBLANES, CW), F32)],
        compiler_params=_params(1),
        name="inproj_odd",
    )(x2, gain, w, lb_logits, conv_w, conv_b)


def _mix_ffn_kernel(ma_ref, mb_ref, wo_ref, x_ref, gmix_ref, gpre_ref, wup_ref, cw_ref, cb_ref,
                    wdn_ref, gpost_ref, o_ref, carry, act, *, tm, fc, group):
    W = GROUP_WIDTH
    n_chunks = D_FF // fc

    @pl.when(pl.program_id(1) == 0)
    def _():
        carry[...] = jnp.zeros_like(carry)

    mix = _dot(ma_ref[...], wo_ref[0:W, :]) + _dot(mb_ref[...], wo_ref[W:2 * W, :])
    x = x_ref[...] + _rms(mix, gmix_ref[...])
    hn = _rms(x, gpre_ref[...]).astype(BF16)
    y = None
    for c in range(n_chunks):
        halves = []
        for half in range(2):
            cols = slice(half * D_FF + c * fc, half * D_FF + (c + 1) * fc)
            u = _dot(hn, wup_ref[:, cols])
            halves.append(_conv_chunk(u, carry, cw_ref, cols, FFN_CONV) + cb_ref[:, cols])
        act[:, c * fc:(c + 1) * fc] = (_silu(halves[0]) * halves[1]).astype(BF16)
        if (c + 1) % group == 0 or c + 1 == n_chunks:
            rows = slice((c // group) * group * fc, (c + 1) * fc)
            part = _dot(act[:, rows], wdn_ref[rows, :])
            y = part if y is None else y + part
    o_ref[...] = x + _rms(y, gpost_ref[...])


def _mix_ffn(mix_a, mix_b, w_out, x2, B, S, gmix, gpre, w_up, conv_w, conv_b, w_down, gpost,
             layer, tm, fc=256):
    W = GROUP_WIDTH
    nt = S // tm
    row = lambda b, i: (b * nt + i, 0)
    slab = lambda *dims: pl.BlockSpec((None,) + dims, lambda *_: (layer,) + (0,) * len(dims),
                                      pipeline_mode=pl.Buffered(1))
    streamed = lambda n: pl.BlockSpec((tm, n), row)
    return pl.pallas_call(
        functools.partial(_mix_ffn_kernel, tm=tm, fc=fc, group=6),
        grid=(B, nt),
        in_specs=[streamed(W),
                  streamed(W),
                  slab(D_MODEL, D_MODEL),
                  streamed(D_MODEL),
                  _resident((1, D_MODEL)),
                  _resident((1, D_MODEL)),
                  slab(D_MODEL, 2 * D_FF),
                  slab(FFN_CONV, 2 * D_FF),
                  slab(1, 2 * D_FF),
                  slab(D_FF, D_MODEL),
                  _resident((1, D_MODEL))],
        out_specs=pl.BlockSpec((tm, D_MODEL), row),
        out_shape=jax.ShapeDtypeStruct(x2.shape, F32),
        scratch_shapes=[pltpu.VMEM((SUBLANES, 2 * D_FF), F32),
                        pltpu.VMEM((tm, D_FF), BF16)],
        compiler_params=_params(2),
        name="mixffn",
    )(mix_a, mix_b, w_out, x2, gmix, gpre, w_up, conv_w, conv_b, w_down, gpost)


def _foxgate_kernel(s_ref, b_ref, qk_ref, qt_ref, ka_ref, vt_ref, carry, *, blk, nsub):
    W = GROUP_WIDTH

    @pl.when(pl.program_id(1) == 0)
    def _():
        carry[...] = jnp.zeros_like(carry)

    lane = lax.broadcasted_iota(jnp.int32, (1, LANES), 1)
    lo = lane < FOX_HEAD_DIM
    ones = jnp.where(lane < FOX_HEAD_DIM + FOX_AUG, 1.0, 0.0)
    scale = FOX_HEAD_DIM ** -0.5
    tri16 = _tril(blk).astype(BF16)
    total = carry[...]
    for sb in range(nsub):
        rs = slice(sb * blk, (sb + 1) * blk)
        z = s_ref[rs, :] + b_ref[...]
        cs = _exact_dot(tri16, _log_sigmoid(z)) + total
        total = cs[blk - 1:blk, :]
        for p in range(FOX_HEADS // 2):
            q = qk_ref[rs, p * LANES:(p + 1) * LANES].astype(F32) * scale
            k = qk_ref[rs, W + p * LANES:W + (p + 1) * LANES].astype(F32)
            for e in range(2):
                h = 2 * p + e
                qe = pltpu.roll(q, FOX_HEAD_DIM, 1) if e else q
                ke = pltpu.roll(k, FOX_HEAD_DIM, 1) if e else k
                rem = -cs[:, h:h + 1]
                aug = jnp.zeros((blk, LANES), F32)
                for piece in range(FOX_AUG):
                    part = rem.astype(BF16).astype(F32)
                    aug = jnp.where(lane == FOX_HEAD_DIM + piece, part, aug)
                    rem = rem - part
                qt_ref[0, sb, h * LANES:(h + 1) * LANES, :] = jnp.where(lo, qe, ones).T.astype(BF16)
                ka_ref[rs, h * LANES:(h + 1) * LANES] = jnp.where(lo, ke, aug).astype(BF16)
            v = qk_ref[rs, 2 * W + p * LANES:2 * W + (p + 1) * LANES].astype(F32)
            vt_ref[0, sb, p * LANES:(p + 1) * LANES, :] = v.T.astype(BF16)
    carry[...] = total


def _foxgate(qkv, small, B, S, small_blk, bias_row, blk, nsub=2):
    T = B * S
    W = GROUP_WIDTH
    nb = S // blk
    ns = nb // nsub
    rows = nsub * blk
    wide = FOX_HEADS * LANES
    row = lambda b, j: b * ns + j
    return pl.pallas_call(
        functools.partial(_foxgate_kernel, blk=blk, nsub=nsub),
        grid=(B, ns),
        in_specs=[pl.BlockSpec((rows, SMALL_W), lambda b, j: (row(b, j), small_blk)),
                  _resident((1, SMALL_W)),
                  pl.BlockSpec((rows, 3 * W), lambda b, j: (row(b, j), 0))],
        out_specs=[pl.BlockSpec((1, nsub, wide, blk), lambda b, j: (b, j, 0, 0)),
                   pl.BlockSpec((rows, wide), lambda b, j: (row(b, j), 0)),
                   pl.BlockSpec((1, nsub, W, blk), lambda b, j: (b, j, 0, 0))],
        out_shape=[jax.ShapeDtypeStruct((B, nb, wide, blk), BF16),
                   jax.ShapeDtypeStruct((T, wide), BF16),
                   jax.ShapeDtypeStruct((B, nb, W, blk), BF16)],
        scratch_shapes=[pltpu.VMEM((1, SMALL_W), F32)],
        compiler_params=_params(2),
        name="foxgate",
    )(small, bias_row, qkv)


def _fox_kernel(qt_ref, k_ref, vt_ref, o_ref, m_ref, l_ref, acc_ref, *, blk, kvb):
    i = pl.program_id(1)
    kv_id = lax.broadcasted_iota(jnp.int32, (kvb, blk), 0)
    q_id = lax.broadcasted_iota(jnp.int32, (kvb, blk), 1)
    first_head = lax.broadcasted_iota(jnp.int32, (LANES, 1), 0) < FOX_HEAD_DIM
    m_ref[...] = jnp.full(m_ref.shape, NEG_INF, F32)
    l_ref[...] = jnp.zeros_like(l_ref)
    acc_ref[...] = jnp.zeros_like(acc_ref)

    def sub_block(j, sub, masked):
        off = pl.multiple_of(j * blk + sub * kvb, kvb)
        for p in range(FOX_HEADS // 2):
            vt = vt_ref[0, j, p * LANES:(p + 1) * LANES, sub * kvb:(sub + 1) * kvb]
            alpha, pv = [], []
            for e in range(2):
                h = 2 * p + e
                hs = slice(h * LANES, (h + 1) * LANES)
                s = _dot(k_ref[pl.ds(off, kvb), hs], qt_ref[0, 0, hs, :])
                if masked:
                    s = jnp.where(kv_id + sub * kvb <= q_id, s, NEG_INF)
                m_old = m_ref[h:h + 1, :]
                m_new = jnp.maximum(m_old, jnp.max(s, axis=0, keepdims=True))
                pe = jnp.exp(s - m_new)
                a = jnp.exp(m_old - m_new)
                m_ref[h:h + 1, :] = m_new
                l_ref[h:h + 1, :] = a * l_ref[h:h + 1, :] + jnp.sum(pe, axis=0, keepdims=True)
                alpha.append(a)
                pv.append(_dot(vt, pe.astype(BF16)))
            acc_ref[p] = (jnp.where(first_head, alpha[0], alpha[1]) * acc_ref[p]
                          + jnp.where(first_head, pv[0], pv[1]))

    def block(j, masked):
        for sub in range(blk // kvb):
            sub_block(j, sub, masked)

    def body(j, carry):
        block(j, False)
        return carry

    lax.fori_loop(0, i, body, 0)
    block(i, True)
    for p in range(FOX_HEADS // 2):
        l = jnp.where(first_head, l_ref[2 * p:2 * p + 1, :], l_ref[2 * p + 1:2 * p + 2, :])
        o_ref[:, p * LANES:(p + 1) * LANES] = (acc_ref[p] / l).T.astype(BF16)


def _fox(qt, ka, vt, B, S, blk):
    T = B * S
    nq = S // blk
    W = GROUP_WIDTH
    wide = FOX_HEADS * LANES
    return pl.pallas_call(
        functools.partial(_fox_kernel, blk=blk, kvb=128),
        grid=(B, nq),
        in_specs=[pl.BlockSpec((1, 1, wide, blk), lambda b, i: (b, i, 0, 0)),
                  pl.BlockSpec((S, wide), lambda b, i: (b, 0)),
                  pl.BlockSpec((1, nq, W, blk), lambda b, i: (b, 0, 0, 0))],
        out_specs=pl.BlockSpec((blk, W), lambda b, i: (b * nq + i, 0)),
        out_shape=jax.ShapeDtypeStruct((T, W), BF16),
        scratch_shapes=[pltpu.VMEM((FOX_HEADS, blk), F32),
                        pltpu.VMEM((FOX_HEADS, blk), F32),
                        pltpu.VMEM((FOX_HEADS // 2, LANES, blk), F32)],
        compiler_params=_params(2),
        name="fox",
    )(qt, ka, vt)


def _unit_lower_solves(ms, rhss, n, nilpotent):
    eye = (lax.broadcasted_iota(jnp.int32, (n, n), 0)
           == lax.broadcasted_iota(jnp.int32, (n, n), 1)).astype(F32)
    xs = [-m for m in ms]
    sols = rhss
    power = 1
    while power < nilpotent:
        x16 = [x.astype(BF16) for x in xs]
        x2 = [_dot(x, x) for x in x16]
        x2_16 = [x.astype(BF16) for x in x2]
        factors = [eye + x + y + _dot(xb, yb) for x, y, xb, yb in zip(xs, x2, x16, x2_16)]
        sols = [_dot(f.astype(BF16), r.astype(BF16)) for f, r in zip(factors, sols)]
        power *= 4
        if power < nilpotent:
            xs = [_dot(y, y) for y in x2_16]
    assert power == nilpotent
    return sols


def _gdn_kernel(act_ref, gate_ref, small_ref, alog_ref, dtb_ref, gain_ref, o_ref,
                state, sol_ref, qd_ref, kd_ref, aqk_ref, ks_ref, ubuf, *, a_lane, b_lane, rows):
    C = GDN_CHUNK
    P = 2 * C
    W = GROUP_WIDTH
    D = GDN_HEAD_DIM
    n_chunks = rows // C

    @pl.when(pl.program_id(1) == 0)
    def _():
        state[...] = jnp.zeros_like(state)

    small = small_ref[...]
    g_all = -jnp.exp(alog_ref[...]) * _softplus(small + dtb_ref[...])
    beta_all = _sigmoid(small)
    r_id = lax.broadcasted_iota(jnp.int32, (P, P), 0)
    c_id = lax.broadcasted_iota(jnp.int32, (P, P), 1)
    tri_bd = (c_id >= r_id - (r_id & (C - 1))) & (c_id <= r_id)
    diag = r_id == c_id
    tri16 = tri_bd.astype(BF16)
    g_tot, ms, rhss, where = [], [], [], []
    for pr in range(rows // P):
        ps = slice(pr * P, (pr + 1) * P)
        G_all = _exact_dot(tri16, g_all[ps, :])
        G_rows = _rows_as_lanes(G_all, a_lane, SUBLANES)
        G_tot = jnp.concatenate(
            [jnp.broadcast_to(G_all[(c + 1) * C - 1:(c + 1) * C, :], (C, SMALL_W)) for c in range(2)],
            axis=0)
        g_tot.append(G_tot)
        for h in range(GDN_HEADS):
            q = act_ref[ps, h * D:(h + 1) * D]
            k = act_ref[ps, W + h * D:W + (h + 1) * D]
            v = act_ref[ps, 2 * W + h * D:2 * W + (h + 1) * D]
            Gc = G_all[:, a_lane + h:a_lane + h + 1]
            Gr = G_rows[h:h + 1, :]
            Gt = G_tot[:, a_lane + h:a_lane + h + 1]
            beta = beta_all[ps, b_lane + h:b_lane + h + 1]
            gamma = jnp.exp(jnp.where(tri_bd, Gc - Gr, NEG_INF))
            kb = k * beta
            kb16, k16 = kb.astype(BF16), k.astype(BF16)
            kq = _dot_nt(jnp.concatenate([kb16, q.astype(BF16)], axis=0), k16)
            ms.append(jnp.where(diag, 0.0, kq[0:P, :] * gamma))
            eG = jnp.exp(Gc)
            rhss.append(jnp.concatenate([v * beta, kb * eG], axis=1))
            where.append((h, ps))
            aqk_ref[h, ps, :] = (kq[P:2 * P, :] * gamma).astype(BF16)
            qd_ref[h, ps, :] = (q * eG).astype(BF16)
            kd_ref[h, ps, :] = (k * jnp.exp(Gt - Gc)).astype(BF16)
    for (h, ps), sol in zip(where, _unit_lower_solves(ms, rhss, P, C)):
        sol_ref[h, ps, :] = sol
    for c in range(n_chunks):
        rs = slice(c * C, (c + 1) * C)
        for h in range(GDN_HEADS):
            ks_ref[h, c] = _dot_tn(kd_ref[h, rs, :], sol_ref[h, rs, :].astype(BF16))

    states = [state[h] for h in range(GDN_HEADS)]
    for c in range(n_chunks):
        rs = slice(c * C, (c + 1) * C)
        first = c % 2 == 0
        for h in range(GDN_HEADS):
            sl = slice(h * D, (h + 1) * D)
            St = states[h]
            St16 = St.astype(BF16)
            decay = jnp.exp(g_tot[c // 2][(c % 2) * C:(c % 2) * C + 1, a_lane + h:a_lane + h + 1])
            states[h] = (decay * St + ks_ref[h, c, :, 0:D]
                         - _dot(ks_ref[h, c, :, D:2 * D].astype(BF16), St16))
            u = sol_ref[h, rs, 0:D] - _dot(sol_ref[h, rs, D:2 * D].astype(BF16), St16)
            u16 = u.astype(BF16)
            o = _dot(qd_ref[h, rs, :], St16)
            if first:
                ubuf[h] = jnp.concatenate([u16, jnp.zeros_like(u16)], axis=0)
                o = o + _dot(aqk_ref[h, rs, 0:C], u16)
            else:
                ubuf[h, C:P, :] = u16
                o = o + _dot(aqk_ref[h, rs, :], ubuf[h])
            o = _rms(o, gain_ref[...]) * gate_ref[rs, sl]
            o_ref[rs, sl] = o.astype(BF16)
    for h in range(GDN_HEADS):
        state[h] = states[h]


def _gdn(act, gate, small, B, S, alog_row, dtb_row, gain_row, a_lane, b_lane, rows=512):
    T = B * S
    W = GROUP_WIDTH
    H, D = GDN_HEADS, GDN_HEAD_DIM
    P = 2 * GDN_CHUNK
    nt = S // rows
    row = lambda b, i: (b * nt + i, 0)
    return pl.pallas_call(
        functools.partial(_gdn_kernel, a_lane=a_lane, b_lane=b_lane, rows=rows),
        grid=(B, nt),
        in_specs=[pl.BlockSpec((rows, 3 * W), row),
                  pl.BlockSpec((rows, W), row),
                  pl.BlockSpec((rows, SMALL_W), row),
                  _resident((1, SMALL_W)),
                  _resident((1, SMALL_W)),
                  _resident((1, D))],
        out_specs=pl.BlockSpec((rows, W), row),
        out_shape=jax.ShapeDtypeStruct((T, W), BF16),
        scratch_shapes=[pltpu.VMEM((H, D, D), F32),
                        pltpu.VMEM((H, rows, 2 * D), F32),
                        pltpu.VMEM((H, rows, D), BF16),
                        pltpu.VMEM((H, rows, D), BF16),
                        pltpu.VMEM((H, rows, P), BF16),
                        pltpu.VMEM((H, rows // GDN_CHUNK, D, 2 * D), F32),
                        pltpu.VMEM((H, P, D), BF16)],
        compiler_params=_params(2),
        name="gdn",
    )(act, gate, small, alog_row, dtb_row, gain_row)


def _block_ref_rows(x, half):
    R, L = x.shape
    if half >= SUBLANES:
        xb = x.reshape(R // (2 * half), 2 * half, L)
        return jnp.broadcast_to(xb[:, half:half + 1, :], xb.shape).reshape(R, L)
    xb = x.reshape(R // SUBLANES, SUBLANES, L)
    sub = lax.broadcasted_iota(jnp.int32, (1, SUBLANES, 1), 1)
    out = None
    for start in range(SUBLANES - 2 * half, -1, -2 * half):
        row = jnp.broadcast_to(xb[:, start + half:start + half + 1, :], xb.shape)
        out = row if out is None else jnp.where(sub < start + 2 * half, row, out)
    return out.reshape(R, L)


def _hgrn_kernel(q_ref, f_ref, k_ref, i_ref, g_ref, gain_ref, o_ref, state, *, rows):
    C = HGRN_CHUNK
    P = 2 * C
    D = HGRN_HEAD_DIM
    halves = [1 << b for b in range(C.bit_length() - 1)]

    @pl.when(pl.program_id(1) == 0)
    def _():
        state[...] = jnp.zeros_like(state)

    r_id = lax.broadcasted_iota(jnp.int32, (P, P), 0)
    c_id = lax.broadcasted_iota(jnp.int32, (P, P), 1)
    tri16 = ((c_id >= r_id - (r_id & (C - 1))) & (c_id <= r_id)).astype(BF16)
    differ = r_id ^ c_id
    row_id = lax.broadcasted_iota(jnp.int32, (P, 1), 0)
    gain = gain_ref[...]
    for pr in range(rows // P):
        ps = slice(pr * P, (pr + 1) * P)
        G_all = _exact_dot(tri16, f_ref[ps, :])
        decays = [jnp.exp(-jnp.abs(G_all - _block_ref_rows(G_all, half))) for half in halves]
        heads = [slice(h * D, (h + 1) * D) for h in range(HGRN_HEADS)]
        scores = [jnp.zeros((P, P), F32) for _ in heads]
        for half, dec in reversed(list(zip(halves, decays))):
            upper = (row_id & half) != 0
            for h, sl in enumerate(heads):
                z16 = (jnp.where(upper, q_ref[ps, sl], k_ref[ps, sl]) * dec[:, sl]).astype(BF16)
                scores[h] = jnp.where(differ < 2 * half, _dot_nt(z16, z16), scores[h])
        for h, sl in enumerate(heads):
            q, k, G = q_ref[ps, sl], k_ref[ps, sl], G_all[:, sl]
            v16 = i_ref[ps, sl].astype(BF16)
            a = jnp.where(differ == 0, _dot_nt(q.astype(BF16), k.astype(BF16)), scores[h])
            a = jnp.where(r_id >= c_id, a, 0.0)
            o_intra = _dot(a.astype(BF16), v16)
            q_in = (q * jnp.exp(G)).astype(BF16)
            for c in range(2):
                cs = slice(c * C, (c + 1) * C)
                G_last = G[(c + 1) * C - 1:(c + 1) * C, :]
                St = state[h]
                o = _dot_nt(q_in[cs, :], St.astype(BF16)) + o_intra[cs, :]
                k_out = (k[cs, :] * jnp.exp(G_last - G[cs, :])).astype(BF16)
                state[h] = jnp.exp(G_last) * St + _dot_tn(v16[cs, :], k_out)
                rs = slice(pr * P + c * C, pr * P + (c + 1) * C)
                o_ref[rs, sl] = (_rms(o, gain) * g_ref[rs, sl]).astype(BF16)


def _hgrn(proj, first_blk, B, S, gain_row, rows=512):
    T = B * S
    W = GROUP_WIDTH
    nt = S // rows
    spec = lambda j: pl.BlockSpec((rows, W), lambda b, i: (b * nt + i, j))
    return pl.pallas_call(
        functools.partial(_hgrn_kernel, rows=rows),
        grid=(B, nt),
        in_specs=[spec(first_blk + j) for j in range(5)] + [_resident((1, HGRN_HEAD_DIM))],
        out_specs=spec(0),
        out_shape=jax.ShapeDtypeStruct((T, W), BF16),
        scratch_shapes=[pltpu.VMEM((HGRN_HEADS, HGRN_HEAD_DIM, HGRN_HEAD_DIM), F32)],
        compiler_params=_params(2),
        name="hgrn2",
    )(proj, proj, proj, proj, proj, gain_row)


def _ssd_kernel(xbc_ref, z_ref, small_ref, alog_ref, dtb_ref, dvec_ref, gain_ref,
                o_ref, state, *, rows):
    L = M2_CHUNK
    W = GROUP_WIDTH
    N = M2_STATE
    pairs_per_group = M2_HEADS // M2_GROUPS // 2

    @pl.when(pl.program_id(1) == 0)
    def _():
        state[...] = jnp.zeros_like(state)

    dt_all = _softplus(small_ref[...] + dtb_ref[...])
    tri = _tril(L)
    tri16 = tri.astype(BF16)
    neg_a = -jnp.exp(alog_ref[...])
    lo = _lane_lo((1, LANES))
    for ck in range(rows // L):
        rs = slice(ck * L, (ck + 1) * L)
        xbc = xbc_ref[rs, :]
        dt = dt_all[rs, :]
        A_cs = _exact_dot(tri16, dt * neg_a)
        A_rows = _rows_as_lanes(A_cs, 0, M2_HEADS)
        for g in range(M2_GROUPS):
            Bg = xbc[:, W + g * N:W + (g + 1) * N]
            Cg = xbc[:, W + M2_GROUPS * N + g * N:W + M2_GROUPS * N + (g + 1) * N]
            cb = _dot_nt(Cg.astype(BF16), Bg.astype(BF16))
            ys = []
            for pp in range(pairs_per_group):
                p = g * pairs_per_group + pp
                xs = xbc[:, p * LANES:(p + 1) * LANES]
                dtl = jnp.where(lo, dt[:, 2 * p:2 * p + 1], dt[:, 2 * p + 1:2 * p + 2])
                X16 = (xs * dtl).astype(BF16)
                St = state[p]
                rhs = jnp.concatenate([X16, St.astype(BF16)], axis=0)
                y_h, st_h = [], []
                for e in range(2):
                    h = 2 * p + e
                    ac = jnp.broadcast_to(A_cs[:, h:h + 1], (L, LANES))
                    ar = A_rows[h:h + 1, :]
                    decay = jnp.exp(jnp.where(tri, ac - ar, NEG_INF))
                    lhs = jnp.concatenate([(cb * decay).astype(BF16),
                                           (Cg * jnp.exp(ac)).astype(BF16)], axis=1)
                    y_h.append(_dot(lhs, rhs))
                    last = ac[L - 1:L, :]
                    st_h.append(jnp.exp(last) * St
                                + _dot_tn((Bg * jnp.exp(last - ac)).astype(BF16), X16))
                state[p] = jnp.where(lo, st_h[0], st_h[1])
                ys.append(jnp.where(lo, y_h[0], y_h[1]) + dvec_ref[:, p * LANES:(p + 1) * LANES] * xs)
            gs = slice(g * (W // M2_GROUPS), (g + 1) * (W // M2_GROUPS))
            y = jnp.concatenate(ys, axis=1) * z_ref[rs, gs]
            o_ref[rs, gs] = _rms(y, gain_ref[:, gs]).astype(BF16)


def _ssd(proj, z_blk, small_blk, B, S, alog_row, dtb_row, dvec, gain_row, rows=512):
    T = B * S
    W = GROUP_WIDTH
    CW = W + 2 * M2_GROUPS * M2_STATE
    nt = S // rows
    row = lambda b, i: b * nt + i
    return pl.pallas_call(
        functools.partial(_ssd_kernel, rows=rows),
        grid=(B, nt),
        in_specs=[pl.BlockSpec((rows, CW), lambda b, i: (row(b, i), 0)),
                  pl.BlockSpec((rows, W), lambda b, i: (row(b, i), z_blk)),
                  pl.BlockSpec((rows, SMALL_W), lambda b, i: (row(b, i), small_blk)),
                  _resident((1, SMALL_W)),
                  _resident((1, SMALL_W)),
                  _resident((1, W)),
                  _resident((1, W))],
        out_specs=pl.BlockSpec((rows, W), lambda b, i: (row(b, i), 0)),
        out_shape=jax.ShapeDtypeStruct((T, W), BF16),
        scratch_shapes=[pltpu.VMEM((M2_HEADS // 2, M2_STATE, LANES), F32)],
        compiler_params=_params(2),
        name="ssd",
    )(proj, proj, proj, alog_row, dtb_row, dvec, gain_row)


def _pad_lanes(v, first, width=SMALL_W):
    v = v.astype(F32)
    return jnp.pad(v, (first, width - first - v.shape[0])).reshape(1, width)


def kernel(x, norm_gains, w_out, ffn_w_up, ffn_conv_w, ffn_conv_b, ffn_w_down,
           even_w_in, fox_f_bias, gdn_conv_w, gdn_A_log, gdn_dt_bias, gdn_norm_gain,
           odd_w_in, hgrn_lb_logits, hgrn_norm_gain, m2_conv_w, m2_conv_b,
           m2_A_log, m2_dt_bias, m2_D, m2_norm_gain):
    B, S, D = x.shape
    assert D == D_MODEL and S % 512 == 0
    T = B * S
    W = GROUP_WIDTH
    tm = 512
    tm_in = 256
    tm_ffn = 512
    row = lambda v: v.astype(F32).reshape(1, -1)
    x2 = x.reshape(T, D).astype(F32)
    w_out16, w_up16, w_down16 = (w.astype(BF16) for w in (w_out, ffn_w_up, ffn_w_down))
    conv_w32 = ffn_conv_w.astype(F32)
    conv_b32 = ffn_conv_b.astype(F32).reshape(ffn_conv_b.shape[0], 1, -1)

    o_ff = 3 * W
    o_qkv = o_ff + FOX_HEADS
    o_a = o_qkv + 3 * W
    o_b = o_a + GDN_HEADS
    o_gate = o_b + GDN_HEADS
    a_lane, b_lane = FOX_HEADS, FOX_HEADS + GDN_HEADS
    pad = jnp.zeros((D, SMALL_W - FOX_HEADS - 2 * GDN_HEADS), BF16)
    w16 = even_w_in.astype(BF16)
    w_even = jnp.concatenate([w16[:, :o_ff], w16[:, o_qkv:o_a], w16[:, o_gate:],
                              w16[:, o_ff:o_qkv], w16[:, o_a:o_gate], pad], axis=1)
    g = norm_gains[0]
    qkv, gdn_act, gdn_gate, small = _inproj_even(x2, row(g[0]), w_even, gdn_conv_w.astype(F32),
                                                 tm_in, S // tm_in)
    fox_blk = 256
    qt, ka, vt = _foxgate(qkv, small, B, S, 0, _pad_lanes(fox_f_bias, 0), fox_blk)
    o_fox = _fox(qt, ka, vt, B, S, fox_blk)
    o_gdn = _gdn(gdn_act, gdn_gate, small, B, S, _pad_lanes(gdn_A_log, a_lane),
                 _pad_lanes(gdn_dt_bias, a_lane), row(gdn_norm_gain), a_lane, b_lane)
    x2 = _mix_ffn(o_fox, o_gdn, w_out16, x2, B, S, row(g[1]), row(g[2]), w_up16, conv_w32, conv_b32,
                  w_down16, row(g[3]), 0, tm_ffn)

    assert hgrn_lb_logits.shape == (2, W)
    CW = W + 2 * M2_GROUPS * M2_STATE
    pad = jnp.zeros((D, SMALL_W - M2_HEADS), BF16)
    w16 = odd_w_in.astype(BF16)
    w_odd = jnp.concatenate([w16[:, 5 * W:5 * W + CW], w16[:, :5 * W], w16[:, 5 * W + CW:], pad],
                            axis=1)
    g = norm_gains[1]
    proj = _inproj_odd(x2, row(g[0]), w_odd, hgrn_lb_logits.astype(F32), m2_conv_w.astype(F32),
                       row(m2_conv_b), tm, S // tm)
    o_hgrn = _hgrn(proj, CW // W, B, S, row(hgrn_norm_gain))
    o_ssd = _ssd(proj, (CW + 5 * W) // W, (CW + 6 * W) // SMALL_W, B, S, _pad_lanes(m2_A_log, 0),
                 _pad_lanes(m2_dt_bias, 0), row(jnp.repeat(m2_D, M2_HEAD_DIM)), row(m2_norm_gain))
    x2 = _mix_ffn(o_hgrn, o_ssd, w_out16, x2, B, S, row(g[1]), row(g[2]), w_up16, conv_w32, conv_b32,
                  w_down16, row(g[3]), 1, tm_ffn)
    return x2.reshape(B, S, D).astype(x.dtype)
```
